```python
import math
import jax, jax.numpy as jnp
from jax import lax
import numpy as np


D_MODEL = 1024
BATCH = 4
SEQ = 4096
DEPTH = 1

GRID_W = 64
CTX_LEN = 256
D_MIX = D_MODEL
D_FOURIER = D_MIX // 4
D_DIFF = D_MIX - D_FOURIER
DIFF_HEAD_DIM = 64
N_DIFF_HEADS = D_DIFF // (2 * DIFF_HEAD_DIM)
N_FOURIER_GROUPS = 4
FOURIER_GROUP_DIM = D_FOURIER // N_FOURIER_GROUPS
D_IN_PROJ = 3 * D_DIFF + D_FOURIER
D_FF = 2816
N_MOD = 9
ROPE_BASE = 10000.0
ROPE_PAIRS = DIFF_HEAD_DIM // 4
Q_BLOCK = 128
RMS_EPS = 1e-6
ATTN_SCALE = DIFF_HEAD_DIM ** -0.5

kernel_name = 'hymba_diff_fnet_macaron_dit_block'


def _rmsnorm(x, g):
    x32 = x.astype(jnp.float32)
    y = x32 * lax.rsqrt(jnp.mean(x32 * x32, axis=-1, keepdims=True) + RMS_EPS)
    return y.astype(x.dtype) * g


def _modulate(h, shift, scale):
    return h * (1 + scale[:, None, :]) + shift[:, None, :]


def _half_ffn(s, shift, scale, gate, g, w_gate, w_up, w_down):
    h = _modulate(_rmsnorm(s, g), shift, scale)
    return s + 0.5 * gate[:, None, :] * ((jax.nn.silu(h @ w_gate) * (h @ w_up)) @ w_down)


def _rope_tables(n_tokens):
    rows = n_tokens // GRID_W
    row = jnp.repeat(jnp.arange(rows, dtype=jnp.float32), GRID_W)
    col = jnp.tile(jnp.arange(GRID_W, dtype=jnp.float32), rows)
    inv_freq = ROPE_BASE ** (-jnp.arange(ROPE_PAIRS, dtype=jnp.float32) / ROPE_PAIRS)
    ang = jnp.stack([row[:, None] * inv_freq, col[:, None] * inv_freq], axis=1)
    return jnp.cos(ang), jnp.sin(ang)


def _apply_rope(x, cos, sin):
    xs = x.astype(jnp.float32).reshape(x.shape[:-1] + (2, 2, ROPE_PAIRS))
    x1, x2 = xs[..., 0, :], xs[..., 1, :]
    out = jnp.stack([x1 * cos - x2 * sin, x2 * cos + x1 * sin], axis=-2)
    return out.reshape(x.shape).astype(x.dtype)


def _split_proj(p):
    b, n, _ = p.shape
    q = p[..., :D_DIFF].reshape(b, n, N_DIFF_HEADS, 2, DIFF_HEAD_DIM).transpose(0, 2, 3, 1, 4)
    k = p[..., D_DIFF:2 * D_DIFF].reshape(b, n, N_DIFF_HEADS, 2, DIFF_HEAD_DIM).transpose(0, 2, 3, 1, 4)
    v = p[..., 2 * D_DIFF:3 * D_DIFF].reshape(b, n, N_DIFF_HEADS, 2 * DIFF_HEAD_DIM).transpose(0, 2, 1, 3)
    f = p[..., 3 * D_DIFF:]
    return q, k, v, f


def _lambda(lq1, lk1, lq2, lk2, lambda_init):
    e1 = jnp.exp(jnp.sum(lq1.astype(jnp.float32) * lk1.astype(jnp.float32)))
    e2 = jnp.exp(jnp.sum(lq2.astype(jnp.float32) * lk2.astype(jnp.float32)))
    return e1 - e2 + lambda_init


def _diff_maps_apply(q, keys, vals, lam):
    s = jnp.einsum('bhcqd,bhckd->bhcqk', q, keys).astype(jnp.float32) * ATTN_SCALE
    p = jax.nn.softmax(s, axis=-1)
    a = p[:, :, 0] - lam * p[:, :, 1]
    return jnp.einsum('bhqk,bhkv->bhqv', a.astype(vals.dtype), vals)


def _diff_attn_latent(q, k, v, k_ctx, v_ctx, lam):
    b, h, _, n, dh = q.shape
    keys = jnp.concatenate([k, k_ctx], axis=3)
    vals = jnp.concatenate([v, v_ctx], axis=2)
    qb = jnp.moveaxis(q.reshape(b, h, 2, n // Q_BLOCK, Q_BLOCK, dh), 3, 0)
    out = lax.map(lambda qi: _diff_maps_apply(qi, keys, vals, lam), qb)
    return jnp.moveaxis(out, 0, 2).reshape(b, h, n, 2 * dh)


def _diff_heads_out(o, subln_g, lambda_init):
    b, h, n, dv = o.shape
    o = _rmsnorm(o, subln_g) * (1.0 - lambda_init)
    return o.transpose(0, 2, 1, 3).reshape(b, n, h * dv)


def _fourier_mix(f, w_fourier):
    b, n, _ = f.shape
    g = f.astype(jnp.float32).reshape(b, n, N_FOURIER_GROUPS, FOURIER_GROUP_DIM)
    z = jnp.fft.fft2(g, axes=(1, 3), norm='ortho').real
    return z.reshape(b, n, D_FOURIER).astype(f.dtype) @ w_fourier


def setup_inputs(seed: int = 0) -> dict:
    key = jax.random.key(seed)
    ks = jax.random.split(key, 24)
    f32 = jnp.float32

    def nrm(k, shape, fan_in, gain=1.0):
        return jax.random.normal(k, shape, f32) * (gain * fan_in ** -0.5)

    def gain(k, shape):
        return 1.0 + 0.05 * jax.random.normal(k, shape, f32)

    return {
        'x': jax.random.normal(ks[0], (BATCH, SEQ, D_MODEL), f32),
        'c': jax.random.normal(ks[1], (BATCH, D_MODEL), f32),
        'ctx': jax.random.normal(ks[2], (BATCH, CTX_LEN, D_MODEL), f32),
        'c_ctx': jax.random.normal(ks[3], (D_MODEL,), f32),
        'w_ada': nrm(ks[4], (DEPTH, D_MODEL, N_MOD * D_MODEL), D_MODEL, 0.5),
        'b_ada': 0.01 * jax.random.normal(ks[5], (DEPTH, N_MOD * D_MODEL), f32),
        'norm1_g': gain(ks[6], (DEPTH, D_MODEL)),
        'ffn1_w_gate': nrm(ks[7], (DEPTH, D_MODEL, D_FF), D_MODEL),
        'ffn1_w_up': nrm(ks[8], (DEPTH, D_MODEL, D_FF), D_MODEL),
        'ffn1_w_down': nrm(ks[9], (DEPTH, D_FF, D_MODEL), D_FF),
        'norm_mix_g': gain(ks[10], (DEPTH, D_MODEL)),
        'w_in': nrm(ks[11], (DEPTH, D_MODEL, D_IN_PROJ), D_MODEL),
        'lambda_q1': 0.1 * jax.random.normal(ks[12], (DEPTH, DIFF_HEAD_DIM), f32),
        'lambda_k1': 0.1 * jax.random.normal(ks[13], (DEPTH, DIFF_HEAD_DIM), f32),
        'lambda_q2': 0.1 * jax.random.normal(ks[14], (DEPTH, DIFF_HEAD_DIM), f32),
        'lambda_k2': 0.1 * jax.random.normal(ks[15], (DEPTH, DIFF_HEAD_DIM), f32),
        'subln_g': gain(ks[16], (DEPTH, 2 * DIFF_HEAD_DIM)),
        'w_fourier': nrm(ks[17], (DEPTH, D_FOURIER, D_FOURIER), D_FOURIER),
        'w_out': nrm(ks[18], (DEPTH, D_MIX, D_MODEL), D_MIX),
        'norm2_g': gain(ks[19], (DEPTH, D_MODEL)),
        'ffn2_w_gate': nrm(ks[20], (DEPTH, D_MODEL, D_FF), D_MODEL),
        'ffn2_w_up': nrm(ks[21], (DEPTH, D_MODEL, D_FF), D_MODEL),
        'ffn2_w_down': nrm(ks[22], (DEPTH, D_FF, D_MODEL), D_FF),
        'final_norm_g': gain(ks[23], (D_MODEL,)),
    }


def reference(x, c, ctx, c_ctx, w_ada, b_ada, norm1_g, ffn1_w_gate, ffn1_w_up, ffn1_w_down,
              norm_mix_g, w_in, lambda_q1, lambda_k1, lambda_q2, lambda_k2, subln_g, w_fourier,
              w_out, norm2_g, ffn2_w_gate, ffn2_w_up, ffn2_w_down, final_norm_g):
    cos, sin = _rope_tables(x.shape[1])
    h_ctx = ctx
    silu_c = jax.nn.silu(c)
    silu_cc = jax.nn.silu(c_ctx)[None, :]
    for l in range(DEPTH):
        lambda_init = 0.8 - 0.6 * math.exp(-0.3 * l)
        update_ctx = l < DEPTH - 1
        mod_x = jnp.split(silu_c @ w_ada[l] + b_ada[l], N_MOD, axis=-1)
        mod_c = jnp.split(silu_cc @ w_ada[l] + b_ada[l], N_MOD, axis=-1)

        x = _half_ffn(x, mod_x[0], mod_x[1], mod_x[2], norm1_g[l], ffn1_w_gate[l], ffn1_w_up[l], ffn1_w_down[l])
        h_ctx = _half_ffn(h_ctx, mod_c[0], mod_c[1], mod_c[2], norm1_g[l], ffn1_w_gate[l], ffn1_w_up[l], ffn1_w_down[l])

        p_x = _modulate(_rmsnorm(x, norm_mix_g[l]), mod_x[3], mod_x[4]) @ w_in[l]
        p_c = _modulate(_rmsnorm(h_ctx, norm_mix_g[l]), mod_c[3], mod_c[4]) @ w_in[l]
        q, k, v, f = _split_proj(p_x)
        qc, kc, vc, fc = _split_proj(p_c)
        q = _apply_rope(q, cos, sin)
        k = _apply_rope(k, cos, sin)
        lam = _lambda(lambda_q1[l], lambda_k1[l], lambda_q2[l], lambda_k2[l], lambda_init)

        att = _diff_heads_out(_diff_attn_latent(q, k, v, kc, vc, lam), subln_g[l], lambda_init)
        mix = jnp.concatenate([att, _fourier_mix(f, w_fourier[l])], axis=-1) @ w_out[l]
        x = x + mod_x[5][:, None, :] * mix

        if update_ctx:
            att_c = _diff_heads_out(_diff_maps_apply(qc, kc, vc, lam), subln_g[l], lambda_init)
            mix_c = jnp.concatenate([att_c, _fourier_mix(fc, w_fourier[l])], axis=-1) @ w_out[l]
            h_ctx = h_ctx + mod_c[5][:, None, :] * mix_c
            h_ctx = _half_ffn(h_ctx, mod_c[6], mod_c[7], mod_c[8], norm2_g[l], ffn2_w_gate[l], ffn2_w_up[l], ffn2_w_down[l])

        x = _half_ffn(x, mod_x[6], mod_x[7], mod_x[8], norm2_g[l], ffn2_w_gate[l], ffn2_w_up[l], ffn2_w_down[l])
    return _rmsnorm(x, final_norm_g)
```

```python
import functools
import math

import numpy as np
import jax
import jax.numpy as jnp
from jax import lax
from jax.experimental import pallas as pl
from jax.experimental.pallas import tpu as pltpu

D_MODEL = 1024
GRID_W = 64
D_FOURIER = 256
D_DIFF = 768
HEAD_DIM = 64
HEAD_W = 2 * HEAD_DIM
N_HEADS = D_DIFF // HEAD_W
FOURIER_GROUP_DIM = 64
D_IN_PROJ = 3 * D_DIFF + D_FOURIER
D_FF = 2816
N_MOD = 9
ROPE_BASE = 10000.0
ROPE_PAIRS = HEAD_DIM // 4
RMS_EPS = 1e-6
ATTN_SCALE = HEAD_DIM ** -0.5
LAMBDA_INIT = 0.8 - 0.6 * math.exp(-0.3 * 0)

F32 = jnp.float32
BF16 = jnp.bfloat16

V7X_VMEM_LIMIT_BYTES = 56 * 1024 * 1024
MOD_ROWS = 8
FF_CHUNKS = ((0, 1536), (1536, 1280))


def _const_spec(shape):
    return pl.BlockSpec(shape, lambda *_: (0,) * len(shape), pipeline_mode=pl.Buffered(1))


def _params(semantics):
    return pltpu.CompilerParams(dimension_semantics=semantics, vmem_limit_bytes=V7X_VMEM_LIMIT_BYTES)


def _rmsnorm(x, g):
    return x * lax.rsqrt(jnp.mean(x * x, axis=-1, keepdims=True) + RMS_EPS) * g


def _mod_row(mod_ref, k):
    return mod_ref[0, :, k * D_MODEL:(k + 1) * D_MODEL]


def _ada_kernel(cc_ref, w_ref, b_ref, o_ref):
    cc = cc_ref[...]
    s = cc * jax.nn.sigmoid(cc)
    o_ref[...] = jnp.dot(s, w_ref[...], preferred_element_type=F32,
                         precision=lax.Precision.HIGHEST) + b_ref[...]


def _ada(cc, w_ada, b_ada):
    n_out = w_ada.shape[1]
    tn = D_MODEL
    return pl.pallas_call(
        _ada_kernel,
        grid=(n_out // tn,),
        in_specs=[_const_spec((MOD_ROWS, D_MODEL)),
                  pl.BlockSpec((D_MODEL, tn), lambda j: (0, j)),
                  pl.BlockSpec((1, tn), lambda j: (0, j))],
        out_specs=pl.BlockSpec((MOD_ROWS, tn), lambda j: (0, j)),
        out_shape=jax.ShapeDtypeStruct((MOD_ROWS, n_out), F32),
        compiler_params=_params(("arbitrary",)),
        name="ada",
    )(cc, w_ada, b_ada)


def _ffn_kernel(*refs, mod_base, has_mix, final_norm):
    x_ref, mod_ref, g_ref, wg_ref, wu_ref, wd_ref = refs[:6]
    rest = list(refs[6:])
    x = x_ref[...]
    if has_mix:
        att_ref, zf_ref, wo_ref = rest[:3]
        rest = rest[3:]
        mix = jnp.dot(att_ref[...], wo_ref[:D_DIFF, :], preferred_element_type=F32)
        mix += jnp.dot(zf_ref[...], wo_ref[D_DIFF:, :], preferred_element_type=F32)
        x = x + _mod_row(mod_ref, 5) * mix
    if final_norm:
        gf_ref = rest[0]
        rest = rest[1:]
    (o_ref,) = rest

    h = _rmsnorm(x, g_ref[...]) * (1.0 + _mod_row(mod_ref, mod_base + 1)) + _mod_row(mod_ref, mod_base)
    h = h.astype(BF16)
    acc = None
    for start, width in FF_CHUNKS:
        gate = jnp.dot(h, wg_ref[:, start:start + width], preferred_element_type=F32)
        up = jnp.dot(h, wu_ref[:, start:start + width], preferred_element_type=F32)
        a = (gate * jax.nn.sigmoid(gate) * up).astype(BF16)
        part = jnp.dot(a, wd_ref[start:start + width, :], preferred_element_type=F32)
        acc = part if acc is None else acc + part
    x = x + (0.5 * _mod_row(mod_ref, mod_base + 2)) * acc
    if final_norm:
        x = _rmsnorm(x, gf_ref[...])
    o_ref[...] = x


def _ffn(x, mod3, mod_index, g, wg, wu, wd, *, tm, mod_base, mix=None, final_g=None):
    t = x.shape[0]
    tile = lambda w: pl.BlockSpec((tm, w), lambda i: (i, 0))
    in_specs = [tile(D_MODEL),
                pl.BlockSpec((1, 1, N_MOD * D_MODEL), lambda i: (mod_index(i), 0, 0)),
                _const_spec((1, D_MODEL)),
                _const_spec((D_MODEL, D_FF)), _const_spec((D_MODEL, D_FF)), _const_spec((D_FF, D_MODEL))]
    args = [x, mod3, g, wg, wu, wd]
    if mix is not None:
        att, zf, wo = mix
        in_specs += [tile(D_DIFF), tile(D_FOURIER), _const_spec((D_MODEL, D_MODEL))]
        args += [att, zf, wo]
    if final_g is not None:
        in_specs.append(_const_spec((1, D_MODEL)))
        args.append(final_g)
    kern = functools.partial(_ffn_kernel, mod_base=mod_base, has_mix=mix is not None,
                             final_norm=final_g is not None)
    return pl.pallas_call(
        kern,
        grid=(t // tm,),
        in_specs=in_specs,
        out_specs=tile(D_MODEL),
        out_shape=jax.ShapeDtypeStruct((t, D_MODEL), F32),
        compiler_params=_params(("parallel",)),
        name="ffn_mix" if mix is not None else "ffn",
    )(*args)


def _rope(x, cos, sin_lo, sin_hi):
    return (x * cos + pltpu.roll(x, HEAD_W - ROPE_PAIRS, 1) * sin_lo
            + pltpu.roll(x, ROPE_PAIRS, 1) * sin_hi)


def _inproj_kernel(*refs, latent):
    x_ref, mod_ref, g_ref, w_ref = refs[:4]
    h = _rmsnorm(x_ref[...], g_ref[...]) * (1.0 + _mod_row(mod_ref, 4)) + _mod_row(mod_ref, 3)
    p = jnp.dot(h.astype(BF16), w_ref[...], preferred_element_type=F32)
    if latent:
        cos_ref, slo_ref, shi_ref, dft_ref, q_ref, k_ref, v_ref, yc_ref, ys_ref = refs[4:]
        cos, slo, shi = cos_ref[...], slo_ref[...], shi_ref[...]
        for hd in range(N_HEADS):
            lo = hd * HEAD_W
            q_ref[:, lo:lo + HEAD_W] = (_rope(p[:, lo:lo + HEAD_W], cos, slo, shi) * ATTN_SCALE).astype(BF16)
            k_ref[:, lo:lo + HEAD_W] = _rope(p[:, D_DIFF + lo:D_DIFF + lo + HEAD_W], cos, slo, shi).astype(BF16)
        y = jnp.dot(p[:, 3 * D_DIFF:].astype(BF16), dft_ref[...], preferred_element_type=F32)
        yc_ref[...] = y[:, :D_FOURIER].astype(BF16)
        ys_ref[...] = y[:, D_FOURIER:].astype(BF16)
    else:
        k_ref, v_ref = refs[4:]
        k_ref[...] = p[:, D_DIFF:2 * D_DIFF].astype(BF16)
    v_ref[...] = p[:, 2 * D_DIFF:3 * D_DIFF].astype(BF16)


def _inproj(x, mod3, mod_index, g, w_in, *, tm, rope=None, chan_dft=None):
    t = x.shape[0]
    latent = rope is not None
    tile = lambda w: pl.BlockSpec((tm, w), lambda i: (i, 0))
    in_specs = [tile(D_MODEL),
                pl.BlockSpec((1, 1, N_MOD * D_MODEL), lambda i: (mod_index(i), 0, 0)),
                _const_spec((1, D_MODEL)),
                _const_spec((D_MODEL, D_IN_PROJ))]
    args = [x, mod3, g, w_in]
    wide = jax.ShapeDtypeStruct((t, D_DIFF), BF16)
    if latent:
        tiles_per_seq = rope[0].shape[0] // tm
        rope_spec = pl.BlockSpec((tm, HEAD_W), lambda i: (i % tiles_per_seq, 0))
        in_specs += [rope_spec, rope_spec, rope_spec, _const_spec((D_FOURIER, 2 * D_FOURIER))]
        args += [*rope, chan_dft]
        narrow = jax.ShapeDtypeStruct((t, D_FOURIER), BF16)
        out_shape = [wide, wide, wide, narrow, narrow]
        out_specs = [tile(D_DIFF)] * 3 + [tile(D_FOURIER)] * 2
    else:
        out_shape = [wide, wide]
        out_specs = [tile(D_DIFF)] * 2
    return pl.pallas_call(
        functools.partial(_inproj_kernel, latent=latent),
        grid=(t // tm,),
        in_specs=in_specs,
        out_specs=out_specs,
        out_shape=out_shape,
        compiler_params=_params(("parallel",)),
        name="inproj" if latent else "inproj_ctx",
    )(*args)


def _dot_nt(a, b):
    return lax.dot_general(a, b, (((1,), (1,)), ((), ())), preferred_element_type=F32)


def _attn_kernel(lam_ref, g_ref, q_ref, k_ref, kc_ref, v_ref, vc_ref, o_ref):
    lv = lam_ref[...]
    lam = (jnp.exp(jnp.sum(lv[0:1] * lv[1:2], axis=1, keepdims=True))
           - jnp.exp(jnp.sum(lv[2:3] * lv[3:4], axis=1, keepdims=True)) + LAMBDA_INIT)
    q = q_ref[0]
    first = lax.broadcasted_iota(jnp.int32, q.shape, 1) < HEAD_DIM
    zero = jnp.zeros_like(q)
    k, kc = k_ref[0], kc_ref[0]

    def softmax_parts(qm):
        sa, sb = _dot_nt(qm, k), _dot_nt(qm, kc)
        m = jnp.maximum(jnp.max(sa, axis=1, keepdims=True), jnp.max(sb, axis=1, keepdims=True))
        ea, eb = jnp.exp(sa - m), jnp.exp(sb - m)
        denom = jnp.sum(ea, axis=1, keepdims=True) + jnp.sum(eb, axis=1, keepdims=True)
        return ea, eb, denom

    e1a, e1b, l1 = softmax_parts(jnp.where(first, q, zero))
    e2a, e2b, l2 = softmax_parts(jnp.where(first, zero, q))
    r1 = 1.0 / l1
    r2 = lam / l2
    aa = (e1a * r1 - e2a * r2).astype(BF16)
    ab = (e1b * r1 - e2b * r2).astype(BF16)
    o = jnp.dot(aa, v_ref[0], preferred_element_type=F32) + jnp.dot(ab, vc_ref[0], preferred_element_type=F32)
    o_ref[0] = (_rmsnorm(o, g_ref[...]) * (1.0 - LAMBDA_INIT)).astype(BF16)


def _attention(lamvec, subln_g, q, k, kc, v, vc, *, tq):
    b, n, _ = q.shape
    c = kc.shape[1]
    head_block = lambda rows: pl.BlockSpec((1, rows, HEAD_W), lambda bi, hi, qi: (bi, 0, hi))
    return pl.pallas_call(
        _attn_kernel,
        grid=(b, N_HEADS, n // tq),
        in_specs=[_const_spec(lamvec.shape), _const_spec((1, HEAD_W)),
                  pl.BlockSpec((1, tq, HEAD_W), lambda bi, hi, qi: (bi, qi, hi)),
                  head_block(n), head_block(c), head_block(n), head_block(c)],
        out_specs=pl.BlockSpec((1, tq, HEAD_W), lambda bi, hi, qi: (bi, qi, hi)),
        out_shape=jax.ShapeDtypeStruct((b, n, D_DIFF), BF16),
        compiler_params=_params(("parallel", "parallel", "arbitrary")),
        name="diff_attn",
    )(lamvec, subln_g, q, k, kc, v, vc)


def _dft_kernel(cn_ref, msn_ref, yc_ref, ys_ref, wf_ref, o_ref):
    cn, msn = cn_ref[...], msn_ref[...]
    for b in range(yc_ref.shape[0]):
        z = jnp.dot(cn, yc_ref[b], preferred_element_type=F32)
        z += jnp.dot(msn, ys_ref[b], preferred_element_type=F32)
        o_ref[b] = jnp.dot(z.astype(BF16), wf_ref[...], preferred_element_type=F32).astype(BF16)


def _position_dft(cn, msn, yc, ys, wf, *, tr):
    b, n, w = yc.shape
    row_tile = pl.BlockSpec((tr, n), lambda i: (i, 0))
    return pl.pallas_call(
        _dft_kernel,
        grid=(n // tr,),
        in_specs=[row_tile, row_tile, _const_spec((b, n, w)), _const_spec((b, n, w)), _const_spec((w, w))],
        out_specs=pl.BlockSpec((b, tr, w), lambda i: (0, i, 0)),
        out_shape=jax.ShapeDtypeStruct((b, n, w), BF16),
        compiler_params=_params(("parallel",)),
        name="position_dft",
    )(cn, msn, yc, ys, wf)


def _rope_tables(n):
    lane = np.arange(HEAD_W)
    sub = lane % HEAD_DIM
    axis = sub // (2 * ROPE_PAIRS)
    second_half = (sub % (2 * ROPE_PAIRS)) // ROPE_PAIRS
    inv_freq = ROPE_BASE ** (-jnp.arange(ROPE_PAIRS, dtype=F32) / ROPE_PAIRS)
    tok = jnp.arange(n, dtype=jnp.int32)
    pos = jnp.where(jnp.asarray(axis)[None, :] == 0, (tok // GRID_W)[:, None], (tok % GRID_W)[:, None]).astype(F32)
    ang = pos * inv_freq[jnp.asarray(sub % ROPE_PAIRS)][None, :]
    sin = jnp.sin(ang)
    hi = jnp.asarray(second_half, dtype=bool)[None, :]
    return jnp.cos(ang), jnp.where(hi, 0.0, -sin), jnp.where(hi, sin, 0.0)


def _channel_dft_table():
    c = np.arange(FOURIER_GROUP_DIM)
    ang = 2.0 * np.pi * np.outer(c, c) / FOURIER_GROUP_DIM
    eye = np.eye(D_FOURIER // FOURIER_GROUP_DIM)
    scale = FOURIER_GROUP_DIM ** -0.5
    return np.concatenate([np.kron(eye, np.cos(ang)), np.kron(eye, np.sin(ang))], axis=1) * scale


def _position_dft_tables(n):
    idx = jnp.arange(n, dtype=jnp.int32)
    ang = ((idx[:, None] * idx[None, :]) % n).astype(F32) * (2.0 * np.pi / n)
    scale = n ** -0.5
    return (jnp.cos(ang) * scale).astype(BF16), (jnp.sin(ang) * -scale).astype(BF16)


def kernel(x, c, ctx, c_ctx, w_ada, b_ada, norm1_g, ffn1_w_gate, ffn1_w_up, ffn1_w_down, norm_mix_g, w_in,
           lambda_q1, lambda_k1, lambda_q2, lambda_k2, subln_g, w_fourier, w_out, norm2_g, ffn2_w_gate,
           ffn2_w_up, ffn2_w_down, final_norm_g):
    b, n, d = x.shape
    n_ctx = ctx.shape[1]
    assert (d, w_ada.shape[0]) == (D_MODEL, 1) and b + 1 <= MOD_ROWS
    tm = 512
    tiles_per_seq = n // tm
    latent_row = lambda i: i // tiles_per_seq
    ctx_row = lambda i: b
    row = lambda g: g.reshape(1, -1)
    bf = lambda w: w.astype(BF16)

    cc = jnp.zeros((MOD_ROWS, d), F32).at[:b].set(c).at[b].set(c_ctx)
    mod3 = _ada(cc, w_ada[0], b_ada).reshape(MOD_ROWS, 1, N_MOD * d)

    ffn1 = (row(norm1_g), bf(ffn1_w_gate[0]), bf(ffn1_w_up[0]), bf(ffn1_w_down[0]))
    x1 = _ffn(x.reshape(b * n, d), mod3, latent_row, *ffn1, tm=tm, mod_base=0)
    c1 = _ffn(ctx.reshape(b * n_ctx, d), mod3, ctx_row, *ffn1, tm=tm, mod_base=0)

    w_in_b = bf(w_in[0])
    chan_dft = jnp.asarray(_channel_dft_table(), dtype=F32).astype(BF16)
    q, k, v, yc, ys = _inproj(x1, mod3, latent_row, row(norm_mix_g), w_in_b, tm=tm,
                              rope=_rope_tables(n), chan_dft=chan_dft)
    kc, vc = _inproj(c1, mod3, ctx_row, row(norm_mix_g), w_in_b, tm=tm)

    lamvec = jnp.concatenate([lambda_q1, lambda_k1, lambda_q2, lambda_k2], axis=0)
    seq = lambda a, rows: a.reshape(b, rows, a.shape[-1])
    att = _attention(lamvec, row(subln_g), seq(q, n), seq(k, n), seq(kc, n_ctx), seq(v, n), seq(vc, n_ctx), tq=256)

    cn, msn = _position_dft_tables(n)
    zf = _position_dft(cn, msn, seq(yc, n), seq(ys, n), bf(w_fourier[0]), tr=256)

    out = _ffn(x1, mod3, latent_row, row(norm2_g), bf(ffn2_w_gate[0]), bf(ffn2_w_up[0]), bf(ffn2_w_down[0]),
               tm=tm, mod_base=6, mix=(att.reshape(b * n, D_DIFF), zf.reshape(b * n, D_FOURIER), bf(w_out[0])),
               final_g=row(final_norm_g))
    return out.reshape(b, n, d)
```

```python
import functools
import math

import numpy as np
import jax
import jax.numpy as jnp
from jax import lax
from jax.experimental import pallas as pl
from jax.experimental.pallas import tpu as pltpu

D_MODEL = 1024
GRID_W = 64
D_FOURIER = 256
D_DIFF = 768
HEAD_DIM = 64
HEAD_W = 2 * HEAD_DIM
N_HEADS = D_DIFF // HEAD_W
FOURIER_GROUP_DIM = 64
D_IN_PROJ = 3 * D_DIFF + D_FOURIER
D_FF = 2816
N_MOD = 9
ROPE_BASE = 10000.0
ROPE_PAIRS = HEAD_DIM // 4
RMS_EPS = 1e-6
ATTN_SCALE = HEAD_DIM ** -0.5
Q_SCALE = ATTN_SCALE * math.log2(math.e)
LAMBDA_INIT = 0.8 - 0.6 * math.exp(-0.3 * 0)
UNDERFLOW_GUARD = 2.0 ** -80

F32 = jnp.float32
BF16 = jnp.bfloat16

V7X_VMEM_LIMIT_BYTES = 56 * 1024 * 1024
MOD_ROWS = 8
FF_CHUNKS = ((0, 1536), (1536, 1280))


def _const_spec(shape):
    return pl.BlockSpec(shape, lambda *_: (0,) * len(shape), pipeline_mode=pl.Buffered(1))


def _params(semantics):
    return pltpu.CompilerParams(dimension_semantics=semantics, vmem_limit_bytes=V7X_VMEM_LIMIT_BYTES)


def _rmsnorm(x, g):
    return x * lax.rsqrt(jnp.mean(x * x, axis=-1, keepdims=True) + RMS_EPS) * g


def _mod_row(mod_ref, k):
    return mod_ref[0, :, k * D_MODEL:(k + 1) * D_MODEL]


def _ada_kernel(cc_ref, w_ref, b_ref, o_ref):
    cc = cc_ref[...]
    s = cc * jax.nn.sigmoid(cc)
    w = w_ref[...]
    s_hi, w_hi = s.astype(BF16), w.astype(BF16)
    s_lo, w_lo = (s - s_hi.astype(F32)).astype(BF16), (w - w_hi.astype(F32)).astype(BF16)
    dot = functools.partial(jnp.dot, preferred_element_type=F32)
    o_ref[...] = dot(s_hi, w_hi) + (dot(s_hi, w_lo) + dot(s_lo, w_hi)) + b_ref[...]


def _ada(cc, w_ada, b_ada):
    n_out = w_ada.shape[1]
    tn = D_MODEL
    return pl.pallas_call(
        _ada_kernel,
        grid=(n_out // tn,),
        in_specs=[_const_spec((MOD_ROWS, D_MODEL)),
                  pl.BlockSpec((D_MODEL, tn), lambda j: (0, j)),
                  pl.BlockSpec((1, tn), lambda j: (0, j))],
        out_specs=pl.BlockSpec((MOD_ROWS, tn), lambda j: (0, j)),
        out_shape=jax.ShapeDtypeStruct((MOD_ROWS, n_out), F32),
        compiler_params=_params(("arbitrary",)),
        name="ada",
    )(cc, w_ada, b_ada)


def _ffn_kernel(*refs, mod_base, has_mix, final_norm):
    x_ref, mod_ref, g_ref, wg_ref, wu_ref, wd_ref = refs[:6]
    rest = list(refs[6:])
    x = x_ref[...]
    if has_mix:
        att_ref, zf_ref, wo_ref = rest[:3]
        rest = rest[3:]
        mix = jnp.dot(att_ref[...], wo_ref[:D_DIFF, :], preferred_element_type=F32)
        mix += jnp.dot(zf_ref[...], wo_ref[D_DIFF:, :], preferred_element_type=F32)
        x = x + _mod_row(mod_ref, 5) * mix
    if final_norm:
        gf_ref = rest[0]
        rest = rest[1:]
    (o_ref,) = rest

    h = _rmsnorm(x, g_ref[...]) * (1.0 + _mod_row(mod_ref, mod_base + 1)) + _mod_row(mod_ref, mod_base)
    h = h.astype(BF16)
    acc = None
    for start, width in FF_CHUNKS:
        gate = jnp.dot(h, wg_ref[:, start:start + width], preferred_element_type=F32)
        up = jnp.dot(h, wu_ref[:, start:start + width], preferred_element_type=F32)
        a = (gate * jax.nn.sigmoid(gate) * up).astype(BF16)
        part = jnp.dot(a, wd_ref[start:start + width, :], preferred_element_type=F32)
        acc = part if acc is None else acc + part
    x = x + (0.5 * _mod_row(mod_ref, mod_base + 2)) * acc
    if final_norm:
        x = _rmsnorm(x, gf_ref[...])
    o_ref[...] = x


def _ffn(x, mod3, mod_index, g, wg, wu, wd, *, tm, mod_base, mix=None, final_g=None):
    t = x.shape[0]
    tile = lambda w: pl.BlockSpec((tm, w), lambda i: (i, 0))
    in_specs = [tile(D_MODEL),
                pl.BlockSpec((1, 1, N_MOD * D_MODEL), lambda i: (mod_index(i), 0, 0)),
                _const_spec((1, D_MODEL)),
                _const_spec((D_MODEL, D_FF)), _const_spec((D_MODEL, D_FF)), _const_spec((D_FF, D_MODEL))]
    args = [x, mod3, g, wg, wu, wd]
    if mix is not None:
        att, zf, wo = mix
        in_specs += [tile(D_DIFF), tile(D_FOURIER), _const_spec((D_MODEL, D_MODEL))]
        args += [att, zf, wo]
    if final_g is not None:
        in_specs.append(_const_spec((1, D_MODEL)))
        args.append(final_g)
    kern = functools.partial(_ffn_kernel, mod_base=mod_base, has_mix=mix is not None,
                             final_norm=final_g is not None)
    return pl.pallas_call(
        kern,
        grid=(t // tm,),
        in_specs=in_specs,
        out_specs=tile(D_MODEL),
        out_shape=jax.ShapeDtypeStruct((t, D_MODEL), F32),
        compiler_params=_params(("parallel",)),
        name="ffn_mix" if mix is not None else "ffn",
    )(*args)


def _rope(x, cos, sin_lo, sin_hi):
    return (x * cos + pltpu.roll(x, HEAD_W - ROPE_PAIRS, 1) * sin_lo
            + pltpu.roll(x, ROPE_PAIRS, 1) * sin_hi)


def _inproj_kernel(*refs, latent):
    x_ref, mod_ref, g_ref, w_ref = refs[:4]
    h = _rmsnorm(x_ref[...], g_ref[...]) * (1.0 + _mod_row(mod_ref, 4)) + _mod_row(mod_ref, 3)
    p = jnp.dot(h.astype(BF16), w_ref[...], preferred_element_type=F32)
    if latent:
        cos_ref, slo_ref, shi_ref, dft_ref, q_ref, k_ref, v_ref, yc_ref, ys_ref = refs[4:]
        cos, slo, shi = cos_ref[...], slo_ref[...], shi_ref[...]
        for hd in range(N_HEADS):
            lo = hd * HEAD_W
            q_ref[:, lo:lo + HEAD_W] = (_rope(p[:, lo:lo + HEAD_W], cos, slo, shi) * Q_SCALE).astype(BF16)
            k_ref[:, lo:lo + HEAD_W] = _rope(p[:, D_DIFF + lo:D_DIFF + lo + HEAD_W], cos, slo, shi).astype(BF16)
        y = jnp.dot(p[:, 3 * D_DIFF:].astype(BF16), dft_ref[...], preferred_element_type=F32)
        yc_ref[...] = y[:, :D_FOURIER].astype(BF16)
        ys_ref[...] = y[:, D_FOURIER:].astype(BF16)
    else:
        k_ref, v_ref = refs[4:]
        k_ref[...] = p[:, D_DIFF:2 * D_DIFF].astype(BF16)
    v_ref[...] = p[:, 2 * D_DIFF:3 * D_DIFF].astype(BF16)


def _inproj(x, mod3, mod_index, g, w_in, *, tm, rope=None, chan_dft=None):
    t = x.shape[0]
    latent = rope is not None
    tile = lambda w: pl.BlockSpec((tm, w), lambda i: (i, 0))
    in_specs = [tile(D_MODEL),
                pl.BlockSpec((1, 1, N_MOD * D_MODEL), lambda i: (mod_index(i), 0, 0)),
                _const_spec((1, D_MODEL)),
                _const_spec((D_MODEL, D_IN_PROJ))]
    args = [x, mod3, g, w_in]
    wide = jax.ShapeDtypeStruct((t, D_DIFF), BF16)
    if latent:
        tiles_per_seq = rope[0].shape[0] // tm
        rope_spec = pl.BlockSpec((tm, HEAD_W), lambda i: (i % tiles_per_seq, 0))
        in_specs += [rope_spec, rope_spec, rope_spec, _const_spec((D_FOURIER, 2 * D_FOURIER))]
        args += [*rope, chan_dft]
        narrow = jax.ShapeDtypeStruct((t, D_FOURIER), BF16)
        out_shape = [wide, wide, wide, narrow, narrow]
        out_specs = [tile(D_DIFF)] * 3 + [tile(D_FOURIER)] * 2
    else:
        out_shape = [wide, wide]
        out_specs = [tile(D_DIFF)] * 2
    return pl.pallas_call(
        functools.partial(_inproj_kernel, latent=latent),
        grid=(t // tm,),
        in_specs=in_specs,
        out_specs=out_specs,
        out_shape=out_shape,
        compiler_params=_params(("parallel",)),
        name="inproj" if latent else "inproj_ctx",
    )(*args)


def _max_key_norm_sq(kk):
    sq = kk.astype(F32) ** 2
    first = lax.broadcasted_iota(jnp.int32, sq.shape, 1) < HEAD_DIM
    biggest = lambda x: jnp.max(jnp.sum(x, axis=1, keepdims=True), axis=0, keepdims=True)
    return biggest(jnp.where(first, sq, 0.0)), biggest(jnp.where(first, 0.0, sq))


def _attn_kernel(lam_ref, g_ref, q_ref, k_ref, kc_ref, v_ref, vc_ref, o_ref, vt_ref, vct_ref, kn_ref, *, tk):
    tq = q_ref.shape[1]

    @pl.when(pl.program_id(2) == 0)
    def _():
        vt_ref[...] = v_ref[0].astype(F32).T.astype(BF16)
        vct_ref[...] = vc_ref[0].astype(F32).T.astype(BF16)
        (a1, a2), (b1, b2) = _max_key_norm_sq(k_ref[0]), _max_key_norm_sq(kc_ref[0])
        kn_ref[...] = jnp.sqrt(jnp.concatenate([jnp.broadcast_to(jnp.maximum(a1, b1), (1, tq)),
                                                 jnp.broadcast_to(jnp.maximum(a2, b2), (1, tq))], axis=1))

    lv = lam_ref[...]
    lam = (jnp.exp(jnp.sum(lv[0:1] * lv[1:2], axis=1, keepdims=True))
           - jnp.exp(jnp.sum(lv[2:3] * lv[3:4], axis=1, keepdims=True)) + LAMBDA_INIT)

    qt = q_ref[0].astype(F32).T
    top = lax.broadcasted_iota(jnp.int32, qt.shape, 0) < HEAD_DIM
    q1t, q2t = jnp.where(top, qt, 0.0), jnp.where(top, 0.0, qt)
    qcat = jnp.concatenate([q1t, q2t], axis=1).astype(BF16)
    chunks = [(k_ref[0, c * tk:(c + 1) * tk, :], vt_ref[:, c * tk:(c + 1) * tk]) for c in range(k_ref.shape[1] // tk)]
    chunks.append((kc_ref[0], vct_ref[...]))

    def scores(kk):
        return jnp.dot(kk, qcat, preferred_element_type=F32)

    def softmax_sums(shift):
        lsum, acc = 0.0, 0.0
        for kk, vt in chunks:
            e = jnp.exp2(scores(kk) - shift)
            lsum += jnp.sum(e.reshape(-1, 8, 2 * tq), axis=0)
            acc += jnp.dot(vt, e.astype(BF16), preferred_element_type=F32)
        return jnp.sum(lsum, axis=0, keepdims=True), acc

    def finish(l, acc):
        ot = acc[:, :tq] * (1.0 / l[:, :tq]) - acc[:, tq:] * (lam / l[:, tq:])
        o_ref[0] = (_rmsnorm(ot.T, g_ref[...]) * (1.0 - LAMBDA_INIT)).astype(BF16)

    qn = jnp.sqrt(jnp.concatenate([jnp.sum(q1t * q1t, axis=0, keepdims=True),
                                   jnp.sum(q2t * q2t, axis=0, keepdims=True)], axis=1))
    l, acc = softmax_sums(qn * kn_ref[...])
    finish(l, acc)

    @pl.when(jnp.min(l) < UNDERFLOW_GUARD)
    def _():
        m = None
        for kk, _ in chunks:
            cm = jnp.max(scores(kk), axis=0, keepdims=True)
            m = cm if m is None else jnp.maximum(m, cm)
        finish(*softmax_sums(m))


def _attention(lamvec, subln_g, q, k, kc, v, vc, *, tq, tk):
    b, n, _ = q.shape
    c = kc.shape[1]
    head_block = lambda rows: pl.BlockSpec((1, rows, HEAD_W), lambda bi, hi, qi: (bi, 0, hi))
    return pl.pallas_call(
        functools.partial(_attn_kernel, tk=tk),
        grid=(b, N_HEADS, n // tq),
        in_specs=[_const_spec(lamvec.shape), _const_spec((1, HEAD_W)),
                  pl.BlockSpec((1, tq, HEAD_W), lambda bi, hi, qi: (bi, qi, hi)),
                  head_block(n), head_block(c), head_block(n), head_block(c)],
        out_specs=pl.BlockSpec((1, tq, HEAD_W), lambda bi, hi, qi: (bi, qi, hi)),
        out_shape=jax.ShapeDtypeStruct((b, n, D_DIFF), BF16),
        scratch_shapes=[pltpu.VMEM((HEAD_W, n), BF16), pltpu.VMEM((HEAD_W, c), BF16),
                        pltpu.VMEM((1, 2 * tq), F32)],
        compiler_params=_params(("parallel", "parallel", "arbitrary")),
        name="diff_attn",
    )(lamvec, subln_g, q, k, kc, v, vc)


def _dft_kernel(cn_ref, msn_ref, yc_ref, ys_ref, wf_ref, o_ref):
    cn, msn = cn_ref[...], msn_ref[...]
    for b in range(yc_ref.shape[0]):
        z = jnp.dot(cn, yc_ref[b], preferred_element_type=F32)
        z += jnp.dot(msn, ys_ref[b], preferred_element_type=F32)
        o_ref[b] = jnp.dot(z.astype(BF16), wf_ref[...], preferred_element_type=F32).astype(BF16)


def _position_dft(cn, msn, yc, ys, wf, *, tr):
    b, n, w = yc.shape
    row_tile = pl.BlockSpec((tr, n), lambda i: (i, 0))
    return pl.pallas_call(
        _dft_kernel,
        grid=(n // tr,),
        in_specs=[row_tile, row_tile, _const_spec((b, n, w)), _const_spec((b, n, w)), _const_spec((w, w))],
        out_specs=pl.BlockSpec((b, tr, w), lambda i: (0, i, 0)),
        out_shape=jax.ShapeDtypeStruct((b, n, w), BF16),
        compiler_params=_params(("parallel",)),
        name="position_dft",
    )(cn, msn, yc, ys, wf)


def _rope_tables(n):
    lane = np.arange(HEAD_W)
    sub = lane % HEAD_DIM
    axis = sub // (2 * ROPE_PAIRS)
    second_half = (sub % (2 * ROPE_PAIRS)) // ROPE_PAIRS
    inv_freq = ROPE_BASE ** (-jnp.arange(ROPE_PAIRS, dtype=F32) / ROPE_PAIRS)
    tok = jnp.arange(n, dtype=jnp.int32)
    pos = jnp.where(jnp.asarray(axis)[None, :] == 0, (tok // GRID_W)[:, None], (tok % GRID_W)[:, None]).astype(F32)
    ang = pos * inv_freq[jnp.asarray(sub % ROPE_PAIRS)][None, :]
    sin = jnp.sin(ang)
    hi = jnp.asarray(second_half, dtype=bool)[None, :]
    return jnp.cos(ang), jnp.where(hi, 0.0, -sin), jnp.where(hi, sin, 0.0)


def _channel_dft_table():
    c = np.arange(FOURIER_GROUP_DIM)
    ang = 2.0 * np.pi * np.outer(c, c) / FOURIER_GROUP_DIM
    eye = np.eye(D_FOURIER // FOURIER_GROUP_DIM)
    scale = FOURIER_GROUP_DIM ** -0.5
    return np.concatenate([np.kron(eye, np.cos(ang)), np.kron(eye, np.sin(ang))], axis=1) * scale


def _position_dft_tables(n):
    idx = jnp.arange(n, dtype=jnp.int32)
    ang = ((idx[:, None] * idx[None, :]) % n).astype(F32) * (2.0 * np.pi / n)
    scale = n ** -0.5
    return (jnp.cos(ang) * scale).astype(BF16), (jnp.sin(ang) * -scale).astype(BF16)


def kernel(x, c, ctx, c_ctx, w_ada, b_ada, norm1_g, ffn1_w_gate, ffn1_w_up, ffn1_w_down, norm_mix_g, w_in,
           lambda_q1, lambda_k1, lambda_q2, lambda_k2, subln_g, w_fourier, w_out, norm2_g, ffn2_w_gate,
           ffn2_w_up, ffn2_w_down, final_norm_g):
    b, n, d = x.shape
    n_ctx = ctx.shape[1]
    assert (d, w_ada.shape[0]) == (D_MODEL, 1) and b + 1 <= MOD_ROWS
    tm = 512
    tiles_per_seq = n // tm
    latent_row = lambda i: i // tiles_per_seq
    ctx_row = lambda i: b
    row = lambda g: g.reshape(1, -1)
    bf = lambda w: w.astype(BF16)

    cc = jnp.zeros((MOD_ROWS, d), F32).at[:b].set(c).at[b].set(c_ctx)
    mod3 = _ada(cc, w_ada[0], b_ada).reshape(MOD_ROWS, 1, N_MOD * d)

    ffn1 = (row(norm1_g), bf(ffn1_w_gate[0]), bf(ffn1_w_up[0]), bf(ffn1_w_down[0]))
    x1 = _ffn(x.reshape(b * n, d), mod3, latent_row, *ffn1, tm=tm, mod_base=0)
    c1 = _ffn(ctx.reshape(b * n_ctx, d), mod3, ctx_row, *ffn1, tm=tm, mod_base=0)

    w_in_b = bf(w_in[0])
    chan_dft = jnp.asarray(_channel_dft_table(), dtype=F32).astype(BF16)
    q, k, v, yc, ys = _inproj(x1, mod3, latent_row, row(norm_mix_g), w_in_b, tm=tm,
                              rope=_rope_tables(n), chan_dft=chan_dft)
    kc, vc = _inproj(c1, mod3, ctx_row, row(norm_mix_g), w_in_b, tm=tm)

    lamvec = jnp.concatenate([lambda_q1, lambda_k1, lambda_q2, lambda_k2], axis=0)
    seq = lambda a, rows: a.reshape(b, rows, a.shape[-1])
    att = _attention(lamvec, row(subln_g), seq(q, n), seq(k, n), seq(kc, n_ctx), seq(v, n), seq(vc, n_ctx),
                     tq=256, tk=2048)

    cn, msn = _position_dft_tables(n)
    zf = _position_dft(cn, msn, seq(yc, n), seq(ys, n), bf(w_fourier[0]), tr=256)

    out = _ffn(x1, mod3, latent_row, row(norm2_g), bf(ffn2_w_gate[0]), bf(ffn2_w_up[0]), bf(ffn2_w_down[0]),
               tm=tm, mod_base=6, mix=(att.reshape(b * n, D_DIFF), zf.reshape(b * n, D_FOURIER), bf(w_out[0])),
               final_g=row(final_norm_g))
    return out.reshape(b, n, d)
```

```python
import functools
import math

import numpy as np
import jax
import jax.numpy as jnp
from jax import lax
from jax.experimental import pallas as pl
from jax.experimental.pallas import tpu as pltpu

D_MODEL = 1024
GRID_W = 64
D_FOURIER = 256
D_DIFF = 768
HEAD_DIM = 64
HEAD_W = 2 * HEAD_DIM
N_HEADS = D_DIFF // HEAD_W
FOURIER_GROUP_DIM = 64
D_IN_PROJ = 3 * D_DIFF + D_FOURIER
D_FF = 2816
N_MOD = 9
ROPE_BASE = 10000.0
ROPE_PAIRS = HEAD_DIM // 4
RMS_EPS = 1e-6
ATTN_SCALE = HEAD_DIM ** -0.5
Q_SCALE = ATTN_SCALE * math.log2(math.e)
LAMBDA_INIT = 0.8 - 0.6 * math.exp(-0.3 * 0)
UNDERFLOW_GUARD = 2.0 ** -80

F32 = jnp.float32
BF16 = jnp.bfloat16

V7X_VMEM_LIMIT_BYTES = 56 * 1024 * 1024
MOD_ROWS = 8
FF_CHUNKS = ((0, 1536), (1536, 1280))


def _const_spec(shape):
    return pl.BlockSpec(shape, lambda *_: (0,) * len(shape), pipeline_mode=pl.Buffered(1))


def _params(semantics):
    return pltpu.CompilerParams(dimension_semantics=semantics, vmem_limit_bytes=V7X_VMEM_LIMIT_BYTES)


def _rmsnorm(x, g):
    return x * lax.rsqrt(jnp.mean(x * x, axis=-1, keepdims=True) + RMS_EPS) * g


def _mod_row(mod_ref, k):
    return mod_ref[0, :, k * D_MODEL:(k + 1) * D_MODEL]


def _ada_kernel(cc_ref, w_ref, b_ref, o_ref):
    cc = cc_ref[...]
    s = cc * jax.nn.sigmoid(cc)
    w = w_ref[...]
    s_hi, w_hi = s.astype(BF16), w.astype(BF16)
    s_lo, w_lo = (s - s_hi.astype(F32)).astype(BF16), (w - w_hi.astype(F32)).astype(BF16)
    dot = functools.partial(jnp.dot, preferred_element_type=F32)
    o_ref[...] = dot(s_hi, w_hi) + (dot(s_hi, w_lo) + dot(s_lo, w_hi)) + b_ref[...]


def _ada(cc, w_ada, b_ada):
    n_out = w_ada.shape[1]
    tn = D_MODEL
    return pl.pallas_call(
        _ada_kernel,
        grid=(n_out // tn,),
        in_specs=[_const_spec((MOD_ROWS, D_MODEL)),
                  pl.BlockSpec((D_MODEL, tn), lambda j: (0, j)),
                  pl.BlockSpec((1, tn), lambda j: (0, j))],
        out_specs=pl.BlockSpec((MOD_ROWS, tn), lambda j: (0, j)),
        out_shape=jax.ShapeDtypeStruct((MOD_ROWS, n_out), F32),
        compiler_params=_params(("arbitrary",)),
        name="ada",
    )(cc, w_ada, b_ada)


def _ffn_kernel(*refs, mod_base, has_mix, final_norm):
    x_ref, mod_ref, g_ref, wg_ref, wu_ref, wd_ref = refs[:6]
    rest = list(refs[6:])
    x = x_ref[...]
    if has_mix:
        att_ref, zf_ref, wo_ref = rest[:3]
        rest = rest[3:]
        mix = jnp.dot(att_ref[...], wo_ref[:D_DIFF, :], preferred_element_type=F32)
        mix += jnp.dot(zf_ref[...], wo_ref[D_DIFF:, :], preferred_element_type=F32)
        x = x + _mod_row(mod_ref, 5) * mix
    if final_norm:
        gf_ref = rest[0]
        rest = rest[1:]
    (o_ref,) = rest

    h = _rmsnorm(x, g_ref[...]) * (1.0 + _mod_row(mod_ref, mod_base + 1)) + _mod_row(mod_ref, mod_base)
    h = h.astype(BF16)
    acc = None
    for start, width in FF_CHUNKS:
        gate = jnp.dot(h, wg_ref[:, start:start + width], preferred_element_type=F32)
        up = jnp.dot(h, wu_ref[:, start:start + width], preferred_element_type=F32)
        a = (gate * jax.nn.sigmoid(gate) * up).astype(BF16)
        part = jnp.dot(a, wd_ref[start:start + width, :], preferred_element_type=F32)
        acc = part if acc is None else acc + part
    x = x + (0.5 * _mod_row(mod_ref, mod_base + 2)) * acc
    if final_norm:
        x = _rmsnorm(x, gf_ref[...])
    o_ref[...] = x


def _ffn(x, mod3, mod_index, g, wg, wu, wd, *, tm, mod_base, mix=None, tiles_per_seq=None, final_g=None):
    t = x.shape[0]
    tile = lambda w: pl.BlockSpec((tm, w), lambda i: (i, 0))
    in_specs = [tile(D_MODEL),
                pl.BlockSpec((1, 1, N_MOD * D_MODEL), lambda i: (mod_index(i), 0, 0)),
                _const_spec((1, D_MODEL)),
                _const_spec((D_MODEL, D_FF)), _const_spec((D_MODEL, D_FF)), _const_spec((D_FF, D_MODEL))]
    args = [x, mod3, g, wg, wu, wd]
    if mix is not None:
        att, zf, wo = mix
        by_batch = pl.BlockSpec((tm, D_FOURIER), lambda i: (i % tiles_per_seq, i // tiles_per_seq))
        in_specs += [tile(D_DIFF), by_batch, _const_spec((D_MODEL, D_MODEL))]
        args += [att, zf, wo]
    if final_g is not None:
        in_specs.append(_const_spec((1, D_MODEL)))
        args.append(final_g)
    kern = functools.partial(_ffn_kernel, mod_base=mod_base, has_mix=mix is not None,
                             final_norm=final_g is not None)
    return pl.pallas_call(
        kern,
        grid=(t // tm,),
        in_specs=in_specs,
        out_specs=tile(D_MODEL),
        out_shape=jax.ShapeDtypeStruct((t, D_MODEL), F32),
        compiler_params=_params(("parallel",)),
        name="ffn_mix" if mix is not None else "ffn",
    )(*args)


def _rope(x, cos, sin_lo, sin_hi):
    return (x * cos + pltpu.roll(x, HEAD_W - ROPE_PAIRS, 1) * sin_lo
            + pltpu.roll(x, ROPE_PAIRS, 1) * sin_hi)


def _inproj_kernel(*refs, latent):
    x_ref, mod_ref, g_ref, w_ref = refs[:4]
    h = _rmsnorm(x_ref[...], g_ref[...]) * (1.0 + _mod_row(mod_ref, 4)) + _mod_row(mod_ref, 3)
    p = jnp.dot(h.astype(BF16), w_ref[...], preferred_element_type=F32)
    if latent:
        cos_ref, slo_ref, shi_ref, dft_ref, q_ref, k_ref, v_ref, yc_ref, ys_ref = refs[4:]
        cos, slo, shi = cos_ref[...], slo_ref[...], shi_ref[...]
        for hd in range(N_HEADS):
            lo = hd * HEAD_W
            q_ref[:, lo:lo + HEAD_W] = (_rope(p[:, lo:lo + HEAD_W], cos, slo, shi) * Q_SCALE).astype(BF16)
            k_ref[:, lo:lo + HEAD_W] = _rope(p[:, D_DIFF + lo:D_DIFF + lo + HEAD_W], cos, slo, shi).astype(BF16)
        y = jnp.dot(p[:, 3 * D_DIFF:].astype(BF16), dft_ref[...], preferred_element_type=F32)
        yc_ref[...] = y[:, :D_FOURIER].astype(BF16)
        ys_ref[...] = y[:, D_FOURIER:].astype(BF16)
    else:
        k_ref, v_ref = refs[4:]
        k_ref[...] = p[:, D_DIFF:2 * D_DIFF].astype(BF16)
    v_ref[...] = p[:, 2 * D_DIFF:3 * D_DIFF].astype(BF16)


def _inproj(x, mod3, mod_index, g, w_in, *, tm, rope=None, chan_dft=None):
    t = x.shape[0]
    latent = rope is not None
    tile = lambda w: pl.BlockSpec((tm, w), lambda i: (i, 0))
    in_specs = [tile(D_MODEL),
                pl.BlockSpec((1, 1, N_MOD * D_MODEL), lambda i: (mod_index(i), 0, 0)),
                _const_spec((1, D_MODEL)),
                _const_spec((D_MODEL, D_IN_PROJ))]
    args = [x, mod3, g, w_in]
    wide = jax.ShapeDtypeStruct((t, D_DIFF), BF16)
    if latent:
        tiles_per_seq = rope[0].shape[0] // tm
        rope_spec = pl.BlockSpec((tm, HEAD_W), lambda i: (i % tiles_per_seq, 0))
        in_specs += [rope_spec, rope_spec, rope_spec, _const_spec((D_FOURIER, 2 * D_FOURIER))]
        args += [*rope, chan_dft]
        narrow = jax.ShapeDtypeStruct((tiles_per_seq * tm, t // (tiles_per_seq * tm) * D_FOURIER), BF16)
        by_batch = pl.BlockSpec((tm, D_FOURIER), lambda i: (i % tiles_per_seq, i // tiles_per_seq))
        out_shape = [wide, wide, wide, narrow, narrow]
        out_specs = [tile(D_DIFF)] * 3 + [by_batch] * 2
    else:
        out_shape = [wide, wide]
        out_specs = [tile(D_DIFF)] * 2
    return pl.pallas_call(
        functools.partial(_inproj_kernel, latent=latent),
        grid=(t // tm,),
        in_specs=in_specs,
        out_specs=out_specs,
        out_shape=out_shape,
        compiler_params=_params(("parallel",)),
        name="inproj" if latent else "inproj_ctx",
    )(*args)


def _max_key_norm_sq(kk):
    sq = kk.astype(F32) ** 2
    first = lax.broadcasted_iota(jnp.int32, sq.shape, 1) < HEAD_DIM
    biggest = lambda x: jnp.max(jnp.sum(x, axis=1, keepdims=True), axis=0, keepdims=True)
    return biggest(jnp.where(first, sq, 0.0)), biggest(jnp.where(first, 0.0, sq))


def _attn_kernel(lam_ref, g_ref, q_ref, k_ref, kc_ref, v_ref, vc_ref, o_ref, vt_ref, vct_ref, kn_ref, *, tk):
    tq = q_ref.shape[1]

    @pl.when(pl.program_id(2) == 0)
    def _():
        vt_ref[...] = v_ref[0].astype(F32).T.astype(BF16)
        vct_ref[...] = vc_ref[0].astype(F32).T.astype(BF16)
        (a1, a2), (b1, b2) = _max_key_norm_sq(k_ref[0]), _max_key_norm_sq(kc_ref[0])
        kn_ref[...] = jnp.sqrt(jnp.concatenate([jnp.broadcast_to(jnp.maximum(a1, b1), (1, tq)),
                                                 jnp.broadcast_to(jnp.maximum(a2, b2), (1, tq))], axis=1))

    lv = lam_ref[...]
    lam = (jnp.exp(jnp.sum(lv[0:1] * lv[1:2], axis=1, keepdims=True))
           - jnp.exp(jnp.sum(lv[2:3] * lv[3:4], axis=1, keepdims=True)) + LAMBDA_INIT)

    qt = q_ref[0].astype(F32).T
    top = lax.broadcasted_iota(jnp.int32, qt.shape, 0) < HEAD_DIM
    q1t, q2t = jnp.where(top, qt, 0.0), jnp.where(top, 0.0, qt)
    qcat = jnp.concatenate([q1t, q2t], axis=1).astype(BF16)
    chunks = [(k_ref[0, c * tk:(c + 1) * tk, :], vt_ref[:, c * tk:(c + 1) * tk]) for c in range(k_ref.shape[1] // tk)]
    chunks.append((kc_ref[0], vct_ref[...]))

    def scores(kk):
        return jnp.dot(kk, qcat, preferred_element_type=F32)

    def softmax_sums(shift):
        lsum, acc = 0.0, 0.0
        for kk, vt in chunks:
            e = jnp.exp2(scores(kk) - shift)
            lsum += jnp.sum(e.reshape(-1, 8, 2 * tq), axis=0)
            acc += jnp.dot(vt, e.astype(BF16), preferred_element_type=F32)
        return jnp.sum(lsum, axis=0, keepdims=True), acc

    def finish(l, acc):
        ot = acc[:, :tq] * (1.0 / l[:, :tq]) - acc[:, tq:] * (lam / l[:, tq:])
        o_ref[0] = (_rmsnorm(ot.T, g_ref[...]) * (1.0 - LAMBDA_INIT)).astype(BF16)

    qn = jnp.sqrt(jnp.concatenate([jnp.sum(q1t * q1t, axis=0, keepdims=True),
                                   jnp.sum(q2t * q2t, axis=0, keepdims=True)], axis=1))
    l, acc = softmax_sums(qn * kn_ref[...])
    finish(l, acc)

    @pl.when(jnp.min(l) < UNDERFLOW_GUARD)
    def _():
        m = None
        for kk, _ in chunks:
            cm = jnp.max(scores(kk), axis=0, keepdims=True)
            m = cm if m is None else jnp.maximum(m, cm)
        finish(*softmax_sums(m))


def _attention(lamvec, subln_g, q, k, kc, v, vc, *, tq, tk):
    b, n, _ = q.shape
    c = kc.shape[1]
    head_block = lambda rows: pl.BlockSpec((1, rows, HEAD_W), lambda bi, hi, qi: (bi, 0, hi))
    return pl.pallas_call(
        functools.partial(_attn_kernel, tk=tk),
        grid=(b, N_HEADS, n // tq),
        in_specs=[_const_spec(lamvec.shape), _const_spec((1, HEAD_W)),
                  pl.BlockSpec((1, tq, HEAD_W), lambda bi, hi, qi: (bi, qi, hi)),
                  head_block(n), head_block(c), head_block(n), head_block(c)],
        out_specs=pl.BlockSpec((1, tq, HEAD_W), lambda bi, hi, qi: (bi, qi, hi)),
        out_shape=jax.ShapeDtypeStruct((b, n, D_DIFF), BF16),
        scratch_shapes=[pltpu.VMEM((HEAD_W, n), BF16), pltpu.VMEM((HEAD_W, c), BF16),
                        pltpu.VMEM((1, 2 * tq), F32)],
        compiler_params=_params(("parallel", "parallel", "arbitrary")),
        name="diff_attn",
    )(lamvec, subln_g, q, k, kc, v, vc)


def _rotate(c, m, rc, rs):
    return c * rc + m * rs, m * rc - c * rs


def _dft_kernel(seed_c_ref, seed_m_ref, rot_c_ref, rot_s_ref, yc_ref, ys_ref, wf_ref, o_ref, cb_ref, mb_ref):
    tr = cb_ref.shape[0]
    seed_rows = seed_c_ref.shape[0]
    n_doublings = (tr // seed_rows).bit_length() - 1
    i = pl.program_id(0)

    @pl.when(i == 0)
    def _():
        cb_ref[0:seed_rows] = seed_c_ref[...]
        mb_ref[0:seed_rows] = seed_m_ref[...]
        for j in range(n_doublings):
            rows = seed_rows << j
            cb_ref[rows:2 * rows], mb_ref[rows:2 * rows] = _rotate(
                cb_ref[0:rows], mb_ref[0:rows], rot_c_ref[j:j + 1], rot_s_ref[j:j + 1])

    ct, mt = _rotate(cb_ref[...], mb_ref[...], rot_c_ref[pl.ds(n_doublings + i, 1)], rot_s_ref[pl.ds(n_doublings + i, 1)])
    z = jnp.dot(ct.astype(BF16), yc_ref[...], preferred_element_type=F32)
    z += jnp.dot(mt.astype(BF16), ys_ref[...], preferred_element_type=F32)
    w = wf_ref.shape[0]
    for b in range(z.shape[1] // w):
        o_ref[:, b * w:(b + 1) * w] = jnp.dot(z[:, b * w:(b + 1) * w].astype(BF16), wf_ref[...],
                                             preferred_element_type=F32).astype(BF16)


def _position_dft(tables, yc, ys, wf, *, tr):
    n, cols = yc.shape
    seed_c, seed_m, rot_c, rot_s = tables
    return pl.pallas_call(
        _dft_kernel,
        grid=(n // tr,),
        in_specs=[_const_spec(seed_c.shape), _const_spec(seed_m.shape), _const_spec(rot_c.shape),
                  _const_spec(rot_s.shape), _const_spec((n, cols)), _const_spec((n, cols)), _const_spec(wf.shape)],
        out_specs=pl.BlockSpec((tr, cols), lambda i: (i, 0)),
        out_shape=jax.ShapeDtypeStruct((n, cols), BF16),
        scratch_shapes=[pltpu.VMEM((tr, n), F32), pltpu.VMEM((tr, n), F32)],
        compiler_params=_params(("arbitrary",)),
        name="position_dft",
    )(seed_c, seed_m, rot_c, rot_s, yc, ys, wf)


def _rope_tables(n):
    lane = np.arange(HEAD_W)
    sub = lane % HEAD_DIM
    axis = sub // (2 * ROPE_PAIRS)
    second_half = (sub % (2 * ROPE_PAIRS)) // ROPE_PAIRS
    inv_freq = ROPE_BASE ** (-jnp.arange(ROPE_PAIRS, dtype=F32) / ROPE_PAIRS)
    tok = jnp.arange(n, dtype=jnp.int32)
    pos = jnp.where(jnp.asarray(axis)[None, :] == 0, (tok // GRID_W)[:, None], (tok % GRID_W)[:, None]).astype(F32)
    ang = pos * inv_freq[jnp.asarray(sub % ROPE_PAIRS)][None, :]
    sin = jnp.sin(ang)
    hi = jnp.asarray(second_half, dtype=bool)[None, :]
    return jnp.cos(ang), jnp.where(hi, 0.0, -sin), jnp.where(hi, sin, 0.0)


def _channel_dft_table():
    c = np.arange(FOURIER_GROUP_DIM)
    ang = 2.0 * np.pi * np.outer(c, c) / FOURIER_GROUP_DIM
    eye = np.eye(D_FOURIER // FOURIER_GROUP_DIM)
    scale = FOURIER_GROUP_DIM ** -0.5
    return np.concatenate([np.kron(eye, np.cos(ang)), np.kron(eye, np.sin(ang))], axis=1) * scale


def _position_dft_tables(n, tr):
    seed_rows = 8
    col = jnp.arange(n, dtype=jnp.int32)[None, :]
    angle = lambda k: ((jnp.asarray(k, jnp.int32)[:, None] * col) % n).astype(F32) * (2.0 * np.pi / n)
    seed = angle(np.arange(seed_rows))
    k0 = [seed_rows << j for j in range((tr // seed_rows).bit_length() - 1)] + list(range(0, n, tr))
    rot = angle(np.asarray(k0))
    scale = n ** -0.5
    return jnp.cos(seed) * scale, jnp.sin(seed) * -scale, jnp.cos(rot), jnp.sin(rot)


def kernel(x, c, ctx, c_ctx, w_ada, b_ada, norm1_g, ffn1_w_gate, ffn1_w_up, ffn1_w_down, norm_mix_g, w_in,
           lambda_q1, lambda_k1, lambda_q2, lambda_k2, subln_g, w_fourier, w_out, norm2_g, ffn2_w_gate,
           ffn2_w_up, ffn2_w_down, final_norm_g):
    b, n, d = x.shape
    n_ctx = ctx.shape[1]
    assert (d, w_ada.shape[0]) == (D_MODEL, 1) and b + 1 <= MOD_ROWS
    tm = 512
    tiles_per_seq = n // tm
    latent_row = lambda i: i // tiles_per_seq
    ctx_row = lambda i: b
    row = lambda g: g.reshape(1, -1)
    bf = lambda w: w.astype(BF16)

    cc = jnp.zeros((MOD_ROWS, d), F32).at[:b].set(c).at[b].set(c_ctx)
    mod3 = _ada(cc, w_ada[0], b_ada).reshape(MOD_ROWS, 1, N_MOD * d)

    ffn1 = (row(norm1_g), bf(ffn1_w_gate[0]), bf(ffn1_w_up[0]), bf(ffn1_w_down[0]))
    x1 = _ffn(x.reshape(b * n, d), mod3, latent_row, *ffn1, tm=tm, mod_base=0)
    c1 = _ffn(ctx.reshape(b * n_ctx, d), mod3, ctx_row, *ffn1, tm=tm, mod_base=0)

    w_in_b = bf(w_in[0])
    chan_dft = jnp.asarray(_channel_dft_table(), dtype=F32).astype(BF16)
    q, k, v, yc, ys = _inproj(x1, mod3, latent_row, row(norm_mix_g), w_in_b, tm=tm,
                              rope=_rope_tables(n), chan_dft=chan_dft)
    kc, vc = _inproj(c1, mod3, ctx_row, row(norm_mix_g), w_in_b, tm=tm)

    lamvec = jnp.concatenate([lambda_q1, lambda_k1, lambda_q2, lambda_k2], axis=0)
    seq = lambda a, rows: a.reshape(b, rows, a.shape[-1])
    att = _attention(lamvec, row(subln_g), seq(q, n), seq(k, n), seq(kc, n_ctx), seq(v, n), seq(vc, n_ctx),
                     tq=256, tk=2048)

    tr = 256
    zf = _position_dft(_position_dft_tables(n, tr), yc, ys, bf(w_fourier[0]), tr=tr)

    out = _ffn(x1, mod3, latent_row, row(norm2_g), bf(ffn2_w_gate[0]), bf(ffn2_w_up[0]), bf(ffn2_w_down[0]),
               tm=tm, mod_base=6, mix=(att.reshape(b * n, D_DIFF), zf, bf(w_out[0])), tiles_per_seq=tiles_per_seq,
               final_g=row(final_norm_g))
    return out.reshape(b, n, d)
```

```python
import functools
import math

import numpy as np
import jax
import jax.numpy as jnp
from jax import lax
from jax.experimental import pallas as pl
from jax.experimental.pallas import tpu as pltpu

D_MODEL = 1024
GRID_W = 64
D_FOURIER = 256
D_DIFF = 768
HEAD_DIM = 64
HEAD_W = 2 * HEAD_DIM
N_HEADS = D_DIFF // HEAD_W
FOURIER_GROUP_DIM = 64
D_IN_PROJ = 3 * D_DIFF + D_FOURIER
D_FF = 2816
N_MOD = 9
ROPE_BASE = 10000.0
ROPE_PAIRS = HEAD_DIM // 4
RMS_EPS = 1e-6
ATTN_SCALE = HEAD_DIM ** -0.5
Q_SCALE = ATTN_SCALE * math.log2(math.e)
LAMBDA_INIT = 0.8 - 0.6 * math.exp(-0.3 * 0)
UNDERFLOW_GUARD = 2.0 ** -80

F32 = jnp.float32
BF16 = jnp.bfloat16

V7X_VMEM_LIMIT_BYTES = 56 * 1024 * 1024
BF16_SUBLANES = 16
STAGE_CHUNK_BYTES = 3 * 512 * 1024
MOD_ROWS = 8
FF_CHUNKS = ((0, 1536), (1536, 1280))


def _const_spec(shape):
    return pl.BlockSpec(shape, lambda *_: (0,) * len(shape), pipeline_mode=pl.Buffered(1))


def _params(semantics):
    return pltpu.CompilerParams(dimension_semantics=semantics, vmem_limit_bytes=V7X_VMEM_LIMIT_BYTES)


def _rmsnorm(x, g):
    return x * lax.rsqrt(jnp.mean(x * x, axis=-1, keepdims=True) + RMS_EPS) * g


def _mod_row(mod_ref, k):
    return mod_ref[0, :, k * D_MODEL:(k + 1) * D_MODEL]


def _ada_kernel(cc_ref, w_ref, b_ref, o_ref):
    cc = cc_ref[...]
    s = cc * jax.nn.sigmoid(cc)
    w = w_ref[...]
    s_hi, w_hi = s.astype(BF16), w.astype(BF16)
    s_lo, w_lo = (s - s_hi.astype(F32)).astype(BF16), (w - w_hi.astype(F32)).astype(BF16)
    dot = functools.partial(jnp.dot, preferred_element_type=F32)
    o_ref[...] = dot(s_hi, w_hi) + (dot(s_hi, w_lo) + dot(s_lo, w_hi)) + b_ref[...]


def _ada(cc, w_ada, b_ada):
    n_out = w_ada.shape[1]
    tn = D_MODEL
    return pl.pallas_call(
        _ada_kernel,
        grid=(n_out // tn,),
        in_specs=[_const_spec((MOD_ROWS, D_MODEL)),
                  pl.BlockSpec((D_MODEL, tn), lambda j: (0, j)),
                  pl.BlockSpec((1, tn), lambda j: (0, j))],
        out_specs=pl.BlockSpec((MOD_ROWS, tn), lambda j: (0, j)),
        out_shape=jax.ShapeDtypeStruct((MOD_ROWS, n_out), F32),
        compiler_params=_params(("arbitrary",)),
        name="ada",
    )(cc, w_ada, b_ada)


def _stage_rows(k, n):
    fits = [r for r in range(BF16_SUBLANES, k + 1, BF16_SUBLANES) if k % r == 0 and r * n * 4 <= STAGE_CHUNK_BYTES]
    return max(fits)


def _stage_weight(src_hbm, dst_ref, stage_ref, sem_ref):
    rows = stage_ref.shape[1]
    n_chunks = src_hbm.shape[0] // rows

    def copy(c):
        return pltpu.make_async_copy(src_hbm.at[pl.ds(c * rows, rows)], stage_ref.at[c % 2], sem_ref.at[c % 2])

    copy(0).start()
    for c in range(n_chunks):
        if c + 1 < n_chunks:
            copy(c + 1).start()
        copy(c).wait()
        dst_ref[c * rows:(c + 1) * rows, :] = stage_ref[c % 2].astype(BF16)


class _StagedWeights:
    def __init__(self, weights):
        self.weights = list(weights)
        self.in_specs = [pl.BlockSpec(memory_space=pl.ANY)] * len(self.weights)
        chunk_shapes = sorted({(_stage_rows(*w.shape), w.shape[1]) for w in self.weights})
        self.stage_of = [chunk_shapes.index((_stage_rows(*w.shape), w.shape[1])) for w in self.weights]
        self.scratch = ([pltpu.VMEM(w.shape, BF16) for w in self.weights]
                        + [pltpu.VMEM((2,) + s, F32) for s in chunk_shapes]
                        + [pltpu.SemaphoreType.DMA((2,))])

    def load(self, hbm_refs, scratch_refs):
        n = len(self.weights)
        dst, stages, sem = scratch_refs[:n], scratch_refs[n:-1], scratch_refs[-1]

        @pl.when(pl.program_id(0) == 0)
        def _():
            for src, d, s in zip(hbm_refs, dst, self.stage_of):
                _stage_weight(src, d, stages[s], sem)

        return dst


def _ffn_kernel(*refs, staged, mod_base, has_mix, final_norm):
    n_w = len(staged.weights)
    n_scratch = len(staged.scratch)
    x_ref, mod_ref, g_ref = refs[:3]
    w_hbm = refs[3:3 + n_w]
    rest = list(refs[3 + n_w:len(refs) - n_scratch])
    weights = staged.load(w_hbm, refs[len(refs) - n_scratch:])
    wg_ref, wu_ref, wd_ref = weights[:3]
    x = x_ref[...]
    if has_mix:
        att_ref, zf_ref = rest[:2]
        rest = rest[2:]
        wo_ref = weights[3]
        mix = jnp.dot(att_ref[...], wo_ref[:D_DIFF, :], preferred_element_type=F32)
        mix += jnp.dot(zf_ref[...], wo_ref[D_DIFF:, :], preferred_element_type=F32)
        x = x + _mod_row(mod_ref, 5) * mix
    if final_norm:
        gf_ref = rest[0]
        rest = rest[1:]
    (o_ref,) = rest

    h = _rmsnorm(x, g_ref[...]) * (1.0 + _mod_row(mod_ref, mod_base + 1)) + _mod_row(mod_ref, mod_base)
    h = h.astype(BF16)
    acc = None
    for start, width in FF_CHUNKS:
        gate = jnp.dot(h, wg_ref[:, start:start + width], preferred_element_type=F32)
        up = jnp.dot(h, wu_ref[:, start:start + width], preferred_element_type=F32)
        a = (gate * jax.nn.sigmoid(gate) * up).astype(BF16)
        part = jnp.dot(a, wd_ref[start:start + width, :], preferred_element_type=F32)
        acc = part if acc is None else acc + part
    x = x + (0.5 * _mod_row(mod_ref, mod_base + 2)) * acc
    if final_norm:
        x = _rmsnorm(x, gf_ref[...])
    o_ref[...] = x


def _ffn(x, mod3, mod_index, g, wg, wu, wd, *, tm, mod_base, mix=None, tiles_per_seq=None, final_g=None):
    t = x.shape[0]
    tile = lambda w: pl.BlockSpec((tm, w), lambda i: (i, 0))
    staged = _StagedWeights([wg, wu, wd] + ([mix[2]] if mix is not None else []))
    in_specs = [tile(D_MODEL),
                pl.BlockSpec((1, 1, N_MOD * D_MODEL), lambda i: (mod_index(i), 0, 0)),
                _const_spec((1, D_MODEL))] + staged.in_specs
    args = [x, mod3, g] + staged.weights
    if mix is not None:
        att, zf, _ = mix
        by_batch = pl.BlockSpec((tm, D_FOURIER), lambda i: (i % tiles_per_seq, i // tiles_per_seq))
        in_specs += [tile(D_DIFF), by_batch]
        args += [att, zf]
    if final_g is not None:
        in_specs.append(_const_spec((1, D_MODEL)))
        args.append(final_g)
    kern = functools.partial(_ffn_kernel, staged=staged, mod_base=mod_base, has_mix=mix is not None,
                             final_norm=final_g is not None)
    return pl.pallas_call(
        kern,
        grid=(t // tm,),
        in_specs=in_specs,
        out_specs=tile(D_MODEL),
        out_shape=jax.ShapeDtypeStruct((t, D_MODEL), F32),
        scratch_shapes=staged.scratch,
        compiler_params=_params(("arbitrary",)),
        name="ffn_mix" if mix is not None else "ffn",
    )(*args)


def _rope(x, cos, sin_lo, sin_hi):
    return (x * cos + pltpu.roll(x, HEAD_W - ROPE_PAIRS, 1) * sin_lo
            + pltpu.roll(x, ROPE_PAIRS, 1) * sin_hi)


def _inproj_kernel(*refs, staged, latent):
    n_scratch = len(staged.scratch)
    x_ref, mod_ref, g_ref = refs[:3]
    (w_ref,) = staged.load(refs[3:4], refs[len(refs) - n_scratch:])
    refs = refs[:len(refs) - n_scratch]
    h = _rmsnorm(x_ref[...], g_ref[...]) * (1.0 + _mod_row(mod_ref, 4)) + _mod_row(mod_ref, 3)
    p = jnp.dot(h.astype(BF16), w_ref[...], preferred_element_type=F32)
    if latent:
        cos_ref, slo_ref, shi_ref, dft_ref, q_ref, k_ref, v_ref, yc_ref, ys_ref = refs[4:]
        cos, slo, shi = cos_ref[...], slo_ref[...], shi_ref[...]
        for hd in range(N_HEADS):
            lo = hd * HEAD_W
            q_ref[:, lo:lo + HEAD_W] = (_rope(p[:, lo:lo + HEAD_W], cos, slo, shi) * Q_SCALE).astype(BF16)
            k_ref[:, lo:lo + HEAD_W] = _rope(p[:, D_DIFF + lo:D_DIFF + lo + HEAD_W], cos, slo, shi).astype(BF16)
        y = jnp.dot(p[:, 3 * D_DIFF:].astype(BF16), dft_ref[...], preferred_element_type=F32)
        yc_ref[...] = y[:, :D_FOURIER].astype(BF16)
        ys_ref[...] = y[:, D_FOURIER:].astype(BF16)
    else:
        k_ref, v_ref = refs[4:]
        k_ref[...] = p[:, D_DIFF:2 * D_DIFF].astype(BF16)
    v_ref[...] = p[:, 2 * D_DIFF:3 * D_DIFF].astype(BF16)


def _inproj(x, mod3, mod_index, g, w_in, *, tm, rope=None, chan_dft=None):
    t = x.shape[0]
    latent = rope is not None
    tile = lambda w: pl.BlockSpec((tm, w), lambda i: (i, 0))
    staged = _StagedWeights([w_in])
    in_specs = [tile(D_MODEL),
                pl.BlockSpec((1, 1, N_MOD * D_MODEL), lambda i: (mod_index(i), 0, 0)),
                _const_spec((1, D_MODEL))] + staged.in_specs
    args = [x, mod3, g, w_in]
    wide = jax.ShapeDtypeStruct((t, D_DIFF), BF16)
    if latent:
        tiles_per_seq = rope[0].shape[0] // tm
        rope_spec = pl.BlockSpec((tm, HEAD_W), lambda i: (i % tiles_per_seq, 0))
        in_specs += [rope_spec, rope_spec, rope_spec, _const_spec((D_FOURIER, 2 * D_FOURIER))]
        args += [*rope, chan_dft]
        narrow = jax.ShapeDtypeStruct((tiles_per_seq * tm, t // (tiles_per_seq * tm) * D_FOURIER), BF16)
        by_batch = pl.BlockSpec((tm, D_FOURIER), lambda i: (i % tiles_per_seq, i // tiles_per_seq))
        out_shape = [wide, wide, wide, narrow, narrow]
        out_specs = [tile(D_DIFF)] * 3 + [by_batch] * 2
    else:
        out_shape = [wide, wide]
        out_specs = [tile(D_DIFF)] * 2
    return pl.pallas_call(
        functools.partial(_inproj_kernel, staged=staged, latent=latent),
        grid=(t // tm,),
        in_specs=in_specs,
        out_specs=out_specs,
        out_shape=out_shape,
        scratch_shapes=staged.scratch,
        compiler_params=_params(("arbitrary",)),
        name="inproj" if latent else "inproj_ctx",
    )(*args)


def _max_key_norm_sq(kk):
    sq = kk.astype(F32) ** 2
    first = lax.broadcasted_iota(jnp.int32, sq.shape, 1) < HEAD_DIM
    biggest = lambda x: jnp.max(jnp.sum(x, axis=1, keepdims=True), axis=0, keepdims=True)
    return biggest(jnp.where(first, sq, 0.0)), biggest(jnp.where(first, 0.0, sq))


def _attn_kernel(lam_ref, g_ref, q_ref, k_ref, kc_ref, v_ref, vc_ref, o_ref, vt_ref, vct_ref, kn_ref, *, tk):
    tq = q_ref.shape[1]

    @pl.when(pl.program_id(2) == 0)
    def _():
        vt_ref[...] = v_ref[0].astype(F32).T.astype(BF16)
        vct_ref[...] = vc_ref[0].astype(F32).T.astype(BF16)
        (a1, a2), (b1, b2) = _max_key_norm_sq(k_ref[0]), _max_key_norm_sq(kc_ref[0])
        kn_ref[...] = jnp.sqrt(jnp.concatenate([jnp.broadcast_to(jnp.maximum(a1, b1), (1, tq)),
                                                 jnp.broadcast_to(jnp.maximum(a2, b2), (1, tq))], axis=1))

    lv = lam_ref[...]
    lam = (jnp.exp(jnp.sum(lv[0:1] * lv[1:2], axis=1, keepdims=True))
           - jnp.exp(jnp.sum(lv[2:3] * lv[3:4], axis=1, keepdims=True)) + LAMBDA_INIT)

    qt = q_ref[0].astype(F32).T
    top = lax.broadcasted_iota(jnp.int32, qt.shape, 0) < HEAD_DIM
    q1t, q2t = jnp.where(top, qt, 0.0), jnp.where(top, 0.0, qt)
    qcat = jnp.concatenate([q1t, q2t], axis=1).astype(BF16)
    chunks = [(k_ref[0, c * tk:(c + 1) * tk, :], vt_ref[:, c * tk:(c + 1) * tk]) for c in range(k_ref.shape[1] // tk)]
    chunks.append((kc_ref[0], vct_ref[...]))

    def scores(kk):
        return jnp.dot(kk, qcat, preferred_element_type=F32)

    def softmax_sums(shift):
        lsum, acc = 0.0, 0.0
        for kk, vt in chunks:
            e = jnp.exp2(scores(kk) - shift)
            lsum += jnp.sum(e.reshape(-1, 8, 2 * tq), axis=0)
            acc += jnp.dot(vt, e.astype(BF16), preferred_element_type=F32)
        return jnp.sum(lsum, axis=0, keepdims=True), acc

    def finish(l, acc):
        ot = acc[:, :tq] * (1.0 / l[:, :tq]) - acc[:, tq:] * (lam / l[:, tq:])
        o_ref[0] = (_rmsnorm(ot.T, g_ref[...]) * (1.0 - LAMBDA_INIT)).astype(BF16)

    qn = jnp.sqrt(jnp.concatenate([jnp.sum(q1t * q1t, axis=0, keepdims=True),
                                   jnp.sum(q2t * q2t, axis=0, keepdims=True)], axis=1))
    l, acc = softmax_sums(qn * kn_ref[...])
    finish(l, acc)

    @pl.when(jnp.min(l) < UNDERFLOW_GUARD)
    def _():
        m = None
        for kk, _ in chunks:
            cm = jnp.max(scores(kk), axis=0, keepdims=True)
            m = cm if m is None else jnp.maximum(m, cm)
        finish(*softmax_sums(m))


def _attention(lamvec, subln_g, q, k, kc, v, vc, *, tq, tk):
    b, n, _ = q.shape
    c = kc.shape[1]
    head_block = lambda rows: pl.BlockSpec((1, rows, HEAD_W), lambda bi, hi, qi: (bi, 0, hi))
    return pl.pallas_call(
        functools.partial(_attn_kernel, tk=tk),
        grid=(b, N_HEADS, n // tq),
        in_specs=[_const_spec(lamvec.shape), _const_spec((1, HEAD_W)),
                  pl.BlockSpec((1, tq, HEAD_W), lambda bi, hi, qi: (bi, qi, hi)),
                  head_block(n), head_block(c), head_block(n), head_block(c)],
        out_specs=pl.BlockSpec((1, tq, HEAD_W), lambda bi, hi, qi: (bi, qi, hi)),
        out_shape=jax.ShapeDtypeStruct((b, n, D_DIFF), BF16),
        scratch_shapes=[pltpu.VMEM((HEAD_W, n), BF16), pltpu.VMEM((HEAD_W, c), BF16),
                        pltpu.VMEM((1, 2 * tq), F32)],
        compiler_params=_params(("parallel", "parallel", "arbitrary")),
        name="diff_attn",
    )(lamvec, subln_g, q, k, kc, v, vc)


def _rotate(c, m, rc, rs):
    return c * rc + m * rs, m * rc - c * rs


def _dft_kernel(seed_c_ref, seed_m_ref, rot_c_ref, rot_s_ref, yc_ref, ys_ref, wf_ref, o_ref, cb_ref, mb_ref):
    tr = cb_ref.shape[0]
    seed_rows = seed_c_ref.shape[0]
    n_doublings = (tr // seed_rows).bit_length() - 1
    i = pl.program_id(0)

    @pl.when(i == 0)
    def _():
        cb_ref[0:seed_rows] = seed_c_ref[...]
        mb_ref[0:seed_rows] = seed_m_ref[...]
        for j in range(n_doublings):
            rows = seed_rows << j
            cb_ref[rows:2 * rows], mb_ref[rows:2 * rows] = _rotate(
                cb_ref[0:rows], mb_ref[0:rows], rot_c_ref[j:j + 1], rot_s_ref[j:j + 1])

    ct, mt = _rotate(cb_ref[...], mb_ref[...], rot_c_ref[pl.ds(n_doublings + i, 1)], rot_s_ref[pl.ds(n_doublings + i, 1)])
    z = jnp.dot(ct.astype(BF16), yc_ref[...], preferred_element_type=F32)
    z += jnp.dot(mt.astype(BF16), ys_ref[...], preferred_element_type=F32)
    w = wf_ref.shape[0]
    wf = wf_ref[...].astype(BF16)
    for b in range(z.shape[1] // w):
        o_ref[:, b * w:(b + 1) * w] = jnp.dot(z[:, b * w:(b + 1) * w].astype(BF16), wf,
                                             preferred_element_type=F32).astype(BF16)


def _position_dft(tables, yc, ys, wf, *, tr):
    n, cols = yc.shape
    seed_c, seed_m, rot_c, rot_s = tables
    return pl.pallas_call(
        _dft_kernel,
        grid=(n // tr,),
        in_specs=[_const_spec(seed_c.shape), _const_spec(seed_m.shape), _const_spec(rot_c.shape),
                  _const_spec(rot_s.shape), _const_spec((n, cols)), _const_spec((n, cols)), _const_spec(wf.shape)],
        out_specs=pl.BlockSpec((tr, cols), lambda i: (i, 0)),
        out_shape=jax.ShapeDtypeStruct((n, cols), BF16),
        scratch_shapes=[pltpu.VMEM((tr, n), F32), pltpu.VMEM((tr, n), F32)],
        compiler_params=_params(("arbitrary",)),
        name="position_dft",
    )(seed_c, seed_m, rot_c, rot_s, yc, ys, wf)


def _rope_tables(n):
    lane = np.arange(HEAD_W)
    sub = lane % HEAD_DIM
    axis = sub // (2 * ROPE_PAIRS)
    second_half = (sub % (2 * ROPE_PAIRS)) // ROPE_PAIRS
    inv_freq = ROPE_BASE ** (-jnp.arange(ROPE_PAIRS, dtype=F32) / ROPE_PAIRS)
    tok = jnp.arange(n, dtype=jnp.int32)
    pos = jnp.where(jnp.asarray(axis)[None, :] == 0, (tok // GRID_W)[:, None], (tok % GRID_W)[:, None]).astype(F32)
    ang = pos * inv_freq[jnp.asarray(sub % ROPE_PAIRS)][None, :]
    sin = jnp.sin(ang)
    hi = jnp.asarray(second_half, dtype=bool)[None, :]
    return jnp.cos(ang), jnp.where(hi, 0.0, -sin), jnp.where(hi, sin, 0.0)


def _channel_dft_table():
    c = np.arange(FOURIER_GROUP_DIM)
    ang = 2.0 * np.pi * np.outer(c, c) / FOURIER_GROUP_DIM
    eye = np.eye(D_FOURIER // FOURIER_GROUP_DIM)
    scale = FOURIER_GROUP_DIM ** -0.5
    return np.concatenate([np.kron(eye, np.cos(ang)), np.kron(eye, np.sin(ang))], axis=1) * scale


def _position_dft_tables(n, tr):
    seed_rows = 8
    col = jnp.arange(n, dtype=jnp.int32)[None, :]
    angle = lambda k: ((jnp.asarray(k, jnp.int32)[:, None] * col) % n).astype(F32) * (2.0 * np.pi / n)
    seed = angle(np.arange(seed_rows))
    k0 = [seed_rows << j for j in range((tr // seed_rows).bit_length() - 1)] + list(range(0, n, tr))
    rot = angle(np.asarray(k0))
    scale = n ** -0.5
    return jnp.cos(seed) * scale, jnp.sin(seed) * -scale, jnp.cos(rot), jnp.sin(rot)


def kernel(x, c, ctx, c_ctx, w_ada, b_ada, norm1_g, ffn1_w_gate, ffn1_w_up, ffn1_w_down, norm_mix_g, w_in,
           lambda_q1, lambda_k1, lambda_q2, lambda_k2, subln_g, w_fourier, w_out, norm2_g, ffn2_w_gate,
           ffn2_w_up, ffn2_w_down, final_norm_g):
    b, n, d = x.shape
    n_ctx = ctx.shape[1]
    assert (d, w_ada.shape[0]) == (D_MODEL, 1) and b + 1 <= MOD_ROWS
    tm = 512
    tiles_per_seq = n // tm
    latent_row = lambda i: i // tiles_per_seq
    ctx_row = lambda i: b
    row = lambda g: g.reshape(1, -1)

    cc = jnp.zeros((MOD_ROWS, d), F32).at[:b].set(c).at[b].set(c_ctx)
    mod3 = _ada(cc, w_ada[0], b_ada).reshape(MOD_ROWS, 1, N_MOD * d)

    ffn1 = (row(norm1_g), ffn1_w_gate[0], ffn1_w_up[0], ffn1_w_down[0])
    x1 = _ffn(x.reshape(b * n, d), mod3, latent_row, *ffn1, tm=tm, mod_base=0)
    c1 = _ffn(ctx.reshape(b * n_ctx, d), mod3, ctx_row, *ffn1, tm=tm, mod_base=0)

    chan_dft = jnp.asarray(_channel_dft_table(), dtype=F32).astype(BF16)
    q, k, v, yc, ys = _inproj(x1, mod3, latent_row, row(norm_mix_g), w_in[0], tm=tm,
                              rope=_rope_tables(n), chan_dft=chan_dft)
    kc, vc = _inproj(c1, mod3, ctx_row, row(norm_mix_g), w_in[0], tm=tm)

    lamvec = jnp.concatenate([lambda_q1, lambda_k1, lambda_q2, lambda_k2], axis=0)
    seq = lambda a, rows: a.reshape(b, rows, a.shape[-1])
    att = _attention(lamvec, row(subln_g), seq(q, n), seq(k, n), seq(kc, n_ctx), seq(v, n), seq(vc, n_ctx),
                     tq=1024, tk=1024)

    tr = 256
    zf = _position_dft(_position_dft_tables(n, tr), yc, ys, w_fourier[0], tr=tr)

    out = _ffn(x1, mod3, latent_row, row(norm2_g), ffn2_w_gate[0], ffn2_w_up[0], ffn2_w_down[0],
               tm=tm, mod_base=6, mix=(att.reshape(b * n, D_DIFF), zf, w_out[0]), tiles_per_seq=tiles_per_seq,
               final_g=row(final_norm_g))
    return out.reshape(b, n, d)
```

```python
import functools
import math

import numpy as np
import jax
import jax.numpy as jnp
from jax import lax
from jax.experimental import pallas as pl
from jax.experimental.pallas import tpu as pltpu

D_MODEL = 1024
GRID_W = 64
D_FOURIER = 256
D_DIFF = 768
HEAD_DIM = 64
HEAD_W = 2 * HEAD_DIM
N_HEADS = D_DIFF // HEAD_W
FOURIER_GROUP_DIM = 64
D_IN_PROJ = 3 * D_DIFF + D_FOURIER
D_FF = 2816
N_MOD = 9
ROPE_BASE = 10000.0
ROPE_PAIRS = HEAD_DIM // 4
RMS_EPS = 1e-6
ATTN_SCALE = HEAD_DIM ** -0.5
Q_SCALE = ATTN_SCALE * math.log2(math.e)
LAMBDA_INIT = 0.8 - 0.6 * math.exp(-0.3 * 0)
UNDERFLOW_GUARD = 2.0 ** -80

F32 = jnp.float32
BF16 = jnp.bfloat16

V7X_VMEM_LIMIT_BYTES = 56 * 1024 * 1024
BF16_SUBLANES = 16
STAGE_CHUNK_BYTES = 3 * 512 * 1024
MOD_ROWS = 8
FF_CHUNKS = ((0, 1536), (1536, 1280))


def _const_spec(shape):
    return pl.BlockSpec(shape, lambda *_: (0,) * len(shape), pipeline_mode=pl.Buffered(1))


def _params(semantics):
    return pltpu.CompilerParams(dimension_semantics=semantics, vmem_limit_bytes=V7X_VMEM_LIMIT_BYTES)


def _rmsnorm(x, g):
    return x * lax.rsqrt(jnp.mean(x * x, axis=-1, keepdims=True) + RMS_EPS) * g


def _mod_row(mod_ref, k):
    return mod_ref[0, :, k * D_MODEL:(k + 1) * D_MODEL]


def _ada_kernel(cc_ref, w_ref, b_ref, o_ref):
    cc = cc_ref[...]
    s = cc * jax.nn.sigmoid(cc)
    w = w_ref[...]
    s_hi, w_hi = s.astype(BF16), w.astype(BF16)
    s_lo, w_lo = (s - s_hi.astype(F32)).astype(BF16), (w - w_hi.astype(F32)).astype(BF16)
    dot = functools.partial(jnp.dot, preferred_element_type=F32)
    o_ref[...] = dot(s_hi, w_hi) + (dot(s_hi, w_lo) + dot(s_lo, w_hi)) + b_ref[...]


def _ada(cc, w_ada, b_ada):
    n_out = w_ada.shape[1]
    tn = D_MODEL
    return pl.pallas_call(
        _ada_kernel,
        grid=(n_out // tn,),
        in_specs=[_const_spec((MOD_ROWS, D_MODEL)),
                  pl.BlockSpec((D_MODEL, tn), lambda j: (0, j)),
                  pl.BlockSpec((1, tn), lambda j: (0, j))],
        out_specs=pl.BlockSpec((MOD_ROWS, tn), lambda j: (0, j)),
        out_shape=jax.ShapeDtypeStruct((MOD_ROWS, n_out), F32),
        compiler_params=_params(("arbitrary",)),
        name="ada",
    )(cc, w_ada, b_ada)


def _stage_rows(k, n):
    fits = [r for r in range(BF16_SUBLANES, k + 1, BF16_SUBLANES) if k % r == 0 and r * n * 4 <= STAGE_CHUNK_BYTES]
    return max(fits)


def _stage_weight(src_hbm, dst_ref, stage_ref, sem_ref):
    rows = stage_ref.shape[1]
    n_chunks = src_hbm.shape[0] // rows

    def copy(c):
        return pltpu.make_async_copy(src_hbm.at[pl.ds(c * rows, rows)], stage_ref.at[c % 2], sem_ref.at[c % 2])

    copy(0).start()
    for c in range(n_chunks):
        if c + 1 < n_chunks:
            copy(c + 1).start()
        copy(c).wait()
        dst_ref[c * rows:(c + 1) * rows, :] = stage_ref[c % 2].astype(BF16)


class _StagedWeights:
    def __init__(self, weights):
        self.weights = list(weights)
        self.in_specs = [pl.BlockSpec(memory_space=pl.ANY)] * len(self.weights)
        chunk_shapes = sorted({(_stage_rows(*w.shape), w.shape[1]) for w in self.weights})
        self.stage_of = [chunk_shapes.index((_stage_rows(*w.shape), w.shape[1])) for w in self.weights]
        self.scratch = ([pltpu.VMEM(w.shape, BF16) for w in self.weights]
                        + [pltpu.VMEM((2,) + s, F32) for s in chunk_shapes]
                        + [pltpu.SemaphoreType.DMA((2,))])

    def load(self, hbm_refs, scratch_refs):
        n = len(self.weights)
        dst, stages, sem = scratch_refs[:n], scratch_refs[n:-1], scratch_refs[-1]

        @pl.when(pl.program_id(0) == 0)
        def _():
            for src, d, s in zip(hbm_refs, dst, self.stage_of):
                _stage_weight(src, d, stages[s], sem)

        return dst


def _ffn_kernel(*refs, staged, mod_base, has_mix, final_norm, n_main):
    n_w = len(staged.weights)
    n_scratch = len(staged.scratch)
    x_ref, mod_ref, g_ref = refs[:3]
    w_hbm = refs[3:3 + n_w]
    rest = list(refs[3 + n_w:len(refs) - n_scratch])
    weights = staged.load(w_hbm, refs[len(refs) - n_scratch:])
    wg_ref, wu_ref, wd_ref = weights[:3]
    x = x_ref[...]
    has_tail = n_main is not None
    if has_tail:
        tail_ref = rest[0]
        tail_o_ref = rest[-1]
        rest = rest[1:-1]
        in_tail = pl.program_id(0) >= n_main
        x = jnp.where(in_tail, tail_ref[...], x)
    if has_mix:
        att_ref, zf_ref = rest[:2]
        rest = rest[2:]
        wo_ref = weights[3]
        mix = jnp.dot(att_ref[...], wo_ref[:D_DIFF, :], preferred_element_type=F32)
        mix += jnp.dot(zf_ref[...], wo_ref[D_DIFF:, :], preferred_element_type=F32)
        x = x + _mod_row(mod_ref, 5) * mix
    if final_norm:
        gf_ref = rest[0]
        rest = rest[1:]
    (o_ref,) = rest

    h = _rmsnorm(x, g_ref[...]) * (1.0 + _mod_row(mod_ref, mod_base + 1)) + _mod_row(mod_ref, mod_base)
    h = h.astype(BF16)
    acc = None
    for start, width in FF_CHUNKS:
        gate = jnp.dot(h, wg_ref[:, start:start + width], preferred_element_type=F32)
        up = jnp.dot(h, wu_ref[:, start:start + width], preferred_element_type=F32)
        a = (gate * jax.nn.sigmoid(gate) * up).astype(BF16)
        part = jnp.dot(a, wd_ref[start:start + width, :], preferred_element_type=F32)
        acc = part if acc is None else acc + part
    x = x + (0.5 * _mod_row(mod_ref, mod_base + 2)) * acc
    if final_norm:
        x = _rmsnorm(x, gf_ref[...])
    if has_tail:
        @pl.when(in_tail)
        def _():
            tail_o_ref[...] = x

        @pl.when(jnp.logical_not(in_tail))
        def _():
            o_ref[...] = x
    else:
        o_ref[...] = x


def _ffn(x, mod3, mod_index, g, wg, wu, wd, *, tm, mod_base, tail=None, mix=None, tiles_per_seq=None, final_g=None):
    t = x.shape[0]
    n_main = t // tm
    n_tiles = n_main + (tail.shape[0] // tm if tail is not None else 0)
    tile = lambda w: pl.BlockSpec((tm, w), lambda i: (jnp.minimum(i, n_main - 1), 0))
    tail_tile = pl.BlockSpec((tm, D_MODEL), lambda i: (jnp.maximum(i - n_main, 0), 0))
    staged = _StagedWeights([wg, wu, wd] + ([mix[2]] if mix is not None else []))
    in_specs = [tile(D_MODEL),
                pl.BlockSpec((1, 1, N_MOD * D_MODEL), lambda i: (mod_index(i), 0, 0)),
                _const_spec((1, D_MODEL))] + staged.in_specs
    args = [x, mod3, g] + staged.weights
    out_specs, out_shape = tile(D_MODEL), jax.ShapeDtypeStruct((t, D_MODEL), F32)
    if tail is not None:
        in_specs.append(tail_tile)
        args.append(tail)
        out_specs, out_shape = [out_specs, tail_tile], [out_shape, jax.ShapeDtypeStruct(tail.shape, F32)]
    if mix is not None:
        att, zf, _ = mix
        by_batch = pl.BlockSpec((tm, D_FOURIER), lambda i: (i % tiles_per_seq, i // tiles_per_seq))
        in_specs += [tile(D_DIFF), by_batch]
        args += [att, zf]
    if final_g is not None:
        in_specs.append(_const_spec((1, D_MODEL)))
        args.append(final_g)
    kern = functools.partial(_ffn_kernel, staged=staged, mod_base=mod_base, has_mix=mix is not None,
                             final_norm=final_g is not None, n_main=n_main if tail is not None else None)
    return pl.pallas_call(
        kern,
        grid=(n_tiles,),
        in_specs=in_specs,
        out_specs=out_specs,
        out_shape=out_shape,
        scratch_shapes=staged.scratch,
        compiler_params=_params(("arbitrary",)),
        name="ffn_mix" if mix is not None else "ffn",
    )(*args)


def _rope(x, cos, sin_lo, sin_hi):
    return (x * cos + pltpu.roll(x, HEAD_W - ROPE_PAIRS, 1) * sin_lo
            + pltpu.roll(x, ROPE_PAIRS, 1) * sin_hi)


def _inproj_kernel(x_ref, c_ref, mod_ref, g_ref, w_hbm, cos_ref, slo_ref, shi_ref, dft_ref,
                   q_ref, k_ref, v_ref, yc_ref, ys_ref, kc_ref, vc_ref, *scratch, staged, n_latent):
    (w_ref,) = staged.load([w_hbm], scratch)
    is_ctx = pl.program_id(0) >= n_latent
    x = jnp.where(is_ctx, c_ref[...], x_ref[...])
    h = _rmsnorm(x, g_ref[...]) * (1.0 + _mod_row(mod_ref, 4)) + _mod_row(mod_ref, 3)
    p = jnp.dot(h.astype(BF16), w_ref[...], preferred_element_type=F32)

    @pl.when(jnp.logical_not(is_ctx))
    def _():
        cos, slo, shi = cos_ref[...], slo_ref[...], shi_ref[...]
        for hd in range(N_HEADS):
            lo = hd * HEAD_W
            q_ref[:, lo:lo + HEAD_W] = (_rope(p[:, lo:lo + HEAD_W], cos, slo, shi) * Q_SCALE).astype(BF16)
            k_ref[:, lo:lo + HEAD_W] = _rope(p[:, D_DIFF + lo:D_DIFF + lo + HEAD_W], cos, slo, shi).astype(BF16)
        v_ref[...] = p[:, 2 * D_DIFF:3 * D_DIFF].astype(BF16)
        y = jnp.dot(p[:, 3 * D_DIFF:].astype(BF16), dft_ref[...], preferred_element_type=F32)
        yc_ref[...] = y[:, :D_FOURIER].astype(BF16)
        ys_ref[...] = y[:, D_FOURIER:].astype(BF16)

    @pl.when(is_ctx)
    def _():
        kc_ref[...] = p[:, D_DIFF:2 * D_DIFF].astype(BF16)
        vc_ref[...] = p[:, 2 * D_DIFF:3 * D_DIFF].astype(BF16)


def _inproj(x, ctx, mod3, mod_index, g, w_in, rope, chan_dft, *, tm):
    t, tc = x.shape[0], ctx.shape[0]
    n = rope[0].shape[0]
    tiles_per_seq = n // tm
    n_latent = t // tm
    last = n_latent - 1
    tile = lambda w: pl.BlockSpec((tm, w), lambda i: (jnp.minimum(i, last), 0))
    ctx_tile = lambda w: pl.BlockSpec((tm, w), lambda i: (jnp.maximum(i - n_latent, 0), 0))
    rope_spec = pl.BlockSpec((tm, HEAD_W), lambda i: (i % tiles_per_seq, 0))
    by_batch = pl.BlockSpec((tm, D_FOURIER), lambda i: (jnp.minimum(i, last) % tiles_per_seq,
                                                        jnp.minimum(i, last) // tiles_per_seq))
    staged = _StagedWeights([w_in])
    wide = lambda rows: jax.ShapeDtypeStruct((rows, D_DIFF), BF16)
    narrow = jax.ShapeDtypeStruct((n, t // n * D_FOURIER), BF16)
    return pl.pallas_call(
        functools.partial(_inproj_kernel, staged=staged, n_latent=n_latent),
        grid=(n_latent + tc // tm,),
        in_specs=[tile(D_MODEL), ctx_tile(D_MODEL),
                  pl.BlockSpec((1, 1, N_MOD * D_MODEL), lambda i: (mod_index(i), 0, 0)),
                  _const_spec((1, D_MODEL))] + staged.in_specs
                 + [rope_spec, rope_spec, rope_spec, _const_spec((D_FOURIER, 2 * D_FOURIER))],
        out_specs=[tile(D_DIFF)] * 3 + [by_batch] * 2 + [ctx_tile(D_DIFF)] * 2,
        out_shape=[wide(t)] * 3 + [narrow] * 2 + [wide(tc)] * 2,
        scratch_shapes=staged.scratch,
        compiler_params=_params(("arbitrary",)),
        name="inproj",
    )(x, ctx, mod3, g, w_in, *rope, chan_dft)


def _max_key_norm_sq(kk):
    sq = (kk.astype(F32) ** 2).astype(BF16)
    r = lax.broadcasted_iota(jnp.int32, (HEAD_W, HEAD_W), 0)
    c = lax.broadcasted_iota(jnp.int32, (HEAD_W, HEAD_W), 1)
    sel = jnp.where((c == 0) == (r < HEAD_DIM), 1.0, 0.0) * jnp.where(c < 2, 1.0, 0.0)
    sums = jnp.dot(sq, sel.astype(BF16), preferred_element_type=F32)
    biggest = jnp.max(sums, axis=0, keepdims=True) * (1.0 + 2.0 ** -7)
    return biggest[:, 0:1], biggest[:, 1:2]


def _attn_kernel(lam_ref, g_ref, q_ref, k_ref, kc_ref, v_ref, vc_ref, o_ref, vt_ref, vct_ref, kn_ref, *, tk):
    tq = q_ref.shape[1]

    @pl.when(pl.program_id(2) == 0)
    def _():
        vt_ref[...] = v_ref[0].T
        vct_ref[...] = vc_ref[0].T
        (a1, a2), (b1, b2) = _max_key_norm_sq(k_ref[0]), _max_key_norm_sq(kc_ref[0])
        kn_ref[...] = jnp.sqrt(jnp.concatenate([jnp.broadcast_to(jnp.maximum(a1, b1), (1, tq)),
                                                 jnp.broadcast_to(jnp.maximum(a2, b2), (1, tq))], axis=1))

    lv = lam_ref[...]
    lam = (jnp.exp(jnp.sum(lv[0:1] * lv[1:2], axis=1, keepdims=True))
           - jnp.exp(jnp.sum(lv[2:3] * lv[3:4], axis=1, keepdims=True)) + LAMBDA_INIT)

    qt = q_ref[0].astype(F32).T
    top = lax.broadcasted_iota(jnp.int32, qt.shape, 0) < HEAD_DIM
    q1t, q2t = jnp.where(top, qt, 0.0), jnp.where(top, 0.0, qt)
    qcat = jnp.concatenate([q1t, q2t], axis=1).astype(BF16)
    chunks = [(k_ref[0, c * tk:(c + 1) * tk, :], vt_ref[:, c * tk:(c + 1) * tk]) for c in range(k_ref.shape[1] // tk)]
    chunks.append((kc_ref[0], vct_ref[...]))

    def scores(kk):
        return jnp.dot(kk, qcat, preferred_element_type=F32)

    def softmax_sums(shift):
        lsum, acc = 0.0, 0.0
        for kk, vt in chunks:
            e = jnp.exp2(scores(kk) - shift)
            lsum += jnp.sum(e.reshape(-1, 8, 2 * tq), axis=0)
            acc += jnp.dot(vt, e.astype(BF16), preferred_element_type=F32)
        return jnp.sum(lsum, axis=0, keepdims=True), acc

    def finish(l, acc):
        ot = acc[:, :tq] * (1.0 / l[:, :tq]) - acc[:, tq:] * (lam / l[:, tq:])
        o_ref[0] = (_rmsnorm(ot.T, g_ref[...]) * (1.0 - LAMBDA_INIT)).astype(BF16)

    qn = jnp.sqrt(jnp.concatenate([jnp.sum(q1t * q1t, axis=0, keepdims=True),
                                   jnp.sum(q2t * q2t, axis=0, keepdims=True)], axis=1))
    l, acc = softmax_sums(qn * kn_ref[...])
    finish(l, acc)

    @pl.when(jnp.min(l) < UNDERFLOW_GUARD)
    def _():
        m = None
        for kk, _ in chunks:
            cm = jnp.max(scores(kk), axis=0, keepdims=True)
            m = cm if m is None else jnp.maximum(m, cm)
        finish(*softmax_sums(m))


def _attention(lamvec, subln_g, q, k, kc, v, vc, *, tq, tk):
    b, n, _ = q.shape
    c = kc.shape[1]
    head_block = lambda rows: pl.BlockSpec((1, rows, HEAD_W), lambda bi, hi, qi: (bi, 0, hi))
    return pl.pallas_call(
        functools.partial(_attn_kernel, tk=tk),
        grid=(b, N_HEADS, n // tq),
        in_specs=[_const_spec(lamvec.shape), _const_spec((1, HEAD_W)),
                  pl.BlockSpec((1, tq, HEAD_W), lambda bi, hi, qi: (bi, qi, hi)),
                  head_block(n), head_block(c), head_block(n), head_block(c)],
        out_specs=pl.BlockSpec((1, tq, HEAD_W), lambda bi, hi, qi: (bi, qi, hi)),
        out_shape=jax.ShapeDtypeStruct((b, n, D_DIFF), BF16),
        scratch_shapes=[pltpu.VMEM((HEAD_W, n), BF16), pltpu.VMEM((HEAD_W, c), BF16),
                        pltpu.VMEM((1, 2 * tq), F32)],
        compiler_params=_params(("parallel", "parallel", "arbitrary")),
        name="diff_attn",
    )(lamvec, subln_g, q, k, kc, v, vc)


def _rotate(c, m, rc, rs):
    return c * rc + m * rs, m * rc - c * rs


def _dft_kernel(seed_c_ref, seed_m_ref, rot_c_ref, rot_s_ref, yc_ref, ys_ref, wf_ref, o_ref, cb_ref, mb_ref):
    tr = cb_ref.shape[0]
    seed_rows = seed_c_ref.shape[0]
    n_doublings = (tr // seed_rows).bit_length() - 1
    i = pl.program_id(0)

    @pl.when(i == 0)
    def _():
        cb_ref[0:seed_rows] = seed_c_ref[...]
        mb_ref[0:seed_rows] = seed_m_ref[...]
        for j in range(n_doublings):
            rows = seed_rows << j
            cb_ref[rows:2 * rows], mb_ref[rows:2 * rows] = _rotate(
                cb_ref[0:rows], mb_ref[0:rows], rot_c_ref[j:j + 1], rot_s_ref[j:j + 1])

    ct, mt = _rotate(cb_ref[...], mb_ref[...], rot_c_ref[pl.ds(n_doublings + i, 1)], rot_s_ref[pl.ds(n_doublings + i, 1)])
    z = jnp.dot(ct.astype(BF16), yc_ref[...], preferred_element_type=F32)
    z += jnp.dot(mt.astype(BF16), ys_ref[...], preferred_element_type=F32)
    w = wf_ref.shape[0]
    wf = wf_ref[...].astype(BF16)
    for b in range(z.shape[1] // w):
        o_ref[:, b * w:(b + 1) * w] = jnp.dot(z[:, b * w:(b + 1) * w].astype(BF16), wf,
                                             preferred_element_type=F32).astype(BF16)


def _position_dft(tables, yc, ys, wf, *, tr):
    n, cols = yc.shape
    seed_c, seed_m, rot_c, rot_s = tables
    return pl.pallas_call(
        _dft_kernel,
        grid=(n // tr,),
        in_specs=[_const_spec(seed_c.shape), _const_spec(seed_m.shape), _const_spec(rot_c.shape),
                  _const_spec(rot_s.shape), _const_spec((n, cols)), _const_spec((n, cols)), _const_spec(wf.shape)],
        out_specs=pl.BlockSpec((tr, cols), lambda i: (i, 0)),
        out_shape=jax.ShapeDtypeStruct((n, cols), BF16),
        scratch_shapes=[pltpu.VMEM((tr, n), F32), pltpu.VMEM((tr, n), F32)],
        compiler_params=_params(("arbitrary",)),
        name="position_dft",
    )(seed_c, seed_m, rot_c, rot_s, yc, ys, wf)


def _rope_tables(n):
    rows = n // GRID_W
    lane = np.arange(HEAD_W)
    sub = lane % HEAD_DIM
    on_row_axis = jnp.asarray(sub // (2 * ROPE_PAIRS) == 0)[None, :]
    second_half = jnp.asarray((sub % (2 * ROPE_PAIRS)) // ROPE_PAIRS == 1)[None, :]
    inv_freq = (ROPE_BASE ** (-jnp.arange(ROPE_PAIRS, dtype=F32) / ROPE_PAIRS))[jnp.asarray(sub % ROPE_PAIRS)]

    def table(fn, keep):
        def side(size, mine):
            t = fn(jnp.arange(size, dtype=F32)[:, None] * inv_freq[None, :])
            return jnp.where(mine & keep, t, 0.0)
        return (side(rows, on_row_axis)[:, None, :] + side(GRID_W, ~on_row_axis)[None, :, :]).reshape(n, HEAD_W)

    return (table(jnp.cos, True), table(lambda a: -jnp.sin(a), ~second_half), table(jnp.sin, second_half))


def _channel_dft_table():
    c = np.arange(FOURIER_GROUP_DIM)
    ang = 2.0 * np.pi * np.outer(c, c) / FOURIER_GROUP_DIM
    eye = np.eye(D_FOURIER // FOURIER_GROUP_DIM)
    scale = FOURIER_GROUP_DIM ** -0.5
    return np.concatenate([np.kron(eye, np.cos(ang)), np.kron(eye, np.sin(ang))], axis=1) * scale


def _position_dft_tables(n, tr):
    seed_rows = 8
    col = jnp.arange(n, dtype=jnp.int32)[None, :]
    angle = lambda k: ((jnp.asarray(k, jnp.int32)[:, None] * col) % n).astype(F32) * (2.0 * np.pi / n)
    seed = angle(np.arange(seed_rows))
    k0 = [seed_rows << j for j in range((tr // seed_rows).bit_length() - 1)] + list(range(0, n, tr))
    rot = angle(np.asarray(k0))
    scale = n ** -0.5
    return jnp.cos(seed) * scale, jnp.sin(seed) * -scale, jnp.cos(rot), jnp.sin(rot)


def kernel(x, c, ctx, c_ctx, w_ada, b_ada, norm1_g, ffn1_w_gate, ffn1_w_up, ffn1_w_down, norm_mix_g, w_in,
           lambda_q1, lambda_k1, lambda_q2, lambda_k2, subln_g, w_fourier, w_out, norm2_g, ffn2_w_gate,
           ffn2_w_up, ffn2_w_down, final_norm_g):
    b, n, d = x.shape
    n_ctx = ctx.shape[1]
    assert (d, w_ada.shape[0]) == (D_MODEL, 1) and b + 1 <= MOD_ROWS
    tm = 512
    tiles_per_seq = n // tm
    mod_row = lambda i: jnp.minimum(i // tiles_per_seq, b)
    row = lambda g: g.reshape(1, -1)

    cc = jnp.zeros((MOD_ROWS, d), F32).at[:b].set(c).at[b].set(c_ctx)
    mod3 = _ada(cc, w_ada[0], b_ada).reshape(MOD_ROWS, 1, N_MOD * d)

    x1, c1 = _ffn(x.reshape(b * n, d), mod3, mod_row, row(norm1_g), ffn1_w_gate[0], ffn1_w_up[0], ffn1_w_down[0],
                  tm=tm, mod_base=0, tail=ctx.reshape(b * n_ctx, d))

    chan_dft = jnp.asarray(_channel_dft_table(), dtype=F32).astype(BF16)
    q, k, v, yc, ys, kc, vc = _inproj(x1, c1, mod3, mod_row, row(norm_mix_g), w_in[0], _rope_tables(n), chan_dft, tm=tm)

    lamvec = jnp.concatenate([lambda_q1, lambda_k1, lambda_q2, lambda_k2], axis=0)
    seq = lambda a, rows: a.reshape(b, rows, a.shape[-1])
    att = _attention(lamvec, row(subln_g), seq(q, n), seq(k, n), seq(kc, n_ctx), seq(v, n), seq(vc, n_ctx),
                     tq=1024, tk=1024)

    tr = 256
    zf = _position_dft(_position_dft_tables(n, tr), yc, ys, w_fourier[0], tr=tr)

    out = _ffn(x1, mod3, mod_row, row(norm2_g), ffn2_w_gate[0], ffn2_w_up[0], ffn2_w_down[0],
               tm=tm, mod_base=6, mix=(att.reshape(b * n, D_DIFF), zf, w_out[0]), tiles_per_seq=tiles_per_seq,
               final_g=row(final_norm_g))
    return out.reshape(b, n, d)
```

```python
import functools
import math

import numpy as np
import jax
import jax.numpy as jnp
from jax import lax
from jax.experimental import pallas as pl
from jax.experimental.pallas import tpu as pltpu

D_MODEL = 1024
GRID_W = 64
D_FOURIER = 256
D_DIFF = 768
HEAD_DIM = 64
HEAD_W = 2 * HEAD_DIM
N_HEADS = D_DIFF // HEAD_W
FOURIER_GROUP_DIM = 64
D_IN_PROJ = 3 * D_DIFF + D_FOURIER
D_FF = 2816
N_MOD = 9
ROPE_BASE = 10000.0
ROPE_PAIRS = HEAD_DIM // 4
RMS_EPS = 1e-6
ATTN_SCALE = HEAD_DIM ** -0.5
Q_SCALE = ATTN_SCALE * math.log2(math.e)
LAMBDA_INIT = 0.8 - 0.6 * math.exp(-0.3 * 0)
UNDERFLOW_GUARD = 2.0 ** -80

F32 = jnp.float32
BF16 = jnp.bfloat16

V7X_VMEM_LIMIT_BYTES = 56 * 1024 * 1024
BF16_SUBLANES = 16
STAGE_CHUNK_BYTES = 3 * 512 * 1024
MOD_ROWS = 8
FF_CHUNKS = ((0, 1536), (1536, 1280))


def _const_spec(shape):
    return pl.BlockSpec(shape, lambda *_: (0,) * len(shape), pipeline_mode=pl.Buffered(1))


def _params(semantics):
    return pltpu.CompilerParams(dimension_semantics=semantics, vmem_limit_bytes=V7X_VMEM_LIMIT_BYTES)


def _rmsnorm(x, g):
    return x * lax.rsqrt(jnp.mean(x * x, axis=-1, keepdims=True) + RMS_EPS) * g


def _mod_row(mod_ref, k):
    return mod_ref[0, :, k * D_MODEL:(k + 1) * D_MODEL]


def _ada_kernel(cc_ref, w_ref, b_ref, o_ref):
    cc = cc_ref[...]
    s = cc * jax.nn.sigmoid(cc)
    o_ref[...] = jnp.dot(s.astype(BF16), w_ref[...].astype(BF16), preferred_element_type=F32) + b_ref[...]


def _ada(cc, w_ada, b_ada):
    n_out = w_ada.shape[1]
    tn = D_MODEL
    return pl.pallas_call(
        _ada_kernel,
        grid=(n_out // tn,),
        in_specs=[_const_spec((MOD_ROWS, D_MODEL)),
                  pl.BlockSpec((D_MODEL, tn), lambda j: (0, j)),
                  pl.BlockSpec((1, tn), lambda j: (0, j))],
        out_specs=pl.BlockSpec((MOD_ROWS, tn), lambda j: (0, j)),
        out_shape=jax.ShapeDtypeStruct((MOD_ROWS, n_out), F32),
        compiler_params=_params(("arbitrary",)),
        name="ada",
    )(cc, w_ada, b_ada)


def _stage_rows(k, n):
    fits = [r for r in range(BF16_SUBLANES, k + 1, BF16_SUBLANES) if k % r == 0 and r * n * 4 <= STAGE_CHUNK_BYTES]
    return max(fits)


def _stage_weight(src_hbm, dst_ref, stage_ref, sem_ref):
    rows = stage_ref.shape[1]
    n_chunks = src_hbm.shape[0] // rows

    def copy(c):
        return pltpu.make_async_copy(src_hbm.at[pl.ds(c * rows, rows)], stage_ref.at[c % 2], sem_ref.at[c % 2])

    copy(0).start()
    for c in range(n_chunks):
        if c + 1 < n_chunks:
            copy(c + 1).start()
        copy(c).wait()
        dst_ref[c * rows:(c + 1) * rows, :] = stage_ref[c % 2].astype(BF16)


class _StagedWeights:
    def __init__(self, weights):
        self.weights = list(weights)
        self.in_specs = [pl.BlockSpec(memory_space=pl.ANY)] * len(self.weights)
        chunk_shapes = sorted({(_stage_rows(*w.shape), w.shape[1]) for w in self.weights})
        self.stage_of = [chunk_shapes.index((_stage_rows(*w.shape), w.shape[1])) for w in self.weights]
        self.scratch = ([pltpu.VMEM(w.shape, BF16) for w in self.weights]
                        + [pltpu.VMEM((2,) + s, F32) for s in chunk_shapes]
                        + [pltpu.SemaphoreType.DMA((2,))])

    def load(self, hbm_refs, scratch_refs):
        n = len(self.weights)
        dst, stages, sem = scratch_refs[:n], scratch_refs[n:-1], scratch_refs[-1]

        @pl.when(pl.program_id(0) == 0)
        def _():
            for src, d, s in zip(hbm_refs, dst, self.stage_of):
                _stage_weight(src, d, stages[s], sem)

        return dst


def _ffn_kernel(*refs, staged, mod_base, has_mix, final_norm, n_main):
    n_w = len(staged.weights)
    n_scratch = len(staged.scratch)
    x_ref, mod_ref, g_ref = refs[:3]
    w_hbm = refs[3:3 + n_w]
    rest = list(refs[3 + n_w:len(refs) - n_scratch])
    weights = staged.load(w_hbm, refs[len(refs) - n_scratch:])
    wg_ref, wu_ref, wd_ref = weights[:3]
    x = x_ref[...]
    has_tail = n_main is not None
    if has_tail:
        tail_ref = rest[0]
        tail_o_ref = rest[-1]
        rest = rest[1:-1]
        in_tail = pl.program_id(0) >= n_main
        x = jnp.where(in_tail, tail_ref[...], x)
    if has_mix:
        att_ref, zf_ref = rest[:2]
        rest = rest[2:]
        wo_ref = weights[3]
        mix = jnp.dot(att_ref[...], wo_ref[:D_DIFF, :], preferred_element_type=F32)
        mix += jnp.dot(zf_ref[...], wo_ref[D_DIFF:, :], preferred_element_type=F32)
        x = x + _mod_row(mod_ref, 5) * mix
    if final_norm:
        gf_ref = rest[0]
        rest = rest[1:]
    (o_ref,) = rest

    h = _rmsnorm(x, g_ref[...]) * (1.0 + _mod_row(mod_ref, mod_base + 1)) + _mod_row(mod_ref, mod_base)
    h = h.astype(BF16)
    acc = None
    for start, width in FF_CHUNKS:
        gate = jnp.dot(h, wg_ref[:, start:start + width], preferred_element_type=F32)
        up = jnp.dot(h, wu_ref[:, start:start + width], preferred_element_type=F32)
        a = (gate * jax.nn.sigmoid(gate) * up).astype(BF16)
        part = jnp.dot(a, wd_ref[start:start + width, :], preferred_element_type=F32)
        acc = part if acc is None else acc + part
    x = x + (0.5 * _mod_row(mod_ref, mod_base + 2)) * acc
    if final_norm:
        x = _rmsnorm(x, gf_ref[...])
    if has_tail:
        @pl.when(in_tail)
        def _():
            tail_o_ref[...] = x

        @pl.when(jnp.logical_not(in_tail))
        def _():
            o_ref[...] = x
    else:
        o_ref[...] = x


def _ffn(x, mod3, mod_index, g, wg, wu, wd, *, tm, mod_base, tail=None, mix=None, tiles_per_seq=None, final_g=None):
    t = x.shape[0]
    n_main = t // tm
    n_tiles = n_main + (tail.shape[0] // tm if tail is not None else 0)
    tile = lambda w: pl.BlockSpec((tm, w), lambda i: (jnp.minimum(i, n_main - 1), 0))
    tail_tile = pl.BlockSpec((tm, D_MODEL), lambda i: (jnp.maximum(i - n_main, 0), 0))
    staged = _StagedWeights([wg, wu, wd] + ([mix[2]] if mix is not None else []))
    in_specs = [tile(D_MODEL),
                pl.BlockSpec((1, 1, N_MOD * D_MODEL), lambda i: (mod_index(i), 0, 0)),
                _const_spec((1, D_MODEL))] + staged.in_specs
    args = [x, mod3, g] + staged.weights
    out_specs, out_shape = tile(D_MODEL), jax.ShapeDtypeStruct((t, D_MODEL), F32)
    if tail is not None:
        in_specs.append(tail_tile)
        args.append(tail)
        out_specs, out_shape = [out_specs, tail_tile], [out_shape, jax.ShapeDtypeStruct(tail.shape, F32)]
    if mix is not None:
        att, zf, _ = mix
        by_batch = pl.BlockSpec((tm, D_FOURIER), lambda i: (i % tiles_per_seq, i // tiles_per_seq))
        in_specs += [tile(D_DIFF), by_batch]
        args += [att, zf]
    if final_g is not None:
        in_specs.append(_const_spec((1, D_MODEL)))
        args.append(final_g)
    kern = functools.partial(_ffn_kernel, staged=staged, mod_base=mod_base, has_mix=mix is not None,
                             final_norm=final_g is not None, n_main=n_main if tail is not None else None)
    return pl.pallas_call(
        kern,
        grid=(n_tiles,),
        in_specs=in_specs,
        out_specs=out_specs,
        out_shape=out_shape,
        scratch_shapes=staged.scratch,
        compiler_params=_params(("arbitrary",)),
        name="ffn_mix" if mix is not None else "ffn",
    )(*args)


def _rope(x, cos, sin_lo, sin_hi):
    return (x * cos + pltpu.roll(x, HEAD_W - ROPE_PAIRS, 1) * sin_lo
            + pltpu.roll(x, ROPE_PAIRS, 1) * sin_hi)


def _inproj_kernel(*refs, staged, latent):
    n_scratch = len(staged.scratch)
    x_ref, mod_ref, g_ref = refs[:3]
    (w_ref,) = staged.load(refs[3:4], refs[len(refs) - n_scratch:])
    refs = refs[:len(refs) - n_scratch]
    h = _rmsnorm(x_ref[...], g_ref[...]) * (1.0 + _mod_row(mod_ref, 4)) + _mod_row(mod_ref, 3)
    p = jnp.dot(h.astype(BF16), w_ref[...], preferred_element_type=F32)
    if latent:
        cos_ref, slo_ref, shi_ref, dft_ref, q_ref, k_ref, v_ref, yc_ref, ys_ref = refs[4:]
        cos, slo, shi = cos_ref[...], slo_ref[...], shi_ref[...]
        for hd in range(N_HEADS):
            lo = hd * HEAD_W
            q_ref[:, lo:lo + HEAD_W] = (_rope(p[:, lo:lo + HEAD_W], cos, slo, shi) * Q_SCALE).astype(BF16)
            k_ref[:, lo:lo + HEAD_W] = _rope(p[:, D_DIFF + lo:D_DIFF + lo + HEAD_W], cos, slo, shi).astype(BF16)
        y = jnp.dot(p[:, 3 * D_DIFF:].astype(BF16), dft_ref[...], preferred_element_type=F32)
        yc_ref[...] = y[:, :D_FOURIER].astype(BF16)
        ys_ref[...] = y[:, D_FOURIER:].astype(BF16)
    else:
        k_ref, v_ref = refs[4:]
        k_ref[...] = p[:, D_DIFF:2 * D_DIFF].astype(BF16)
    v_ref[...] = p[:, 2 * D_DIFF:3 * D_DIFF].astype(BF16)


def _inproj(x, mod3, mod_index, g, w_in, *, tm, rope=None, chan_dft=None):
    t = x.shape[0]
    latent = rope is not None
    tile = lambda w: pl.BlockSpec((tm, w), lambda i: (i, 0))
    staged = _StagedWeights([w_in])
    in_specs = [tile(D_MODEL),
                pl.BlockSpec((1, 1, N_MOD * D_MODEL), lambda i: (mod_index(i), 0, 0)),
                _const_spec((1, D_MODEL))] + staged.in_specs
    args = [x, mod3, g, w_in]
    wide = jax.ShapeDtypeStruct((t, D_DIFF), BF16)
    if latent:
        tiles_per_seq = rope[0].shape[0] // tm
        rope_spec = pl.BlockSpec((tm, HEAD_W), lambda i: (i % tiles_per_seq, 0))
        in_specs += [rope_spec, rope_spec, rope_spec, _const_spec((D_FOURIER, 2 * D_FOURIER))]
        args += [*rope, chan_dft]
        narrow = jax.ShapeDtypeStruct((tiles_per_seq * tm, t // (tiles_per_seq * tm) * D_FOURIER), BF16)
        by_batch = pl.BlockSpec((tm, D_FOURIER), lambda i: (i % tiles_per_seq, i // tiles_per_seq))
        out_shape = [wide, wide, wide, narrow, narrow]
        out_specs = [tile(D_DIFF)] * 3 + [by_batch] * 2
    else:
        out_shape = [wide, wide]
        out_specs = [tile(D_DIFF)] * 2
    return pl.pallas_call(
        functools.partial(_inproj_kernel, staged=staged, latent=latent),
        grid=(t // tm,),
        in_specs=in_specs,
        out_specs=out_specs,
        out_shape=out_shape,
        scratch_shapes=staged.scratch,
        compiler_params=_params(("arbitrary",)),
        name="inproj" if latent else "inproj_ctx",
    )(*args)


def _max_key_norm_sq(kk):
    sq = (kk.astype(F32) ** 2).astype(BF16)
    r = lax.broadcasted_iota(jnp.int32, (HEAD_W, HEAD_W), 0)
    c = lax.broadcasted_iota(jnp.int32, (HEAD_W, HEAD_W), 1)
    sel = jnp.where((c == 0) == (r < HEAD_DIM), 1.0, 0.0) * jnp.where(c < 2, 1.0, 0.0)
    sums = jnp.dot(sq, sel.astype(BF16), preferred_element_type=F32)
    biggest = jnp.max(sums, axis=0, keepdims=True) * (1.0 + 2.0 ** -7)
    return biggest[:, 0:1], biggest[:, 1:2]


def _attn_kernel(lam_ref, g_ref, q_ref, k_ref, kc_ref, v_ref, vc_ref, o_ref, vt_ref, vct_ref, kn_ref, *, tk):
    tq = q_ref.shape[1]

    @pl.when(pl.program_id(2) == 0)
    def _():
        vt_ref[...] = v_ref[0].T
        vct_ref[...] = vc_ref[0].T
        (a1, a2), (b1, b2) = _max_key_norm_sq(k_ref[0]), _max_key_norm_sq(kc_ref[0])
        kn_ref[...] = jnp.sqrt(jnp.concatenate([jnp.broadcast_to(jnp.maximum(a1, b1), (1, tq)),
                                                 jnp.broadcast_to(jnp.maximum(a2, b2), (1, tq))], axis=1))

    lv = lam_ref[...]
    lam = (jnp.exp(jnp.sum(lv[0:1] * lv[1:2], axis=1, keepdims=True))
           - jnp.exp(jnp.sum(lv[2:3] * lv[3:4], axis=1, keepdims=True)) + LAMBDA_INIT)

    qt = q_ref[0].astype(F32).T
    top = lax.broadcasted_iota(jnp.int32, qt.shape, 0) < HEAD_DIM
    q1t, q2t = jnp.where(top, qt, 0.0), jnp.where(top, 0.0, qt)
    qcat = jnp.concatenate([q1t, q2t], axis=1).astype(BF16)
    chunks = [(k_ref[0, c * tk:(c + 1) * tk, :], vt_ref[:, c * tk:(c + 1) * tk]) for c in range(k_ref.shape[1] // tk)]
    chunks.append((kc_ref[0], vct_ref[...]))

    def scores(kk):
        return jnp.dot(kk, qcat, preferred_element_type=F32)

    def softmax_sums(shift):
        lsum, acc = 0.0, 0.0
        for kk, vt in chunks:
            e = jnp.exp2(scores(kk) - shift)
            lsum += jnp.sum(e.reshape(-1, 8, 2 * tq), axis=0)
            acc += jnp.dot(vt, e.astype(BF16), preferred_element_type=F32)
        return jnp.sum(lsum, axis=0, keepdims=True), acc

    def finish(l, acc):
        ot = acc[:, :tq] * (1.0 / l[:, :tq]) - acc[:, tq:] * (lam / l[:, tq:])
        o_ref[0] = (_rmsnorm(ot.T, g_ref[...]) * (1.0 - LAMBDA_INIT)).astype(BF16)

    qn = jnp.sqrt(jnp.concatenate([jnp.sum(q1t * q1t, axis=0, keepdims=True),
                                   jnp.sum(q2t * q2t, axis=0, keepdims=True)], axis=1))
    l, acc = softmax_sums(qn * kn_ref[...])
    finish(l, acc)

    @pl.when(jnp.min(l) < UNDERFLOW_GUARD)
    def _():
        m = None
        for kk, _ in chunks:
            cm = jnp.max(scores(kk), axis=0, keepdims=True)
            m = cm if m is None else jnp.maximum(m, cm)
        finish(*softmax_sums(m))


def _attention(lamvec, subln_g, q, k, kc, v, vc, *, tq, tk):
    b, n, _ = q.shape
    c = kc.shape[1]
    head_block = lambda rows: pl.BlockSpec((1, rows, HEAD_W), lambda bi, hi, qi: (bi, 0, hi))
    return pl.pallas_call(
        functools.partial(_attn_kernel, tk=tk),
        grid=(b, N_HEADS, n // tq),
        in_specs=[_const_spec(lamvec.shape), _const_spec((1, HEAD_W)),
                  pl.BlockSpec((1, tq, HEAD_W), lambda bi, hi, qi: (bi, qi, hi)),
                  head_block(n), head_block(c), head_block(n), head_block(c)],
        out_specs=pl.BlockSpec((1, tq, HEAD_W), lambda bi, hi, qi: (bi, qi, hi)),
        out_shape=jax.ShapeDtypeStruct((b, n, D_DIFF), BF16),
        scratch_shapes=[pltpu.VMEM((HEAD_W, n), BF16), pltpu.VMEM((HEAD_W, c), BF16),
                        pltpu.VMEM((1, 2 * tq), F32)],
        compiler_params=_params(("parallel", "parallel", "arbitrary")),
        name="diff_attn",
    )(lamvec, subln_g, q, k, kc, v, vc)


def _rotate(c, m, rc, rs):
    return c * rc + m * rs, m * rc - c * rs


def _dft_kernel(seed_c_ref, seed_m_ref, rot_c_ref, rot_s_ref, yc_ref, ys_ref, wf_ref, o_ref, cb_ref, mb_ref):
    tr = cb_ref.shape[0]
    seed_rows = seed_c_ref.shape[0]
    n_doublings = (tr // seed_rows).bit_length() - 1
    i = pl.program_id(0)

    @pl.when(i == 0)
    def _():
        cb_ref[0:seed_rows] = seed_c_ref[...]
        mb_ref[0:seed_rows] = seed_m_ref[...]
        for j in range(n_doublings):
            rows = seed_rows << j
            cb_ref[rows:2 * rows], mb_ref[rows:2 * rows] = _rotate(
                cb_ref[0:rows], mb_ref[0:rows], rot_c_ref[j:j + 1], rot_s_ref[j:j + 1])

    ct, mt = _rotate(cb_ref[...], mb_ref[...], rot_c_ref[pl.ds(n_doublings + i, 1)], rot_s_ref[pl.ds(n_doublings + i, 1)])
    z = jnp.dot(ct.astype(BF16), yc_ref[...], preferred_element_type=F32)
    z += jnp.dot(mt.astype(BF16), ys_ref[...], preferred_element_type=F32)
    w = wf_ref.shape[0]
    wf = wf_ref[...].astype(BF16)
    for b in range(z.shape[1] // w):
        o_ref[:, b * w:(b + 1) * w] = jnp.dot(z[:, b * w:(b + 1) * w].astype(BF16), wf,
                                             preferred_element_type=F32).astype(BF16)


def _position_dft(tables, yc, ys, wf, *, tr):
    n, cols = yc.shape
    seed_c, seed_m, rot_c, rot_s = tables
    return pl.pallas_call(
        _dft_kernel,
        grid=(n // tr,),
        in_specs=[_const_spec(seed_c.shape), _const_spec(seed_m.shape), _const_spec(rot_c.shape),
                  _const_spec(rot_s.shape), _const_spec((n, cols)), _const_spec((n, cols)), _const_spec(wf.shape)],
        out_specs=pl.BlockSpec((tr, cols), lambda i: (i, 0)),
        out_shape=jax.ShapeDtypeStruct((n, cols), BF16),
        scratch_shapes=[pltpu.VMEM((tr, n), F32), pltpu.VMEM((tr, n), F32)],
        compiler_params=_params(("arbitrary",)),
        name="position_dft",
    )(seed_c, seed_m, rot_c, rot_s, yc, ys, wf)


def _rope_tables(n):
    rows = n // GRID_W
    lane = np.arange(HEAD_W)
    sub = lane % HEAD_DIM
    on_row_axis = jnp.asarray(sub // (2 * ROPE_PAIRS) == 0)[None, :]
    second_half = jnp.asarray((sub % (2 * ROPE_PAIRS)) // ROPE_PAIRS == 1)[None, :]
    inv_freq = ROPE_BASE ** (-jnp.asarray(sub % ROPE_PAIRS, dtype=F32) / ROPE_PAIRS)

    def table(fn, keep):
        def side(size, mine):
            t = fn(jnp.arange(size, dtype=F32)[:, None] * inv_freq[None, :])
            return jnp.where(mine & keep, t, 0.0)
        return (side(rows, on_row_axis)[:, None, :] + side(GRID_W, ~on_row_axis)[None, :, :]).reshape(n, HEAD_W)

    return (table(jnp.cos, True), table(lambda a: -jnp.sin(a), ~second_half), table(jnp.sin, second_half))


def _channel_dft_table():
    c = np.arange(FOURIER_GROUP_DIM)
    ang = 2.0 * np.pi * np.outer(c, c) / FOURIER_GROUP_DIM
    eye = np.eye(D_FOURIER // FOURIER_GROUP_DIM)
    scale = FOURIER_GROUP_DIM ** -0.5
    return np.concatenate([np.kron(eye, np.cos(ang)), np.kron(eye, np.sin(ang))], axis=1) * scale


def _position_dft_tables(n, tr):
    seed_rows = 8
    col = jnp.arange(n, dtype=jnp.int32)[None, :]
    angle = lambda k: ((jnp.asarray(k, jnp.int32)[:, None] * col) % n).astype(F32) * (2.0 * np.pi / n)
    seed = angle(np.arange(seed_rows))
    k0 = [seed_rows << j for j in range((tr // seed_rows).bit_length() - 1)] + list(range(0, n, tr))
    rot = angle(np.asarray(k0))
    scale = n ** -0.5
    return jnp.cos(seed) * scale, jnp.sin(seed) * -scale, jnp.cos(rot), jnp.sin(rot)


def kernel(x, c, ctx, c_ctx, w_ada, b_ada, norm1_g, ffn1_w_gate, ffn1_w_up, ffn1_w_down, norm_mix_g, w_in,
           lambda_q1, lambda_k1, lambda_q2, lambda_k2, subln_g, w_fourier, w_out, norm2_g, ffn2_w_gate,
           ffn2_w_up, ffn2_w_down, final_norm_g):
    b, n, d = x.shape
    n_ctx = ctx.shape[1]
    assert (d, w_ada.shape[0]) == (D_MODEL, 1) and b + 1 <= MOD_ROWS
    tm = 512
    tiles_per_seq = n // tm
    mod_row = lambda i: jnp.minimum(i // tiles_per_seq, b)
    row = lambda g: g.reshape(1, -1)

    cc = jnp.zeros((MOD_ROWS, d), F32).at[:b].set(c).at[b].set(c_ctx)
    mod3 = _ada(cc, w_ada[0], b_ada).reshape(MOD_ROWS, 1, N_MOD * d)

    x1, c1 = _ffn(x.reshape(b * n, d), mod3, mod_row, row(norm1_g), ffn1_w_gate[0], ffn1_w_up[0], ffn1_w_down[0],
                  tm=tm, mod_base=0, tail=ctx.reshape(b * n_ctx, d))

    chan_dft = jnp.asarray(_channel_dft_table(), dtype=F32).astype(BF16)
    q, k, v, yc, ys = _inproj(x1, mod3, mod_row, row(norm_mix_g), w_in[0], tm=tm,
                              rope=_rope_tables(n), chan_dft=chan_dft)
    kc, vc = _inproj(c1, mod3, lambda i: b, row(norm_mix_g), w_in[0], tm=tm)

    lamvec = jnp.concatenate([lambda_q1, lambda_k1, lambda_q2, lambda_k2], axis=0)
    seq = lambda a, rows: a.reshape(b, rows, a.shape[-1])
    att = _attention(lamvec, row(subln_g), seq(q, n), seq(k, n), seq(kc, n_ctx), seq(v, n), seq(vc, n_ctx),
                     tq=1024, tk=1024)

    tr = 512
    zf = _position_dft(_position_dft_tables(n, tr), yc, ys, w_fourier[0], tr=tr)

    out = _ffn(x1, mod3, mod_row, row(norm2_g), ffn2_w_gate[0], ffn2_w_up[0], ffn2_w_down[0],
               tm=tm, mod_base=6, mix=(att.reshape(b * n, D_DIFF), zf, w_out[0]), tiles_per_seq=tiles_per_seq,
               final_g=row(final_norm_g))
    return out.reshape(b, n, d)
```

```python
import functools
import math

import numpy as np
import jax
import jax.numpy as jnp
from jax import lax
from jax.experimental import pallas as pl
from jax.experimental.pallas import tpu as pltpu

D_MODEL = 1024
GRID_W = 64
D_FOURIER = 256
D_DIFF = 768
HEAD_DIM = 64
HEAD_W = 2 * HEAD_DIM
N_HEADS = D_DIFF // HEAD_W
FOURIER_GROUP_DIM = 64
D_IN_PROJ = 3 * D_DIFF + D_FOURIER
D_FF = 2816
N_MOD = 9
ROPE_BASE = 10000.0
ROPE_PAIRS = HEAD_DIM // 4
RMS_EPS = 1e-6
ATTN_SCALE = HEAD_DIM ** -0.5
Q_SCALE = ATTN_SCALE * math.log2(math.e)
LAMBDA_INIT = 0.8 - 0.6 * math.exp(-0.3 * 0)
UNDERFLOW_GUARD = 2.0 ** -80

F32 = jnp.float32
BF16 = jnp.bfloat16

V7X_VMEM_LIMIT_BYTES = 56 * 1024 * 1024
BF16_SUBLANES = 16
STAGE_CHUNK_BYTES = 3 * 512 * 1024
MOD_ROWS = 8
DFT_N1 = 16
FF_CHUNKS = ((0, 1536), (1536, 1280))


def _const_spec(shape):
    return pl.BlockSpec(shape, lambda *_: (0,) * len(shape), pipeline_mode=pl.Buffered(1))


def _params(semantics):
    return pltpu.CompilerParams(dimension_semantics=semantics, vmem_limit_bytes=V7X_VMEM_LIMIT_BYTES)


def _rmsnorm(x, g):
    return x * lax.rsqrt(jnp.mean(x * x, axis=-1, keepdims=True) + RMS_EPS) * g


def _mod_row(mod_ref, k):
    return mod_ref[0, :, k * D_MODEL:(k + 1) * D_MODEL]


def _ada_kernel(cc_ref, w_ref, b_ref, o_ref):
    cc = cc_ref[...]
    s = cc * jax.nn.sigmoid(cc)
    o_ref[...] = jnp.dot(s.astype(BF16), w_ref[...].astype(BF16), preferred_element_type=F32) + b_ref[...]


def _ada(cc, w_ada, b_ada):
    n_out = w_ada.shape[1]
    tn = D_MODEL
    return pl.pallas_call(
        _ada_kernel,
        grid=(n_out // tn,),
        in_specs=[_const_spec((MOD_ROWS, D_MODEL)),
                  pl.BlockSpec((D_MODEL, tn), lambda j: (0, j)),
                  pl.BlockSpec((1, tn), lambda j: (0, j))],
        out_specs=pl.BlockSpec((MOD_ROWS, tn), lambda j: (0, j)),
        out_shape=jax.ShapeDtypeStruct((MOD_ROWS, n_out), F32),
        compiler_params=_params(("arbitrary",)),
        name="ada",
    )(cc, w_ada, b_ada)


def _stage_rows(k, n):
    fits = [r for r in range(BF16_SUBLANES, k + 1, BF16_SUBLANES) if k % r == 0 and r * n * 4 <= STAGE_CHUNK_BYTES]
    return max(fits)


def _stage_weight(src_hbm, dst_ref, stage_ref, sem_ref):
    rows = stage_ref.shape[1]
    n_chunks = src_hbm.shape[0] // rows

    def copy(c):
        return pltpu.make_async_copy(src_hbm.at[pl.ds(c * rows, rows)], stage_ref.at[c % 2], sem_ref.at[c % 2])

    copy(0).start()
    for c in range(n_chunks):
        if c + 1 < n_chunks:
            copy(c + 1).start()
        copy(c).wait()
        dst_ref[c * rows:(c + 1) * rows, :] = stage_ref[c % 2].astype(BF16)


class _StagedWeights:
    def __init__(self, weights):
        self.weights = list(weights)
        self.in_specs = [pl.BlockSpec(memory_space=pl.ANY)] * len(self.weights)
        chunk_shapes = sorted({(_stage_rows(*w.shape), w.shape[1]) for w in self.weights})
        self.stage_of = [chunk_shapes.index((_stage_rows(*w.shape), w.shape[1])) for w in self.weights]
        self.scratch = ([pltpu.VMEM(w.shape, BF16) for w in self.weights]
                        + [pltpu.VMEM((2,) + s, F32) for s in chunk_shapes]
                        + [pltpu.SemaphoreType.DMA((2,))])

    def load(self, hbm_refs, scratch_refs):
        n = len(self.weights)
        dst, stages, sem = scratch_refs[:n], scratch_refs[n:-1], scratch_refs[-1]

        @pl.when(pl.program_id(0) == 0)
        def _():
            for src, d, s in zip(hbm_refs, dst, self.stage_of):
                _stage_weight(src, d, stages[s], sem)

        return dst


def _ffn_kernel(*refs, staged, mod_base, has_mix, final_norm, n_main):
    n_w = len(staged.weights)
    n_scratch = len(staged.scratch)
    x_ref, mod_ref, g_ref = refs[:3]
    w_hbm = refs[3:3 + n_w]
    rest = list(refs[3 + n_w:len(refs) - n_scratch])
    weights = staged.load(w_hbm, refs[len(refs) - n_scratch:])
    wg_ref, wu_ref, wd_ref = weights[:3]
    has_tail = n_main is not None
    if has_tail:
        tail_ref = rest[0]
        tail_o_ref = rest[-1]
        rest = rest[1:-1]
        in_tail = pl.program_id(0) >= n_main
    if has_mix:
        att_ref, zf_ref, perm_ref = rest[:3]
        rest = rest[3:]
        wo_ref = weights[3]
    if final_norm:
        gf_ref = rest[0]
        rest = rest[1:]
    (o_ref,) = rest

    x = x_ref[...]
    if has_tail:
        x = jnp.where(in_tail, tail_ref[...], x)
    if has_mix:
        zf = jnp.dot(perm_ref[...], zf_ref[...].reshape(x.shape[0], D_FOURIER), preferred_element_type=F32)
        mix = jnp.dot(att_ref[...], wo_ref[:D_DIFF, :], preferred_element_type=F32)
        mix += jnp.dot(zf.astype(BF16), wo_ref[D_DIFF:, :], preferred_element_type=F32)
        x = x + _mod_row(mod_ref, 5) * mix
    h = _rmsnorm(x, g_ref[...]) * (1.0 + _mod_row(mod_ref, mod_base + 1)) + _mod_row(mod_ref, mod_base)
    h = h.astype(BF16)
    acc = None
    for start, width in FF_CHUNKS:
        gate = jnp.dot(h, wg_ref[:, start:start + width], preferred_element_type=F32)
        up = jnp.dot(h, wu_ref[:, start:start + width], preferred_element_type=F32)
        a = (gate * jax.nn.sigmoid(gate) * up).astype(BF16)
        part = jnp.dot(a, wd_ref[start:start + width, :], preferred_element_type=F32)
        acc = part if acc is None else acc + part
    x = x + (0.5 * _mod_row(mod_ref, mod_base + 2)) * acc
    if final_norm:
        x = _rmsnorm(x, gf_ref[...])
    if has_tail:
        @pl.when(in_tail)
        def _():
            tail_o_ref[...] = x

        @pl.when(jnp.logical_not(in_tail))
        def _():
            o_ref[...] = x
    else:
        o_ref[...] = x


def _ffn(x, mod3, mod_index, g, wg, wu, wd, *, tm, mod_base, tail=None, mix=None, tiles_per_seq=None, final_g=None):
    t = x.shape[0]
    n_main = t // tm
    n_tiles = n_main + (tail.shape[0] // tm if tail is not None else 0)
    tile = lambda w: pl.BlockSpec((tm, w), lambda i: (jnp.minimum(i, n_main - 1), 0))
    tail_tile = pl.BlockSpec((tm, D_MODEL), lambda i: (jnp.maximum(i - n_main, 0), 0))
    staged = _StagedWeights([wg, wu, wd] + ([mix[2]] if mix is not None else []))
    in_specs = [tile(D_MODEL),
                pl.BlockSpec((1, 1, N_MOD * D_MODEL), lambda i: (mod_index(i), 0, 0)),
                _const_spec((1, D_MODEL))] + staged.in_specs
    args = [x, mod3, g] + staged.weights
    out_specs, out_shape = tile(D_MODEL), jax.ShapeDtypeStruct((t, D_MODEL), F32)
    if tail is not None:
        in_specs.append(tail_tile)
        args.append(tail)
        out_specs, out_shape = [out_specs, tail_tile], [out_shape, jax.ShapeDtypeStruct(tail.shape, F32)]
    if mix is not None:
        att, zf, _ = mix
        n, cols = zf.shape
        d_per_tile = tm // DFT_N1
        by_batch = pl.BlockSpec((DFT_N1, d_per_tile, D_FOURIER), lambda i: (0, i % tiles_per_seq, i // tiles_per_seq))
        in_specs += [tile(D_DIFF), by_batch, _const_spec((tm, tm))]
        args += [att, zf.reshape(DFT_N1, n // DFT_N1, cols), _row_permutation(DFT_N1, tm)]
    if final_g is not None:
        in_specs.append(_const_spec((1, D_MODEL)))
        args.append(final_g)
    kern = functools.partial(_ffn_kernel, staged=staged, mod_base=mod_base, has_mix=mix is not None,
                             final_norm=final_g is not None, n_main=n_main if tail is not None else None)
    return pl.pallas_call(
        kern,
        grid=(n_tiles,),
        in_specs=in_specs,
        out_specs=out_specs,
        out_shape=out_shape,
        scratch_shapes=staged.scratch,
        compiler_params=_params(("arbitrary",)),
        name="ffn_mix" if mix is not None else "ffn",
    )(*args)


def _rope(x, cos, sin_lo, sin_hi):
    return (x * cos + pltpu.roll(x, HEAD_W - ROPE_PAIRS, 1) * sin_lo
            + pltpu.roll(x, ROPE_PAIRS, 1) * sin_hi)


def _inproj_kernel(*refs, staged, latent):
    n_scratch = len(staged.scratch)
    x_ref, mod_ref, g_ref = refs[:3]
    (w_ref,) = staged.load(refs[3:4], refs[len(refs) - n_scratch:])
    refs = refs[:len(refs) - n_scratch]
    h = _rmsnorm(x_ref[...], g_ref[...]) * (1.0 + _mod_row(mod_ref, 4)) + _mod_row(mod_ref, 3)
    p = jnp.dot(h.astype(BF16), w_ref[...], preferred_element_type=F32)
    if latent:
        cos_ref, slo_ref, shi_ref, dft_ref, q_ref, k_ref, v_ref, yc_ref, ys_ref = refs[4:]
        cos, slo, shi = cos_ref[...], slo_ref[...], shi_ref[...]
        for hd in range(N_HEADS):
            lo = hd * HEAD_W
            q_ref[:, lo:lo + HEAD_W] = (_rope(p[:, lo:lo + HEAD_W], cos, slo, shi) * Q_SCALE).astype(BF16)
            k_ref[:, lo:lo + HEAD_W] = _rope(p[:, D_DIFF + lo:D_DIFF + lo + HEAD_W], cos, slo, shi).astype(BF16)
        y = jnp.dot(p[:, 3 * D_DIFF:].astype(BF16), dft_ref[...], preferred_element_type=F32)
        yc_ref[...] = y[:, :D_FOURIER].astype(BF16)
        ys_ref[...] = y[:, D_FOURIER:].astype(BF16)
    else:
        k_ref, v_ref = refs[4:]
        k_ref[...] = p[:, D_DIFF:2 * D_DIFF].astype(BF16)
    v_ref[...] = p[:, 2 * D_DIFF:3 * D_DIFF].astype(BF16)


def _inproj(x, mod3, mod_index, g, w_in, *, tm, rope=None, chan_dft=None):
    t = x.shape[0]
    latent = rope is not None
    tile = lambda w: pl.BlockSpec((tm, w), lambda i: (i, 0))
    staged = _StagedWeights([w_in])
    in_specs = [tile(D_MODEL),
                pl.BlockSpec((1, 1, N_MOD * D_MODEL), lambda i: (mod_index(i), 0, 0)),
                _const_spec((1, D_MODEL))] + staged.in_specs
    args = [x, mod3, g, w_in]
    wide = jax.ShapeDtypeStruct((t, D_DIFF), BF16)
    if latent:
        tiles_per_seq = rope[0].shape[0] // tm
        rope_spec = pl.BlockSpec((tm, HEAD_W), lambda i: (i % tiles_per_seq, 0))
        in_specs += [rope_spec, rope_spec, rope_spec, _const_spec((D_FOURIER, 2 * D_FOURIER))]
        args += [*rope, chan_dft]
        narrow = jax.ShapeDtypeStruct((tiles_per_seq * tm, t // (tiles_per_seq * tm) * D_FOURIER), BF16)
        by_batch = pl.BlockSpec((tm, D_FOURIER), lambda i: (i % tiles_per_seq, i // tiles_per_seq))
        out_shape = [wide, wide, wide, narrow, narrow]
        out_specs = [tile(D_DIFF)] * 3 + [by_batch] * 2
    else:
        out_shape = [wide, wide]
        out_specs = [tile(D_DIFF)] * 2
    return pl.pallas_call(
        functools.partial(_inproj_kernel, staged=staged, latent=latent),
        grid=(t // tm,),
        in_specs=in_specs,
        out_specs=out_specs,
        out_shape=out_shape,
        scratch_shapes=staged.scratch,
        compiler_params=_params(("arbitrary",)),
        name="inproj" if latent else "inproj_ctx",
    )(*args)


def _max_key_norm_sq(kk):
    sq = (kk.astype(F32) ** 2).astype(BF16)
    r = lax.broadcasted_iota(jnp.int32, (HEAD_W, HEAD_W), 0)
    c = lax.broadcasted_iota(jnp.int32, (HEAD_W, HEAD_W), 1)
    sel = jnp.where((c == 0) == (r < HEAD_DIM), 1.0, 0.0) * jnp.where(c < 2, 1.0, 0.0)
    sums = jnp.dot(sq, sel.astype(BF16), preferred_element_type=F32)
    biggest = jnp.max(sums, axis=0, keepdims=True) * (1.0 + 2.0 ** -7)
    return biggest[:, 0:1], biggest[:, 1:2]


def _attn_kernel(lam_ref, g_ref, q_ref, k_ref, kc_ref, v_ref, vc_ref, o_ref, vt_ref, vct_ref, kn_ref, *, tk):
    tq = q_ref.shape[1]

    @pl.when(pl.program_id(2) == 0)
    def _():
        vt_ref[...] = v_ref[0].T
        vct_ref[...] = vc_ref[0].T
        (a1, a2), (b1, b2) = _max_key_norm_sq(k_ref[0]), _max_key_norm_sq(kc_ref[0])
        kn_ref[...] = jnp.sqrt(jnp.concatenate([jnp.broadcast_to(jnp.maximum(a1, b1), (1, tq)),
                                                 jnp.broadcast_to(jnp.maximum(a2, b2), (1, tq))], axis=1))

    lv = lam_ref[...]
    lam = (jnp.exp(jnp.sum(lv[0:1] * lv[1:2], axis=1, keepdims=True))
           - jnp.exp(jnp.sum(lv[2:3] * lv[3:4], axis=1, keepdims=True)) + LAMBDA_INIT)

    qt = q_ref[0].astype(F32).T
    top = lax.broadcasted_iota(jnp.int32, qt.shape, 0) < HEAD_DIM
    q1t, q2t = jnp.where(top, qt, 0.0), jnp.where(top, 0.0, qt)
    qcat = jnp.concatenate([q1t, q2t], axis=1).astype(BF16)
    chunks = [(k_ref[0, c * tk:(c + 1) * tk, :], vt_ref[:, c * tk:(c + 1) * tk]) for c in range(k_ref.shape[1] // tk)]
    chunks.append((kc_ref[0], vct_ref[...]))

    def scores(kk):
        return jnp.dot(kk, qcat, preferred_element_type=F32)

    def softmax_sums(shift):
        lsum, acc = 0.0, 0.0
        for kk, vt in chunks:
            e = jnp.exp2(scores(kk) - shift)
            lsum += jnp.sum(e.reshape(-1, 8, 2 * tq), axis=0)
            acc += jnp.dot(vt, e.astype(BF16), preferred_element_type=F32)
        return jnp.sum(lsum, axis=0, keepdims=True), acc

    def finish(l, acc):
        ot = acc[:, :tq] * (1.0 / l[:, :tq]) - acc[:, tq:] * (lam / l[:, tq:])
        o_ref[0] = (_rmsnorm(ot.T, g_ref[...]) * (1.0 - LAMBDA_INIT)).astype(BF16)

    qn = jnp.sqrt(jnp.concatenate([jnp.sum(q1t * q1t, axis=0, keepdims=True),
                                   jnp.sum(q2t * q2t, axis=0, keepdims=True)], axis=1))
    l, acc = softmax_sums(qn * kn_ref[...])
    finish(l, acc)

    @pl.when(jnp.min(l) < UNDERFLOW_GUARD)
    def _():
        m = None
        for kk, _ in chunks:
            cm = jnp.max(scores(kk), axis=0, keepdims=True)
            m = cm if m is None else jnp.maximum(m, cm)
        finish(*softmax_sums(m))


def _attention(lamvec, subln_g, q, k, kc, v, vc, *, tq, tk):
    b, n, _ = q.shape
    c = kc.shape[1]
    head_block = lambda rows: pl.BlockSpec((1, rows, HEAD_W), lambda bi, hi, qi: (bi, 0, hi))
    return pl.pallas_call(
        functools.partial(_attn_kernel, tk=tk),
        grid=(b, N_HEADS, n // tq),
        in_specs=[_const_spec(lamvec.shape), _const_spec((1, HEAD_W)),
                  pl.BlockSpec((1, tq, HEAD_W), lambda bi, hi, qi: (bi, qi, hi)),
                  head_block(n), head_block(c), head_block(n), head_block(c)],
        out_specs=pl.BlockSpec((1, tq, HEAD_W), lambda bi, hi, qi: (bi, qi, hi)),
        out_shape=jax.ShapeDtypeStruct((b, n, D_DIFF), BF16),
        scratch_shapes=[pltpu.VMEM((HEAD_W, n), BF16), pltpu.VMEM((HEAD_W, c), BF16),
                        pltpu.VMEM((1, 2 * tq), F32)],
        compiler_params=_params(("parallel", "parallel", "arbitrary")),
        name="diff_attn",
    )(lamvec, subln_g, q, k, kc, v, vc)


def _dft_kernel(yc_ref, ys_ref, m1_ref, twr_ref, twi_ref, w2_ref, wf_ref, o_ref, br_ref, bi_ref):
    n1, g, cols = yc_ref.shape
    n2 = br_ref.shape[0] // n1
    n_groups = n2 // g
    i = pl.program_id(0)

    @pl.when(i < n_groups)
    def _():
        u = jnp.concatenate([yc_ref[...].reshape(n1 * g, cols), ys_ref[...].reshape(n1 * g, cols)], axis=0)
        a = jnp.dot(m1_ref[...], u, preferred_element_type=F32)
        ar, ai = a[:n1 * g], a[n1 * g:]
        tr, ti = twr_ref[i], twi_ref[i]
        br = (ar * tr - ai * ti).astype(BF16)
        bi = (ar * ti + ai * tr).astype(BF16)
        for c in range(n1):
            dst = pl.ds(pl.multiple_of(c * n2 + i * g, g), g)
            br_ref[dst, :] = br[c * g:(c + 1) * g]
            bi_ref[dst, :] = bi[c * g:(c + 1) * g]

    @pl.when(i >= n_groups)
    def _():
        src = pl.ds(pl.multiple_of((i - n_groups) * n2, n2), n2)
        b = jnp.concatenate([br_ref[src, :], bi_ref[src, :]], axis=0)
        x = jnp.dot(w2_ref[...], b, preferred_element_type=F32)
        w = wf_ref.shape[0]
        wf = wf_ref[...].astype(BF16)
        for g in range(cols // w):
            o_ref[:, g * w:(g + 1) * w] = jnp.dot(x[:, g * w:(g + 1) * w].astype(BF16), wf,
                                                 preferred_element_type=F32).astype(BF16)


def _position_dft(yc, ys, wf):
    n, cols = yc.shape
    n1, g = DFT_N1, BF16_SUBLANES
    n2 = n // n1
    n_groups = n2 // g
    m1, twr, twi, w2 = _position_dft_tables(n, n1, g)
    group = pl.BlockSpec((n1, g, cols), lambda i: (0, jnp.minimum(i, n_groups - 1), 0))
    return pl.pallas_call(
        _dft_kernel,
        grid=(n_groups + n1,),
        in_specs=[group, group, _const_spec(m1.shape), _const_spec(twr.shape), _const_spec(twi.shape),
                  _const_spec(w2.shape), _const_spec(wf.shape)],
        out_specs=pl.BlockSpec((n2, cols), lambda i: (jnp.maximum(i - n_groups, 0), 0)),
        out_shape=jax.ShapeDtypeStruct((n, cols), BF16),
        scratch_shapes=[pltpu.VMEM((n, cols), BF16), pltpu.VMEM((n, cols), BF16)],
        compiler_params=_params(("arbitrary",)),
        name="position_dft",
    )(yc.reshape(n1, n2, cols), ys.reshape(n1, n2, cols), m1, twr, twi, w2, wf)


def _rope_tables(n):
    rows = n // GRID_W
    lane = np.arange(HEAD_W)
    sub = lane % HEAD_DIM
    on_row_axis = jnp.asarray(sub // (2 * ROPE_PAIRS) == 0)[None, :]
    second_half = jnp.asarray((sub % (2 * ROPE_PAIRS)) // ROPE_PAIRS == 1)[None, :]
    inv_freq = ROPE_BASE ** (-jnp.asarray(sub % ROPE_PAIRS, dtype=F32) / ROPE_PAIRS)

    def table(fn, keep):
        def side(size, mine):
            t = fn(jnp.arange(size, dtype=F32)[:, None] * inv_freq[None, :])
            return jnp.where(mine & keep, t, 0.0)
        return (side(rows, on_row_axis)[:, None, :] + side(GRID_W, ~on_row_axis)[None, :, :]).reshape(n, HEAD_W)

    return (table(jnp.cos, True), table(lambda a: -jnp.sin(a), ~second_half), table(jnp.sin, second_half))


def _channel_dft_table():
    c = np.arange(FOURIER_GROUP_DIM)
    ang = 2.0 * np.pi * np.outer(c, c) / FOURIER_GROUP_DIM
    eye = np.eye(D_FOURIER // FOURIER_GROUP_DIM)
    scale = FOURIER_GROUP_DIM ** -0.5
    return np.concatenate([np.kron(eye, np.cos(ang)), np.kron(eye, np.sin(ang))], axis=1) * scale


def _position_dft_tables(n, n1, g):
    n2 = n // n1
    w1 = np.exp(-2j * np.pi * np.outer(np.arange(n1), np.arange(n1)) / n1)
    m = np.kron(w1, np.eye(g)) * n ** -0.5
    m1 = np.block([[m.real, m.imag], [m.imag, -m.real]])
    b = (g * np.arange(n2 // g)[:, None, None] + np.arange(g)[None, None, :])
    tw = np.exp(-2j * np.pi * b * np.arange(n1)[None, :, None] / n).reshape(n2 // g, n1 * g, 1)
    ang2 = 2.0 * np.pi * np.outer(np.arange(n2), np.arange(n2)) / n2
    w2 = np.concatenate([np.cos(ang2), np.sin(ang2)], axis=1)
    f32 = lambda t: jnp.asarray(t, dtype=F32)
    return f32(m1).astype(BF16), f32(tw.real), f32(tw.imag), f32(w2).astype(BF16)


def _row_permutation(n1, rows):
    d_per_tile = rows // n1
    p = np.zeros((rows, rows), np.float32)
    c, d = np.meshgrid(np.arange(n1), np.arange(d_per_tile), indexing="ij")
    p[(c + n1 * d).ravel(), (c * d_per_tile + d).ravel()] = 1.0
    return jnp.asarray(p).astype(BF16)


def kernel(x, c, ctx, c_ctx, w_ada, b_ada, norm1_g, ffn1_w_gate, ffn1_w_up, ffn1_w_down, norm_mix_g, w_in,
           lambda_q1, lambda_k1, lambda_q2, lambda_k2, subln_g, w_fourier, w_out, norm2_g, ffn2_w_gate,
           ffn2_w_up, ffn2_w_down, final_norm_g):
    b, n, d = x.shape
    n_ctx = ctx.shape[1]
    assert (d, w_ada.shape[0]) == (D_MODEL, 1) and b + 1 <= MOD_ROWS
    tm = 512
    tiles_per_seq = n // tm
    mod_row = lambda i: jnp.minimum(i // tiles_per_seq, b)
    row = lambda g: g.reshape(1, -1)

    cc = jnp.zeros((MOD_ROWS, d), F32).at[:b].set(c).at[b].set(c_ctx)
    mod3 = _ada(cc, w_ada[0], b_ada).reshape(MOD_ROWS, 1, N_MOD * d)

    x1, c1 = _ffn(x.reshape(b * n, d), mod3, mod_row, row(norm1_g), ffn1_w_gate[0], ffn1_w_up[0], ffn1_w_down[0],
                  tm=tm, mod_base=0, tail=ctx.reshape(b * n_ctx, d))

    chan_dft = jnp.asarray(_channel_dft_table(), dtype=F32).astype(BF16)
    q, k, v, yc, ys = _inproj(x1, mod3, mod_row, row(norm_mix_g), w_in[0], tm=tm,
                              rope=_rope_tables(n), chan_dft=chan_dft)
    kc, vc = _inproj(c1, mod3, lambda i: b, row(norm_mix_g), w_in[0], tm=tm)

    lamvec = jnp.concatenate([lambda_q1, lambda_k1, lambda_q2, lambda_k2], axis=0)
    seq = lambda a, rows: a.reshape(b, rows, a.shape[-1])
    att = _attention(lamvec, row(subln_g), seq(q, n), seq(k, n), seq(kc, n_ctx), seq(v, n), seq(vc, n_ctx),
                     tq=1024, tk=1024)

    zf = _position_dft(yc, ys, w_fourier[0])

    out = _ffn(x1, mod3, mod_row, row(norm2_g), ffn2_w_gate[0], ffn2_w_up[0], ffn2_w_down[0],
               tm=tm, mod_base=6, mix=(att.reshape(b * n, D_DIFF), zf, w_out[0]), tiles_per_seq=tiles_per_seq,
               final_g=row(final_norm_g))
    return out.reshape(b, n, d)
```

```python
import functools
import math

import numpy as np
import jax
import jax.numpy as jnp
from jax import lax
from jax.experimental import pallas as pl
from jax.experimental.pallas import tpu as pltpu

D_MODEL = 1024
GRID_W = 64
D_FOURIER = 256
D_DIFF = 768
HEAD_DIM = 64
HEAD_W = 2 * HEAD_DIM
N_HEADS = D_DIFF // HEAD_W
FOURIER_GROUP_DIM = 64
D_IN_PROJ = 3 * D_DIFF + D_FOURIER
D_FF = 2816
N_MOD = 9
ROPE_BASE = 10000.0
ROPE_PAIRS = HEAD_DIM // 4
RMS_EPS = 1e-6
ATTN_SCALE = HEAD_DIM ** -0.5
Q_SCALE = ATTN_SCALE * math.log2(math.e)
LAMBDA_INIT = 0.8 - 0.6 * math.exp(-0.3 * 0)
UNDERFLOW_GUARD = 2.0 ** -80

F32 = jnp.float32
BF16 = jnp.bfloat16

V7X_VMEM_LIMIT_BYTES = 56 * 1024 * 1024
BF16_SUBLANES = 16
STAGE_CHUNK_BYTES = 3 * 512 * 1024
MOD_ROWS = 8
DFT_N1 = 16
FF_CHUNKS = ((0, 1536), (1536, 1280))


def _const_spec(shape):
    return pl.BlockSpec(shape, lambda *_: (0,) * len(shape), pipeline_mode=pl.Buffered(1))


def _params(semantics):
    return pltpu.CompilerParams(dimension_semantics=semantics, vmem_limit_bytes=V7X_VMEM_LIMIT_BYTES)


def _rmsnorm(x, g):
    return x * lax.rsqrt(jnp.mean(x * x, axis=-1, keepdims=True) + RMS_EPS) * g


def _mod_row(mod_ref, k):
    return mod_ref[0, :, k * D_MODEL:(k + 1) * D_MODEL]


def _ada_kernel(cc_ref, w_ref, b_ref, o_ref):
    cc = cc_ref[...]
    s = cc * jax.nn.sigmoid(cc)
    o_ref[...] = jnp.dot(s.astype(BF16), w_ref[...].astype(BF16), preferred_element_type=F32) + b_ref[...]


def _ada(cc, w_ada, b_ada):
    n_out = w_ada.shape[1]
    tn = D_MODEL
    return pl.pallas_call(
        _ada_kernel,
        grid=(n_out // tn,),
        in_specs=[_const_spec((MOD_ROWS, D_MODEL)),
                  pl.BlockSpec((D_MODEL, tn), lambda j: (0, j)),
                  pl.BlockSpec((1, tn), lambda j: (0, j))],
        out_specs=pl.BlockSpec((MOD_ROWS, tn), lambda j: (0, j)),
        out_shape=jax.ShapeDtypeStruct((MOD_ROWS, n_out), F32),
        compiler_params=_params(("arbitrary",)),
        name="ada",
    )(cc, w_ada, b_ada)


def _stage_rows(k, n):
    fits = [r for r in range(BF16_SUBLANES, k + 1, BF16_SUBLANES) if k % r == 0 and r * n * 4 <= STAGE_CHUNK_BYTES]
    return max(fits)


def _stage_weight(src_hbm, dst_ref, stage_ref, sem_ref):
    rows = stage_ref.shape[1]
    n_chunks = src_hbm.shape[0] // rows

    def copy(c):
        return pltpu.make_async_copy(src_hbm.at[pl.ds(c * rows, rows)], stage_ref.at[c % 2], sem_ref.at[c % 2])

    copy(0).start()
    for c in range(n_chunks):
        if c + 1 < n_chunks:
            copy(c + 1).start()
        copy(c).wait()
        dst_ref[c * rows:(c + 1) * rows, :] = stage_ref[c % 2].astype(BF16)


class _StagedWeights:
    def __init__(self, weights):
        self.weights = list(weights)
        self.in_specs = [pl.BlockSpec(memory_space=pl.ANY)] * len(self.weights)
        chunk_shapes = sorted({(_stage_rows(*w.shape), w.shape[1]) for w in self.weights})
        self.stage_of = [chunk_shapes.index((_stage_rows(*w.shape), w.shape[1])) for w in self.weights]
        self.scratch = ([pltpu.VMEM(w.shape, BF16) for w in self.weights]
                        + [pltpu.VMEM((2,) + s, F32) for s in chunk_shapes]
                        + [pltpu.SemaphoreType.DMA((2,))])

    def load(self, hbm_refs, scratch_refs):
        n = len(self.weights)
        dst, stages, sem = scratch_refs[:n], scratch_refs[n:-1], scratch_refs[-1]

        @pl.when(pl.program_id(0) == 0)
        def _():
            for src, d, s in zip(hbm_refs, dst, self.stage_of):
                _stage_weight(src, d, stages[s], sem)

        return dst


def _ffn_kernel(*refs, staged, mod_base, has_mix, final_norm, n_main):
    n_w = len(staged.weights)
    n_scratch = len(staged.scratch)
    x_ref, mod_ref, g_ref = refs[:3]
    w_hbm = refs[3:3 + n_w]
    rest = list(refs[3 + n_w:len(refs) - n_scratch])
    weights = staged.load(w_hbm, refs[len(refs) - n_scratch:])
    wg_ref, wu_ref, wd_ref = weights[:3]
    has_tail = n_main is not None
    if has_tail:
        tail_ref = rest[0]
        tail_o_ref = rest[-1]
        rest = rest[1:-1]
        in_tail = pl.program_id(0) >= n_main
    if has_mix:
        att_ref, zf_ref = rest[:2]
        rest = rest[2:]
        wo_ref = weights[3]
    if final_norm:
        gf_ref = rest[0]
        rest = rest[1:]
    (o_ref,) = rest

    x = x_ref[...]
    if has_tail:
        x = jnp.where(in_tail, tail_ref[...], x)
    if has_mix:
        mix = jnp.dot(att_ref[...], wo_ref[:D_DIFF, :], preferred_element_type=F32)
        mix += jnp.dot(zf_ref[...], wo_ref[D_DIFF:, :], preferred_element_type=F32)
        x = x + _mod_row(mod_ref, 5) * mix
    h = _rmsnorm(x, g_ref[...]) * (1.0 + _mod_row(mod_ref, mod_base + 1)) + _mod_row(mod_ref, mod_base)
    h = h.astype(BF16)
    acc = None
    for start, width in FF_CHUNKS:
        gate = jnp.dot(h, wg_ref[:, start:start + width], preferred_element_type=F32)
        up = jnp.dot(h, wu_ref[:, start:start + width], preferred_element_type=F32)
        a = (gate * jax.nn.sigmoid(gate) * up).astype(BF16)
        part = jnp.dot(a, wd_ref[start:start + width, :], preferred_element_type=F32)
        acc = part if acc is None else acc + part
    x = x + (0.5 * _mod_row(mod_ref, mod_base + 2)) * acc
    if final_norm:
        x = _rmsnorm(x, gf_ref[...])
    if has_tail:
        @pl.when(in_tail)
        def _():
            tail_o_ref[...] = x

        @pl.when(jnp.logical_not(in_tail))
        def _():
            o_ref[...] = x
    else:
        o_ref[...] = x


def _ffn(x, mod3, mod_index, g, wg, wu, wd, *, tm, mod_base, tail=None, mix=None, tiles_per_seq=None, final_g=None):
    t = x.shape[0]
    n_main = t // tm
    n_tiles = n_main + (tail.shape[0] // tm if tail is not None else 0)
    tile = lambda w: pl.BlockSpec((tm, w), lambda i: (jnp.minimum(i, n_main - 1), 0))
    tail_tile = pl.BlockSpec((tm, D_MODEL), lambda i: (jnp.maximum(i - n_main, 0), 0))
    staged = _StagedWeights([wg, wu, wd] + ([mix[2]] if mix is not None else []))
    in_specs = [tile(D_MODEL),
                pl.BlockSpec((1, 1, N_MOD * D_MODEL), lambda i: (mod_index(i), 0, 0)),
                _const_spec((1, D_MODEL))] + staged.in_specs
    args = [x, mod3, g] + staged.weights
    out_specs, out_shape = tile(D_MODEL), jax.ShapeDtypeStruct((t, D_MODEL), F32)
    if tail is not None:
        in_specs.append(tail_tile)
        args.append(tail)
        out_specs, out_shape = [out_specs, tail_tile], [out_shape, jax.ShapeDtypeStruct(tail.shape, F32)]
    if mix is not None:
        att, zf, _ = mix
        by_batch = pl.BlockSpec((tm, D_FOURIER), lambda i: (i % tiles_per_seq, i // tiles_per_seq))
        in_specs += [tile(D_DIFF), by_batch]
        args += [att, zf]
    if final_g is not None:
        in_specs.append(_const_spec((1, D_MODEL)))
        args.append(final_g)
    kern = functools.partial(_ffn_kernel, staged=staged, mod_base=mod_base, has_mix=mix is not None,
                             final_norm=final_g is not None, n_main=n_main if tail is not None else None)
    return pl.pallas_call(
        kern,
        grid=(n_tiles,),
        in_specs=in_specs,
        out_specs=out_specs,
        out_shape=out_shape,
        scratch_shapes=staged.scratch,
        compiler_params=_params(("arbitrary",)),
        name="ffn_mix" if mix is not None else "ffn",
    )(*args)


def _rope(x, cos, sin_lo, sin_hi):
    return (x * cos + pltpu.roll(x, HEAD_W - ROPE_PAIRS, 1) * sin_lo
            + pltpu.roll(x, ROPE_PAIRS, 1) * sin_hi)


def _inproj_kernel(*refs, staged, latent):
    n_scratch = len(staged.scratch)
    x_ref, mod_ref, g_ref = refs[:3]
    (w_ref,) = staged.load(refs[3:4], refs[len(refs) - n_scratch:])
    refs = refs[:len(refs) - n_scratch]
    h = _rmsnorm(x_ref[...], g_ref[...]) * (1.0 + _mod_row(mod_ref, 4)) + _mod_row(mod_ref, 3)
    p = jnp.dot(h.astype(BF16), w_ref[...], preferred_element_type=F32)
    if latent:
        cos_ref, slo_ref, shi_ref, dft_ref, q_ref, k_ref, v_ref, yc_ref, ys_ref = refs[4:]
        cos, slo, shi = cos_ref[...], slo_ref[...], shi_ref[...]
        for hd in range(N_HEADS):
            lo = hd * HEAD_W
            q_ref[:, lo:lo + HEAD_W] = (_rope(p[:, lo:lo + HEAD_W], cos, slo, shi) * Q_SCALE).astype(BF16)
            k_ref[:, lo:lo + HEAD_W] = _rope(p[:, D_DIFF + lo:D_DIFF + lo + HEAD_W], cos, slo, shi).astype(BF16)
        y = jnp.dot(p[:, 3 * D_DIFF:].astype(BF16), dft_ref[...], preferred_element_type=F32)
        yc_ref[...] = y[:, :D_FOURIER].astype(BF16)
        ys_ref[...] = y[:, D_FOURIER:].astype(BF16)
    else:
        k_ref, v_ref = refs[4:]
        k_ref[...] = p[:, D_DIFF:2 * D_DIFF].astype(BF16)
    v_ref[...] = p[:, 2 * D_DIFF:3 * D_DIFF].astype(BF16)


def _inproj(x, mod3, mod_index, g, w_in, *, tm, rope=None, chan_dft=None):
    t = x.shape[0]
    latent = rope is not None
    tile = lambda w: pl.BlockSpec((tm, w), lambda i: (i, 0))
    staged = _StagedWeights([w_in])
    in_specs = [tile(D_MODEL),
                pl.BlockSpec((1, 1, N_MOD * D_MODEL), lambda i: (mod_index(i), 0, 0)),
                _const_spec((1, D_MODEL))] + staged.in_specs
    args = [x, mod3, g, w_in]
    wide = jax.ShapeDtypeStruct((t, D_DIFF), BF16)
    if latent:
        tiles_per_seq = rope[0].shape[0] // tm
        rope_spec = pl.BlockSpec((tm, HEAD_W), lambda i: (i % tiles_per_seq, 0))
        in_specs += [rope_spec, rope_spec, rope_spec, _const_spec((D_FOURIER, 2 * D_FOURIER))]
        args += [*rope, chan_dft]
        narrow = jax.ShapeDtypeStruct((tiles_per_seq * tm, t // (tiles_per_seq * tm) * D_FOURIER), BF16)
        by_batch = pl.BlockSpec((tm, D_FOURIER), lambda i: (i % tiles_per_seq, i // tiles_per_seq))
        out_shape = [wide, wide, wide, narrow, narrow]
        out_specs = [tile(D_DIFF)] * 3 + [by_batch] * 2
    else:
        out_shape = [wide, wide]
        out_specs = [tile(D_DIFF)] * 2
    return pl.pallas_call(
        functools.partial(_inproj_kernel, staged=staged, latent=latent),
        grid=(t // tm,),
        in_specs=in_specs,
        out_specs=out_specs,
        out_shape=out_shape,
        scratch_shapes=staged.scratch,
        compiler_params=_params(("arbitrary",)),
        name="inproj" if latent else "inproj_ctx",
    )(*args)


def _max_key_norm_sq(kk):
    sq = (kk.astype(F32) ** 2).astype(BF16)
    r = lax.broadcasted_iota(jnp.int32, (HEAD_W, HEAD_W), 0)
    c = lax.broadcasted_iota(jnp.int32, (HEAD_W, HEAD_W), 1)
    sel = jnp.where((c == 0) == (r < HEAD_DIM), 1.0, 0.0) * jnp.where(c < 2, 1.0, 0.0)
    sums = jnp.dot(sq, sel.astype(BF16), preferred_element_type=F32)
    biggest = jnp.max(sums, axis=0, keepdims=True) * (1.0 + 2.0 ** -7)
    return biggest[:, 0:1], biggest[:, 1:2]


def _attn_kernel(lam_ref, g_ref, q_ref, k_ref, kc_ref, v_ref, vc_ref, o_ref, vt_ref, vct_ref, kn_ref, *, tk):
    tq = q_ref.shape[1]

    @pl.when(pl.program_id(2) == 0)
    def _():
        vt_ref[...] = v_ref[0].T
        vct_ref[...] = vc_ref[0].T
        (a1, a2), (b1, b2) = _max_key_norm_sq(k_ref[0]), _max_key_norm_sq(kc_ref[0])
        kn_ref[...] = jnp.sqrt(jnp.concatenate([jnp.broadcast_to(jnp.maximum(a1, b1), (1, tq)),
                                                 jnp.broadcast_to(jnp.maximum(a2, b2), (1, tq))], axis=1))

    lv = lam_ref[...]
    lam = (jnp.exp(jnp.sum(lv[0:1] * lv[1:2], axis=1, keepdims=True))
           - jnp.exp(jnp.sum(lv[2:3] * lv[3:4], axis=1, keepdims=True)) + LAMBDA_INIT)

    qt = q_ref[0].astype(F32).T
    top = lax.broadcasted_iota(jnp.int32, qt.shape, 0) < HEAD_DIM
    q1t, q2t = jnp.where(top, qt, 0.0), jnp.where(top, 0.0, qt)
    qcat = jnp.concatenate([q1t, q2t], axis=1).astype(BF16)
    chunks = [(k_ref[0, c * tk:(c + 1) * tk, :], vt_ref[:, c * tk:(c + 1) * tk]) for c in range(k_ref.shape[1] // tk)]
    chunks.append((kc_ref[0], vct_ref[...]))

    def scores(kk):
        return jnp.dot(kk, qcat, preferred_element_type=F32)

    def softmax_sums(shift):
        lsum, acc = 0.0, 0.0
        for kk, vt in chunks:
            e = jnp.exp2(scores(kk) - shift)
            lsum += jnp.sum(e.reshape(-1, 8, 2 * tq), axis=0)
            acc += jnp.dot(vt, e.astype(BF16), preferred_element_type=F32)
        return jnp.sum(lsum, axis=0, keepdims=True), acc

    def finish(l, acc):
        ot = acc[:, :tq] * (1.0 / l[:, :tq]) - acc[:, tq:] * (lam / l[:, tq:])
        o_ref[0] = (_rmsnorm(ot.T, g_ref[...]) * (1.0 - LAMBDA_INIT)).astype(BF16)

    qn = jnp.sqrt(jnp.concatenate([jnp.sum(q1t * q1t, axis=0, keepdims=True),
                                   jnp.sum(q2t * q2t, axis=0, keepdims=True)], axis=1))
    l, acc = softmax_sums(qn * kn_ref[...])
    finish(l, acc)

    @pl.when(jnp.min(l) < UNDERFLOW_GUARD)
    def _():
        m = None
        for kk, _ in chunks:
            cm = jnp.max(scores(kk), axis=0, keepdims=True)
            m = cm if m is None else jnp.maximum(m, cm)
        finish(*softmax_sums(m))


def _attention(lamvec, subln_g, q, k, kc, v, vc, *, tq, tk):
    b, n, _ = q.shape
    c = kc.shape[1]
    head_block = lambda rows: pl.BlockSpec((1, rows, HEAD_W), lambda bi, hi, qi: (bi, 0, hi))
    return pl.pallas_call(
        functools.partial(_attn_kernel, tk=tk),
        grid=(b, N_HEADS, n // tq),
        in_specs=[_const_spec(lamvec.shape), _const_spec((1, HEAD_W)),
                  pl.BlockSpec((1, tq, HEAD_W), lambda bi, hi, qi: (bi, qi, hi)),
                  head_block(n), head_block(c), head_block(n), head_block(c)],
        out_specs=pl.BlockSpec((1, tq, HEAD_W), lambda bi, hi, qi: (bi, qi, hi)),
        out_shape=jax.ShapeDtypeStruct((b, n, D_DIFF), BF16),
        scratch_shapes=[pltpu.VMEM((HEAD_W, n), BF16), pltpu.VMEM((HEAD_W, c), BF16),
                        pltpu.VMEM((1, 2 * tq), F32)],
        compiler_params=_params(("parallel", "parallel", "arbitrary")),
        name="diff_attn",
    )(lamvec, subln_g, q, k, kc, v, vc)


def _dft_kernel(yc_ref, ys_ref, m1_ref, twr_ref, twi_ref, w2_ref, wf_ref, perm_ref, o_ref, br_ref, bi_ref):
    n1, g, cols = yc_ref.shape
    n2 = br_ref.shape[0] // n1
    n_groups = n2 // g
    i = pl.program_id(0)

    @pl.when(i < n_groups)
    def _():
        u = jnp.concatenate([yc_ref[...].reshape(n1 * g, cols), ys_ref[...].reshape(n1 * g, cols)], axis=0)
        a = jnp.dot(m1_ref[...], u, preferred_element_type=F32)
        ar, ai = a[:n1 * g], a[n1 * g:]
        tr, ti = twr_ref[i], twi_ref[i]
        br = (ar * tr - ai * ti).astype(BF16)
        bi = (ar * ti + ai * tr).astype(BF16)
        for c in range(n1):
            dst = pl.ds(pl.multiple_of(c * n2 + i * g, g), g)
            br_ref[dst, :] = br[c * g:(c + 1) * g]
            bi_ref[dst, :] = bi[c * g:(c + 1) * g]

    @pl.when((i >= n_groups) & (i < n_groups + n1))
    def _():
        src = pl.ds(pl.multiple_of((i - n_groups) * n2, n2), n2)
        b = jnp.concatenate([br_ref[src, :], bi_ref[src, :]], axis=0)
        x = jnp.dot(w2_ref[...], b, preferred_element_type=F32)
        w = wf_ref.shape[0]
        wf = wf_ref[...].astype(BF16)
        br_ref[src, :] = jnp.concatenate(
            [jnp.dot(x[:, lo:lo + w].astype(BF16), wf, preferred_element_type=F32).astype(BF16)
             for lo in range(0, cols, w)], axis=1)

    @pl.when(i >= n_groups + n1)
    def _():
        run = o_ref.shape[0] // n1
        first = (i - n_groups - n1) * run
        rows = jnp.concatenate([br_ref[pl.ds(pl.multiple_of(c * n2 + first, run), run), :] for c in range(n1)], axis=0)
        o_ref[...] = jnp.dot(perm_ref[...], rows, preferred_element_type=F32).astype(BF16)


def _position_dft(yc, ys, wf, *, tile_rows):
    n, cols = yc.shape
    n1, g = DFT_N1, BF16_SUBLANES
    n2 = n // n1
    n_groups = n2 // g
    n_tiles = n // tile_rows
    m1, twr, twi, w2 = _position_dft_tables(n, n1, g)
    perm = _row_permutation(n1, tile_rows)
    group = pl.BlockSpec((n1, g, cols), lambda i: (0, jnp.minimum(i, n_groups - 1), 0))
    return pl.pallas_call(
        _dft_kernel,
        grid=(n_groups + n1 + n_tiles,),
        in_specs=[group, group, _const_spec(m1.shape), _const_spec(twr.shape), _const_spec(twi.shape),
                  _const_spec(w2.shape), _const_spec(wf.shape), _const_spec(perm.shape)],
        out_specs=pl.BlockSpec((tile_rows, cols), lambda i: (jnp.maximum(i - n_groups - n1, 0), 0)),
        out_shape=jax.ShapeDtypeStruct((n, cols), BF16),
        scratch_shapes=[pltpu.VMEM((n, cols), BF16), pltpu.VMEM((n, cols), BF16)],
        compiler_params=_params(("arbitrary",)),
        name="position_dft",
    )(yc.reshape(n1, n2, cols), ys.reshape(n1, n2, cols), m1, twr, twi, w2, wf, perm)


def _rope_tables(n):
    rows = n // GRID_W
    lane = np.arange(HEAD_W)
    sub = lane % HEAD_DIM
    on_row_axis = jnp.asarray(sub // (2 * ROPE_PAIRS) == 0)[None, :]
    second_half = jnp.asarray((sub % (2 * ROPE_PAIRS)) // ROPE_PAIRS == 1)[None, :]
    inv_freq = ROPE_BASE ** (-jnp.asarray(sub % ROPE_PAIRS, dtype=F32) / ROPE_PAIRS)

    def table(fn, keep):
        def side(size, mine):
            t = fn(jnp.arange(size, dtype=F32)[:, None] * inv_freq[None, :])
            return jnp.where(mine & keep, t, 0.0)
        return (side(rows, on_row_axis)[:, None, :] + side(GRID_W, ~on_row_axis)[None, :, :]).reshape(n, HEAD_W)

    return (table(jnp.cos, True), table(lambda a: -jnp.sin(a), ~second_half), table(jnp.sin, second_half))


def _channel_dft_table():
    c = np.arange(FOURIER_GROUP_DIM)
    ang = 2.0 * np.pi * np.outer(c, c) / FOURIER_GROUP_DIM
    eye = np.eye(D_FOURIER // FOURIER_GROUP_DIM)
    scale = FOURIER_GROUP_DIM ** -0.5
    return np.concatenate([np.kron(eye, np.cos(ang)), np.kron(eye, np.sin(ang))], axis=1) * scale


def _position_dft_tables(n, n1, g):
    n2 = n // n1
    w1 = np.exp(-2j * np.pi * np.outer(np.arange(n1), np.arange(n1)) / n1)
    m = np.kron(w1, np.eye(g)) * n ** -0.5
    m1 = np.block([[m.real, m.imag], [m.imag, -m.real]])
    b = (g * np.arange(n2 // g)[:, None, None] + np.arange(g)[None, None, :])
    tw = np.exp(-2j * np.pi * b * np.arange(n1)[None, :, None] / n).reshape(n2 // g, n1 * g, 1)
    ang2 = 2.0 * np.pi * np.outer(np.arange(n2), np.arange(n2)) / n2
    w2 = np.concatenate([np.cos(ang2), np.sin(ang2)], axis=1)
    f32 = lambda t: jnp.asarray(t, dtype=F32)
    return f32(m1).astype(BF16), f32(tw.real), f32(tw.imag), f32(w2).astype(BF16)


def _row_permutation(n1, rows):
    d_per_tile = rows // n1
    p = np.zeros((rows, rows), np.float32)
    c, d = np.meshgrid(np.arange(n1), np.arange(d_per_tile), indexing="ij")
    p[(c + n1 * d).ravel(), (c * d_per_tile + d).ravel()] = 1.0
    return jnp.asarray(p).astype(BF16)


def kernel(x, c, ctx, c_ctx, w_ada, b_ada, norm1_g, ffn1_w_gate, ffn1_w_up, ffn1_w_down, norm_mix_g, w_in,
           lambda_q1, lambda_k1, lambda_q2, lambda_k2, subln_g, w_fourier, w_out, norm2_g, ffn2_w_gate,
           ffn2_w_up, ffn2_w_down, final_norm_g):
    b, n, d = x.shape
    n_ctx = ctx.shape[1]
    assert (d, w_ada.shape[0]) == (D_MODEL, 1) and b + 1 <= MOD_ROWS
    tm = 512
    tiles_per_seq = n // tm
    mod_row = lambda i: jnp.minimum(i // tiles_per_seq, b)
    row = lambda g: g.reshape(1, -1)

    cc = jnp.zeros((MOD_ROWS, d), F32).at[:b].set(c).at[b].set(c_ctx)
    mod3 = _ada(cc, w_ada[0], b_ada).reshape(MOD_ROWS, 1, N_MOD * d)

    x1, c1 = _ffn(x.reshape(b * n, d), mod3, mod_row, row(norm1_g), ffn1_w_gate[0], ffn1_w_up[0], ffn1_w_down[0],
                  tm=tm, mod_base=0, tail=ctx.reshape(b * n_ctx, d))

    chan_dft = jnp.asarray(_channel_dft_table(), dtype=F32).astype(BF16)
    q, k, v, yc, ys = _inproj(x1, mod3, mod_row, row(norm_mix_g), w_in[0], tm=tm,
                              rope=_rope_tables(n), chan_dft=chan_dft)
    kc, vc = _inproj(c1, mod3, lambda i: b, row(norm_mix_g), w_in[0], tm=tm)

    lamvec = jnp.concatenate([lambda_q1, lambda_k1, lambda_q2, lambda_k2], axis=0)
    seq = lambda a, rows: a.reshape(b, rows, a.shape[-1])
    att = _attention(lamvec, row(subln_g), seq(q, n), seq(k, n), seq(kc, n_ctx), seq(v, n), seq(vc, n_ctx),
                     tq=1024, tk=1024)

    zf = _position_dft(yc, ys, w_fourier[0], tile_rows=tm)

    out = _ffn(x1, mod3, mod_row, row(norm2_g), ffn2_w_gate[0], ffn2_w_up[0], ffn2_w_down[0],
               tm=tm, mod_base=6, mix=(att.reshape(b * n, D_DIFF), zf, w_out[0]), tiles_per_seq=tiles_per_seq,
               final_g=row(final_norm_g))
    return out.reshape(b, n, d)
```

```python
import functools
import math

import numpy as np
import jax
import jax.numpy as jnp
from jax import lax
from jax.experimental import pallas as pl
from jax.experimental.pallas import tpu as pltpu

D_MODEL = 1024
GRID_W = 64
D_FOURIER = 256
D_DIFF = 768
HEAD_DIM = 64
HEAD_W = 2 * HEAD_DIM
N_HEADS = D_DIFF // HEAD_W
FOURIER_GROUP_DIM = 64
D_IN_PROJ = 3 * D_DIFF + D_FOURIER
D_FF = 2816
N_MOD = 9
ROPE_BASE = 10000.0
ROPE_PAIRS = HEAD_DIM // 4
RMS_EPS = 1e-6
ATTN_SCALE = HEAD_DIM ** -0.5
Q_SCALE = ATTN_SCALE * math.log2(math.e)
LAMBDA_INIT = 0.8 - 0.6 * math.exp(-0.3 * 0)
UNDERFLOW_GUARD = 2.0 ** -80

F32 = jnp.float32
BF16 = jnp.bfloat16

V7X_VMEM_LIMIT_BYTES = 56 * 1024 * 1024
BF16_SUBLANES = 16
STAGE_CHUNK_BYTES = 3 * 1024 * 1024
MOD_ROWS = 8
DFT_N1 = 16
FF_CHUNKS = ((0, 1536), (1536, 1280))


def _const_spec(shape):
    return pl.BlockSpec(shape, lambda *_: (0,) * len(shape), pipeline_mode=pl.Buffered(1))


def _params(semantics):
    return pltpu.CompilerParams(dimension_semantics=semantics, vmem_limit_bytes=V7X_VMEM_LIMIT_BYTES)


def _rmsnorm(x, g):
    return x * lax.rsqrt(jnp.mean(x * x, axis=-1, keepdims=True) + RMS_EPS) * g


def _mod_row(mod_ref, k):
    return mod_ref[0, :, k * D_MODEL:(k + 1) * D_MODEL]


def _ada_kernel(cc_ref, w_ref, b_ref, o_ref):
    cc = cc_ref[...]
    s = cc * jax.nn.sigmoid(cc)
    o_ref[...] = jnp.dot(s.astype(BF16), w_ref[...].astype(BF16), preferred_element_type=F32) + b_ref[...]


def _ada(cc, w_ada, b_ada):
    n_out = w_ada.shape[1]
    tn = n_out // 4
    return pl.pallas_call(
        _ada_kernel,
        grid=(n_out // tn,),
        in_specs=[_const_spec((MOD_ROWS, D_MODEL)),
                  pl.BlockSpec((D_MODEL, tn), lambda j: (0, j)),
                  pl.BlockSpec((1, tn), lambda j: (0, j))],
        out_specs=pl.BlockSpec((MOD_ROWS, tn), lambda j: (0, j)),
        out_shape=jax.ShapeDtypeStruct((MOD_ROWS, n_out), F32),
        compiler_params=_params(("arbitrary",)),
        name="ada",
    )(cc, w_ada, b_ada)


def _stage_rows(k, n):
    fits = [r for r in range(BF16_SUBLANES, k + 1, BF16_SUBLANES) if k % r == 0 and r * n * 4 <= STAGE_CHUNK_BYTES]
    return max(fits)


def _stage_weight(src_hbm, dst_ref, stage_ref, sem_ref):
    rows = stage_ref.shape[1]
    n_chunks = src_hbm.shape[0] // rows

    def copy(c):
        return pltpu.make_async_copy(src_hbm.at[pl.ds(c * rows, rows)], stage_ref.at[c % 2], sem_ref.at[c % 2])

    copy(0).start()
    for c in range(n_chunks):
        if c + 1 < n_chunks:
            copy(c + 1).start()
        copy(c).wait()
        dst_ref[c * rows:(c + 1) * rows, :] = stage_ref[c % 2].astype(BF16)


class _StagedWeights:
    def __init__(self, weights):
        self.weights = list(weights)
        self.in_specs = [pl.BlockSpec(memory_space=pl.ANY)] * len(self.weights)
        chunk_shapes = sorted({(_stage_rows(*w.shape), w.shape[1]) for w in self.weights})
        self.stage_of = [chunk_shapes.index((_stage_rows(*w.shape), w.shape[1])) for w in self.weights]
        self.scratch = ([pltpu.VMEM(w.shape, BF16) for w in self.weights]
                        + [pltpu.VMEM((2,) + s, F32) for s in chunk_shapes]
                        + [pltpu.SemaphoreType.DMA((2,))])

    def load(self, hbm_refs, scratch_refs):
        n = len(self.weights)
        dst, stages, sem = scratch_refs[:n], scratch_refs[n:-1], scratch_refs[-1]

        @pl.when(pl.program_id(0) == 0)
        def _():
            for src, d, s in zip(hbm_refs, dst, self.stage_of):
                _stage_weight(src, d, stages[s], sem)

        return dst


def _ffn_kernel(*refs, staged, mod_base, has_mix, final_norm, n_main):
    n_w = len(staged.weights)
    n_scratch = len(staged.scratch)
    x_ref, mod_ref, g_ref = refs[:3]
    w_hbm = refs[3:3 + n_w]
    rest = list(refs[3 + n_w:len(refs) - n_scratch])
    weights = staged.load(w_hbm, refs[len(refs) - n_scratch:])
    wg_ref, wu_ref, wd_ref = weights[:3]
    has_tail = n_main is not None
    if has_tail:
        tail_ref = rest[0]
        tail_o_ref = rest[-1]
        rest = rest[1:-1]
        in_tail = pl.program_id(0) >= n_main
    if has_mix:
        att_ref, zf_ref = rest[:2]
        rest = rest[2:]
        wo_ref = weights[3]
    if final_norm:
        gf_ref = rest[0]
        rest = rest[1:]
    (o_ref,) = rest

    x = x_ref[...]
    if has_tail:
        x = jnp.where(in_tail, tail_ref[...], x)
    if has_mix:
        mix = jnp.dot(att_ref[...], wo_ref[:D_DIFF, :], preferred_element_type=F32)
        mix += jnp.dot(zf_ref[...], wo_ref[D_DIFF:, :], preferred_element_type=F32)
        x = x + _mod_row(mod_ref, 5) * mix
    h = _rmsnorm(x, g_ref[...]) * (1.0 + _mod_row(mod_ref, mod_base + 1)) + _mod_row(mod_ref, mod_base)
    h = h.astype(BF16)
    acc = None
    for start, width in FF_CHUNKS:
        gate = jnp.dot(h, wg_ref[:, start:start + width], preferred_element_type=F32)
        up = jnp.dot(h, wu_ref[:, start:start + width], preferred_element_type=F32)
        a = (gate * jax.nn.sigmoid(gate) * up).astype(BF16)
        part = jnp.dot(a, wd_ref[start:start + width, :], preferred_element_type=F32)
        acc = part if acc is None else acc + part
    x = x + (0.5 * _mod_row(mod_ref, mod_base + 2)) * acc
    if final_norm:
        x = _rmsnorm(x, gf_ref[...])
    if has_tail:
        @pl.when(in_tail)
        def _():
            tail_o_ref[...] = x

        @pl.when(jnp.logical_not(in_tail))
        def _():
            o_ref[...] = x
    else:
        o_ref[...] = x


def _ffn(x, mod3, mod_index, g, wg, wu, wd, *, tm, mod_base, tail=None, mix=None, tiles_per_seq=None, final_g=None):
    t = x.shape[0]
    n_main = t // tm
    n_tiles = n_main + (tail.shape[0] // tm if tail is not None else 0)
    tile = lambda w: pl.BlockSpec((tm, w), lambda i: (jnp.minimum(i, n_main - 1), 0))
    tail_tile = pl.BlockSpec((tm, D_MODEL), lambda i: (jnp.maximum(i - n_main, 0), 0))
    staged = _StagedWeights([wg, wu, wd] + ([mix[2]] if mix is not None else []))
    in_specs = [tile(D_MODEL),
                pl.BlockSpec((1, 1, N_MOD * D_MODEL), lambda i: (mod_index(i), 0, 0)),
                _const_spec((1, D_MODEL))] + staged.in_specs
    args = [x, mod3, g] + staged.weights
    out_specs, out_shape = tile(D_MODEL), jax.ShapeDtypeStruct((t, D_MODEL), F32)
    if tail is not None:
        in_specs.append(tail_tile)
        args.append(tail)
        out_specs, out_shape = [out_specs, tail_tile], [out_shape, jax.ShapeDtypeStruct(tail.shape, F32)]
    if mix is not None:
        att, zf, _ = mix
        by_batch = pl.BlockSpec((tm, D_FOURIER), lambda i: (i % tiles_per_seq, i // tiles_per_seq))
        in_specs += [tile(D_DIFF), by_batch]
        args += [att, zf]
    if final_g is not None:
        in_specs.append(_const_spec((1, D_MODEL)))
        args.append(final_g)
    kern = functools.partial(_ffn_kernel, staged=staged, mod_base=mod_base, has_mix=mix is not None,
                             final_norm=final_g is not None, n_main=n_main if tail is not None else None)
    return pl.pallas_call(
        kern,
        grid=(n_tiles,),
        in_specs=in_specs,
        out_specs=out_specs,
        out_shape=out_shape,
        scratch_shapes=staged.scratch,
        compiler_params=_params(("arbitrary",)),
        name="ffn_mix" if mix is not None else "ffn",
    )(*args)


def _rope(x, cos, sin_lo, sin_hi):
    return (x * cos + pltpu.roll(x, HEAD_W - ROPE_PAIRS, 1) * sin_lo
            + pltpu.roll(x, ROPE_PAIRS, 1) * sin_hi)


def _inproj_kernel(*refs, staged, latent):
    n_scratch = len(staged.scratch)
    x_ref, mod_ref, g_ref = refs[:3]
    (w_ref,) = staged.load(refs[3:4], refs[len(refs) - n_scratch:])
    refs = refs[:len(refs) - n_scratch]
    h = _rmsnorm(x_ref[...], g_ref[...]) * (1.0 + _mod_row(mod_ref, 4)) + _mod_row(mod_ref, 3)
    p = jnp.dot(h.astype(BF16), w_ref[...], preferred_element_type=F32)
    if latent:
        cos_ref, slo_ref, shi_ref, dft_ref, q_ref, k_ref, v_ref, yc_ref, ys_ref = refs[4:]
        cos, slo, shi = cos_ref[...], slo_ref[...], shi_ref[...]
        for hd in range(N_HEADS):
            lo = hd * HEAD_W
            q_ref[:, lo:lo + HEAD_W] = (_rope(p[:, lo:lo + HEAD_W], cos, slo, shi) * Q_SCALE).astype(BF16)
            k_ref[:, lo:lo + HEAD_W] = _rope(p[:, D_DIFF + lo:D_DIFF + lo + HEAD_W], cos, slo, shi).astype(BF16)
        y = jnp.dot(p[:, 3 * D_DIFF:].astype(BF16), dft_ref[...], preferred_element_type=F32)
        yc_ref[...] = y[:, :D_FOURIER].astype(BF16)
        ys_ref[...] = y[:, D_FOURIER:].astype(BF16)
    else:
        k_ref, v_ref = refs[4:]
        k_ref[...] = p[:, D_DIFF:2 * D_DIFF].astype(BF16)
    v_ref[...] = p[:, 2 * D_DIFF:3 * D_DIFF].astype(BF16)


def _inproj(x, mod3, mod_index, g, w_in, *, tm, rope=None, chan_dft=None):
    t = x.shape[0]
    latent = rope is not None
    tile = lambda w: pl.BlockSpec((tm, w), lambda i: (i, 0))
    staged = _StagedWeights([w_in])
    in_specs = [tile(D_MODEL),
                pl.BlockSpec((1, 1, N_MOD * D_MODEL), lambda i: (mod_index(i), 0, 0)),
                _const_spec((1, D_MODEL))] + staged.in_specs
    args = [x, mod3, g, w_in]
    wide = jax.ShapeDtypeStruct((t, D_DIFF), BF16)
    if latent:
        tiles_per_seq = rope[0].shape[0] // tm
        rope_spec = pl.BlockSpec((tm, HEAD_W), lambda i: (i % tiles_per_seq, 0))
        in_specs += [rope_spec, rope_spec, rope_spec, _const_spec((D_FOURIER, 2 * D_FOURIER))]
        args += [*rope, chan_dft]
        narrow = jax.ShapeDtypeStruct((tiles_per_seq * tm, t // (tiles_per_seq * tm) * D_FOURIER), BF16)
        by_batch = pl.BlockSpec((tm, D_FOURIER), lambda i: (i % tiles_per_seq, i // tiles_per_seq))
        out_shape = [wide, wide, wide, narrow, narrow]
        out_specs = [tile(D_DIFF)] * 3 + [by_batch] * 2
    else:
        out_shape = [wide, wide]
        out_specs = [tile(D_DIFF)] * 2
    return pl.pallas_call(
        functools.partial(_inproj_kernel, staged=staged, latent=latent),
        grid=(t // tm,),
        in_specs=in_specs,
        out_specs=out_specs,
        out_shape=out_shape,
        scratch_shapes=staged.scratch,
        compiler_params=_params(("arbitrary",)),
        name="inproj" if latent else "inproj_ctx",
    )(*args)


def _max_key_norm_sq(kk):
    sq = (kk.astype(F32) ** 2).astype(BF16)
    r = lax.broadcasted_iota(jnp.int32, (HEAD_W, HEAD_W), 0)
    c = lax.broadcasted_iota(jnp.int32, (HEAD_W, HEAD_W), 1)
    sel = jnp.where((c == 0) == (r < HEAD_DIM), 1.0, 0.0) * jnp.where(c < 2, 1.0, 0.0)
    sums = jnp.dot(sq, sel.astype(BF16), preferred_element_type=F32)
    biggest = jnp.max(sums, axis=0, keepdims=True) * (1.0 + 2.0 ** -7)
    return biggest[:, 0:1], biggest[:, 1:2]


def _attn_kernel(lam_ref, g_ref, q_ref, k_ref, kc_ref, v_ref, vc_ref, o_ref, vt_ref, vct_ref, kn_ref, *, tk):
    tq = q_ref.shape[1]

    @pl.when(pl.program_id(2) == 0)
    def _():
        vt_ref[...] = v_ref[0].T
        vct_ref[...] = vc_ref[0].T
        (a1, a2), (b1, b2) = _max_key_norm_sq(k_ref[0]), _max_key_norm_sq(kc_ref[0])
        kn_ref[...] = jnp.sqrt(jnp.concatenate([jnp.broadcast_to(jnp.maximum(a1, b1), (1, tq)),
                                                 jnp.broadcast_to(jnp.maximum(a2, b2), (1, tq))], axis=1))

    lv = lam_ref[...]
    lam = (jnp.exp(jnp.sum(lv[0:1] * lv[1:2], axis=1, keepdims=True))
           - jnp.exp(jnp.sum(lv[2:3] * lv[3:4], axis=1, keepdims=True)) + LAMBDA_INIT)

    qt = q_ref[0].T
    none = jnp.zeros((HEAD_DIM, tq), BF16)
    qcat = jnp.concatenate([jnp.concatenate([qt[:HEAD_DIM], none], axis=0),
                            jnp.concatenate([none, qt[HEAD_DIM:]], axis=0)], axis=1)
    chunks = [(k_ref[0, c * tk:(c + 1) * tk, :], vt_ref[:, c * tk:(c + 1) * tk]) for c in range(k_ref.shape[1] // tk)]
    chunks.append((kc_ref[0], vct_ref[...]))

    def scores(kk):
        return jnp.dot(kk, qcat, preferred_element_type=F32)

    def softmax_sums(shift):
        lsum, acc = 0.0, 0.0
        for kk, vt in chunks:
            e = jnp.exp2(scores(kk) - shift)
            lsum += jnp.sum(e.reshape(-1, 8, 2 * tq), axis=0)
            acc += jnp.dot(vt, e.astype(BF16), preferred_element_type=F32)
        return jnp.sum(lsum, axis=0, keepdims=True), acc

    def finish(l, acc):
        ot = acc[:, :tq] * (1.0 / l[:, :tq]) - acc[:, tq:] * (lam / l[:, tq:])
        scale = lax.rsqrt(jnp.mean(ot * ot, axis=0, keepdims=True) + RMS_EPS)
        o_ref[0] = (ot * scale * (g_ref[...] * (1.0 - LAMBDA_INIT))).astype(BF16).T

    sq = qt.astype(F32) ** 2
    qn = jnp.sqrt(jnp.concatenate([jnp.sum(sq[:HEAD_DIM], axis=0, keepdims=True),
                                   jnp.sum(sq[HEAD_DIM:], axis=0, keepdims=True)], axis=1))
    l, acc = softmax_sums(qn * kn_ref[...])
    finish(l, acc)

    @pl.when(jnp.min(l) < UNDERFLOW_GUARD)
    def _():
        m = None
        for kk, _ in chunks:
            cm = jnp.max(scores(kk), axis=0, keepdims=True)
            m = cm if m is None else jnp.maximum(m, cm)
        finish(*softmax_sums(m))


def _attention(lamvec, subln_g, q, k, kc, v, vc, *, tq, tk):
    b, n, _ = q.shape
    c = kc.shape[1]
    head_block = lambda rows: pl.BlockSpec((1, rows, HEAD_W), lambda bi, hi, qi: (bi, 0, hi))
    return pl.pallas_call(
        functools.partial(_attn_kernel, tk=tk),
        grid=(b, N_HEADS, n // tq),
        in_specs=[_const_spec(lamvec.shape), _const_spec((HEAD_W, 1)),
                  pl.BlockSpec((1, tq, HEAD_W), lambda bi, hi, qi: (bi, qi, hi)),
                  head_block(n), head_block(c), head_block(n), head_block(c)],
        out_specs=pl.BlockSpec((1, tq, HEAD_W), lambda bi, hi, qi: (bi, qi, hi)),
        out_shape=jax.ShapeDtypeStruct((b, n, D_DIFF), BF16),
        scratch_shapes=[pltpu.VMEM((HEAD_W, n), BF16), pltpu.VMEM((HEAD_W, c), BF16),
                        pltpu.VMEM((1, 2 * tq), F32)],
        compiler_params=_params(("parallel", "parallel", "arbitrary")),
        name="diff_attn",
    )(lamvec, subln_g, q, k, kc, v, vc)


def _dft_kernel(yc_ref, ys_ref, m1_ref, twr_ref, twi_ref, w2_ref, wf_ref, perm_ref, o_ref, br_ref, bi_ref):
    n1, g, cols = yc_ref.shape
    n2 = br_ref.shape[0] // n1
    n_groups = n2 // g
    i = pl.program_id(0)

    @pl.when(i < n_groups)
    def _():
        u = jnp.concatenate([yc_ref[...].reshape(n1 * g, cols), ys_ref[...].reshape(n1 * g, cols)], axis=0)
        a = jnp.dot(m1_ref[...], u, preferred_element_type=F32)
        ar, ai = a[:n1 * g], a[n1 * g:]
        tr, ti = twr_ref[i], twi_ref[i]
        br = (ar * tr - ai * ti).astype(BF16)
        bi = (ar * ti + ai * tr).astype(BF16)
        for c in range(n1):
            dst = pl.ds(pl.multiple_of(c * n2 + i * g, g), g)
            br_ref[dst, :] = br[c * g:(c + 1) * g]
            bi_ref[dst, :] = bi[c * g:(c + 1) * g]

    @pl.when((i >= n_groups) & (i < n_groups + n1))
    def _():
        src = pl.ds(pl.multiple_of((i - n_groups) * n2, n2), n2)
        b = jnp.concatenate([br_ref[src, :], bi_ref[src, :]], axis=0)
        x = jnp.dot(w2_ref[...], b, preferred_element_type=F32)
        w = wf_ref.shape[0]
        wf = wf_ref[...].astype(BF16)
        br_ref[src, :] = jnp.concatenate(
            [jnp.dot(x[:, lo:lo + w].astype(BF16), wf, preferred_element_type=F32).astype(BF16)
             for lo in range(0, cols, w)], axis=1)

    @pl.when(i >= n_groups + n1)
    def _():
        run = o_ref.shape[0] // n1
        first = (i - n_groups - n1) * run
        rows = jnp.concatenate([br_ref[pl.ds(pl.multiple_of(c * n2 + first, run), run), :] for c in range(n1)], axis=0)
        o_ref[...] = jnp.dot(perm_ref[...], rows, preferred_element_type=F32).astype(BF16)


def _position_dft(yc, ys, wf, *, tile_rows):
    n, cols = yc.shape
    n1, g = DFT_N1, BF16_SUBLANES
    n2 = n // n1
    n_groups = n2 // g
    n_tiles = n // tile_rows
    m1, twr, twi, w2 = _position_dft_tables(n, n1, g)
    perm = _row_permutation(n1, tile_rows)
    group = pl.BlockSpec((n1, g, cols), lambda i: (0, jnp.minimum(i, n_groups - 1), 0))
    return pl.pallas_call(
        _dft_kernel,
        grid=(n_groups + n1 + n_tiles,),
        in_specs=[group, group, _const_spec(m1.shape), _const_spec(twr.shape), _const_spec(twi.shape),
                  _const_spec(w2.shape), _const_spec(wf.shape), _const_spec(perm.shape)],
        out_specs=pl.BlockSpec((tile_rows, cols), lambda i: (jnp.maximum(i - n_groups - n1, 0), 0)),
        out_shape=jax.ShapeDtypeStruct((n, cols), BF16),
        scratch_shapes=[pltpu.VMEM((n, cols), BF16), pltpu.VMEM((n, cols), BF16)],
        compiler_params=_params(("arbitrary",)),
        name="position_dft",
    )(yc.reshape(n1, n2, cols), ys.reshape(n1, n2, cols), m1, twr, twi, w2, wf, perm)


def _rope_tables(n):
    rows = n // GRID_W
    lane = np.arange(HEAD_W)
    sub = lane % HEAD_DIM
    on_row_axis = jnp.asarray(sub // (2 * ROPE_PAIRS) == 0)[None, :]
    second_half = jnp.asarray((sub % (2 * ROPE_PAIRS)) // ROPE_PAIRS == 1)[None, :]
    inv_freq = ROPE_BASE ** (-jnp.asarray(sub % ROPE_PAIRS, dtype=F32) / ROPE_PAIRS)

    def table(fn, keep):
        def side(size, mine):
            t = fn(jnp.arange(size, dtype=F32)[:, None] * inv_freq[None, :])
            return jnp.where(mine & keep, t, 0.0)
        return (side(rows, on_row_axis)[:, None, :] + side(GRID_W, ~on_row_axis)[None, :, :]).reshape(n, HEAD_W)

    return (table(jnp.cos, True), table(lambda a: -jnp.sin(a), ~second_half), table(jnp.sin, second_half))


def _channel_dft_table():
    c = np.arange(FOURIER_GROUP_DIM)
    ang = 2.0 * np.pi * np.outer(c, c) / FOURIER_GROUP_DIM
    eye = np.eye(D_FOURIER // FOURIER_GROUP_DIM)
    scale = FOURIER_GROUP_DIM ** -0.5
    return np.concatenate([np.kron(eye, np.cos(ang)), np.kron(eye, np.sin(ang))], axis=1) * scale


def _position_dft_tables(n, n1, g):
    n2 = n // n1
    w1 = np.exp(-2j * np.pi * np.outer(np.arange(n1), np.arange(n1)) / n1)
    m = np.kron(w1, np.eye(g)) * n ** -0.5
    m1 = np.block([[m.real, m.imag], [m.imag, -m.real]])
    b = (g * np.arange(n2 // g)[:, None, None] + np.arange(g)[None, None, :])
    tw = np.exp(-2j * np.pi * b * np.arange(n1)[None, :, None] / n).reshape(n2 // g, n1 * g, 1)
    ang2 = 2.0 * np.pi * np.outer(np.arange(n2), np.arange(n2)) / n2
    w2 = np.concatenate([np.cos(ang2), np.sin(ang2)], axis=1)
    f32 = lambda t: jnp.asarray(t, dtype=F32)
    return f32(m1).astype(BF16), f32(tw.real), f32(tw.imag), f32(w2).astype(BF16)


def _row_permutation(n1, rows):
    d_per_tile = rows // n1
    p = np.zeros((rows, rows), np.float32)
    c, d = np.meshgrid(np.arange(n1), np.arange(d_per_tile), indexing="ij")
    p[(c + n1 * d).ravel(), (c * d_per_tile + d).ravel()] = 1.0
    return jnp.asarray(p).astype(BF16)


def kernel(x, c, ctx, c_ctx, w_ada, b_ada, norm1_g, ffn1_w_gate, ffn1_w_up, ffn1_w_down, norm_mix_g, w_in,
           lambda_q1, lambda_k1, lambda_q2, lambda_k2, subln_g, w_fourier, w_out, norm2_g, ffn2_w_gate,
           ffn2_w_up, ffn2_w_down, final_norm_g):
    b, n, d = x.shape
    n_ctx = ctx.shape[1]
    assert (d, w_ada.shape[0]) == (D_MODEL, 1) and b + 1 <= MOD_ROWS
    tm = 512
    tiles_per_seq = n // tm
    mod_row = lambda i: jnp.minimum(i // tiles_per_seq, b)
    row = lambda g: g.reshape(1, -1)

    cc = jnp.zeros((MOD_ROWS, d), F32).at[:b].set(c).at[b].set(c_ctx)
    mod3 = _ada(cc, w_ada[0], b_ada).reshape(MOD_ROWS, 1, N_MOD * d)

    x1, c1 = _ffn(x.reshape(b * n, d), mod3, mod_row, row(norm1_g), ffn1_w_gate[0], ffn1_w_up[0], ffn1_w_down[0],
                  tm=tm, mod_base=0, tail=ctx.reshape(b * n_ctx, d))

    chan_dft = jnp.asarray(_channel_dft_table(), dtype=F32).astype(BF16)
    q, k, v, yc, ys = _inproj(x1, mod3, mod_row, row(norm_mix_g), w_in[0], tm=tm,
                              rope=_rope_tables(n), chan_dft=chan_dft)
    kc, vc = _inproj(c1, mod3, lambda i: b, row(norm_mix_g), w_in[0], tm=tm)

    lamvec = jnp.concatenate([lambda_q1, lambda_k1, lambda_q2, lambda_k2], axis=0)
    seq = lambda a, rows: a.reshape(b, rows, a.shape[-1])
    att = _attention(lamvec, subln_g.reshape(HEAD_W, 1), seq(q, n), seq(k, n), seq(kc, n_ctx), seq(v, n), seq(vc, n_ctx),
                     tq=1024, tk=1024)

    zf = _position_dft(yc, ys, w_fourier[0], tile_rows=tm)

    out = _ffn(x1, mod3, mod_row, row(norm2_g), ffn2_w_gate[0], ffn2_w_up[0], ffn2_w_down[0],
               tm=tm, mod_base=6, mix=(att.reshape(b * n, D_DIFF), zf, w_out[0]), tiles_per_seq=tiles_per_seq,
               final_g=row(final_norm_g))
    return out.reshape(b, n, d)
```

```python
import functools
import math

import numpy as np
import jax
import jax.numpy as jnp
from jax import lax
from jax.experimental import pallas as pl
from jax.experimental.pallas import tpu as pltpu

D_MODEL = 1024
GRID_W = 64
D_FOURIER = 256
D_DIFF = 768
HEAD_DIM = 64
HEAD_W = 2 * HEAD_DIM
N_HEADS = D_DIFF // HEAD_W
FOURIER_GROUP_DIM = 64
D_IN_PROJ = 3 * D_DIFF + D_FOURIER
D_FF = 2816
N_MOD = 9
ROPE_BASE = 10000.0
ROPE_PAIRS = HEAD_DIM // 4
RMS_EPS = 1e-6
ATTN_SCALE = HEAD_DIM ** -0.5
Q_SCALE = ATTN_SCALE * math.log2(math.e)
LAMBDA_INIT = 0.8 - 0.6 * math.exp(-0.3 * 0)
UNDERFLOW_GUARD = 2.0 ** -80

F32 = jnp.float32
BF16 = jnp.bfloat16

V7X_VMEM_LIMIT_BYTES = 56 * 1024 * 1024
BF16_SUBLANES = 16
STAGE_CHUNK_BYTES = 3 * 1024 * 1024
MOD_ROWS = 8
DFT_N1 = 16
FF_CHUNKS = ((0, 1536), (1536, 1280))


def _const_spec(shape):
    return pl.BlockSpec(shape, lambda *_: (0,) * len(shape), pipeline_mode=pl.Buffered(1))


def _params(semantics):
    return pltpu.CompilerParams(dimension_semantics=semantics, vmem_limit_bytes=V7X_VMEM_LIMIT_BYTES)


def _rmsnorm(x, g):
    return x * lax.rsqrt(jnp.mean(x * x, axis=-1, keepdims=True) + RMS_EPS) * g


def _mod_row(mod_ref, k):
    return mod_ref[0, :, k * D_MODEL:(k + 1) * D_MODEL]


def _ada_kernel(cc_ref, w_ref, b_ref, o_ref):
    cc = cc_ref[...]
    s = cc * jax.nn.sigmoid(cc)
    o_ref[...] = jnp.dot(s.astype(BF16), w_ref[...].astype(BF16), preferred_element_type=F32) + b_ref[...]


def _ada(cc, w_ada, b_ada):
    n_out = w_ada.shape[1]
    tn = n_out // 4
    return pl.pallas_call(
        _ada_kernel,
        grid=(n_out // tn,),
        in_specs=[_const_spec((MOD_ROWS, D_MODEL)),
                  pl.BlockSpec((D_MODEL, tn), lambda j: (0, j)),
                  pl.BlockSpec((1, tn), lambda j: (0, j))],
        out_specs=pl.BlockSpec((MOD_ROWS, tn), lambda j: (0, j)),
        out_shape=jax.ShapeDtypeStruct((MOD_ROWS, n_out), F32),
        compiler_params=_params(("arbitrary",)),
        name="ada",
    )(cc, w_ada, b_ada)


def _stage_rows(k, n):
    fits = [r for r in range(BF16_SUBLANES, k + 1, BF16_SUBLANES) if k % r == 0 and r * n * 4 <= STAGE_CHUNK_BYTES]
    return max(fits)


def _stage_weight(src_hbm, dst_ref, stage_ref, sem_ref):
    rows = stage_ref.shape[1]
    n_chunks = src_hbm.shape[0] // rows

    def copy(c):
        return pltpu.make_async_copy(src_hbm.at[pl.ds(c * rows, rows)], stage_ref.at[c % 2], sem_ref.at[c % 2])

    copy(0).start()
    for c in range(n_chunks):
        if c + 1 < n_chunks:
            copy(c + 1).start()
        copy(c).wait()
        dst_ref[c * rows:(c + 1) * rows, :] = stage_ref[c % 2].astype(BF16)


class _StagedWeights:
    def __init__(self, weights):
        self.weights = list(weights)
        self.in_specs = [pl.BlockSpec(memory_space=pl.ANY)] * len(self.weights)
        chunk_shapes = sorted({(_stage_rows(*w.shape), w.shape[1]) for w in self.weights})
        self.stage_of = [chunk_shapes.index((_stage_rows(*w.shape), w.shape[1])) for w in self.weights]
        self.scratch = ([pltpu.VMEM(w.shape, BF16) for w in self.weights]
                        + [pltpu.VMEM((2,) + s, F32) for s in chunk_shapes]
                        + [pltpu.SemaphoreType.DMA((2,))])

    def load(self, hbm_refs, scratch_refs):
        n = len(self.weights)
        dst, stages, sem = scratch_refs[:n], scratch_refs[n:-1], scratch_refs[-1]

        @pl.when(pl.program_id(0) == 0)
        def _():
            for src, d, s in zip(hbm_refs, dst, self.stage_of):
                _stage_weight(src, d, stages[s], sem)

        return dst


def _ffn_kernel(*refs, staged, mod_base, has_mix, final_norm, n_main):
    n_w = len(staged.weights)
    n_scratch = len(staged.scratch)
    x_ref, mod_ref, g_ref = refs[:3]
    w_hbm = refs[3:3 + n_w]
    rest = list(refs[3 + n_w:len(refs) - n_scratch])
    weights = staged.load(w_hbm, refs[len(refs) - n_scratch:])
    wg_ref, wu_ref, wd_ref = weights[:3]
    has_tail = n_main is not None
    if has_tail:
        tail_ref = rest[0]
        tail_o_ref = rest[-1]
        rest = rest[1:-1]
        in_tail = pl.program_id(0) >= n_main
    if has_mix:
        att_ref, zf_ref = rest[:2]
        rest = rest[2:]
        wo_ref = weights[3]
    if final_norm:
        gf_ref = rest[0]
        rest = rest[1:]
    (o_ref,) = rest

    x = x_ref[...]
    if has_tail:
        x = jnp.where(in_tail, tail_ref[...], x)
    if has_mix:
        mix = jnp.dot(att_ref[...], wo_ref[:D_DIFF, :], preferred_element_type=F32)
        mix += jnp.dot(zf_ref[...], wo_ref[D_DIFF:, :], preferred_element_type=F32)
        x = x + _mod_row(mod_ref, 5) * mix
    h = _rmsnorm(x, g_ref[...]) * (1.0 + _mod_row(mod_ref, mod_base + 1)) + _mod_row(mod_ref, mod_base)
    h = h.astype(BF16)
    acc = None
    for start, width in FF_CHUNKS:
        gate = jnp.dot(h, wg_ref[:, start:start + width], preferred_element_type=F32)
        up = jnp.dot(h, wu_ref[:, start:start + width], preferred_element_type=F32)
        a = (gate * jax.nn.sigmoid(gate) * up).astype(BF16)
        part = jnp.dot(a, wd_ref[start:start + width, :], preferred_element_type=F32)
        acc = part if acc is None else acc + part
    x = x + (0.5 * _mod_row(mod_ref, mod_base + 2)) * acc
    if final_norm:
        x = _rmsnorm(x, gf_ref[...])
    if has_tail:
        @pl.when(in_tail)
        def _():
            tail_o_ref[...] = x

        @pl.when(jnp.logical_not(in_tail))
        def _():
            o_ref[...] = x
    else:
        o_ref[...] = x


def _ffn(x, mod3, mod_index, g, wg, wu, wd, *, tm, mod_base, tail=None, mix=None, tiles_per_seq=None, final_g=None):
    t = x.shape[0]
    n_main = t // tm
    n_tiles = n_main + (tail.shape[0] // tm if tail is not None else 0)
    tile = lambda w: pl.BlockSpec((tm, w), lambda i: (jnp.minimum(i, n_main - 1), 0))
    tail_tile = pl.BlockSpec((tm, D_MODEL), lambda i: (jnp.maximum(i - n_main, 0), 0))
    staged = _StagedWeights([wg, wu, wd] + ([mix[2]] if mix is not None else []))
    in_specs = [tile(D_MODEL),
                pl.BlockSpec((1, 1, N_MOD * D_MODEL), lambda i: (mod_index(i), 0, 0)),
                _const_spec((1, D_MODEL))] + staged.in_specs
    args = [x, mod3, g] + staged.weights
    out_specs, out_shape = tile(D_MODEL), jax.ShapeDtypeStruct((t, D_MODEL), F32)
    if tail is not None:
        in_specs.append(tail_tile)
        args.append(tail)
        out_specs, out_shape = [out_specs, tail_tile], [out_shape, jax.ShapeDtypeStruct(tail.shape, F32)]
    if mix is not None:
        att, zf, _ = mix
        by_batch = pl.BlockSpec((tm, D_FOURIER), lambda i: (i % tiles_per_seq, i // tiles_per_seq))
        in_specs += [tile(D_DIFF), by_batch]
        args += [att, zf]
    if final_g is not None:
        in_specs.append(_const_spec((1, D_MODEL)))
        args.append(final_g)
    kern = functools.partial(_ffn_kernel, staged=staged, mod_base=mod_base, has_mix=mix is not None,
                             final_norm=final_g is not None, n_main=n_main if tail is not None else None)
    return pl.pallas_call(
        kern,
        grid=(n_tiles,),
        in_specs=in_specs,
        out_specs=out_specs,
        out_shape=out_shape,
        scratch_shapes=staged.scratch,
        compiler_params=_params(("arbitrary",)),
        name="ffn_mix" if mix is not None else "ffn",
    )(*args)


def _rope(x, cos, sin_lo, sin_hi):
    return (x * cos + pltpu.roll(x, HEAD_W - ROPE_PAIRS, 1) * sin_lo
            + pltpu.roll(x, ROPE_PAIRS, 1) * sin_hi)


def _inproj_kernel(*refs, staged, latent):
    n_scratch = len(staged.scratch)
    x_ref, mod_ref, g_ref = refs[:3]
    (w_ref,) = staged.load(refs[3:4], refs[len(refs) - n_scratch:])
    refs = refs[:len(refs) - n_scratch]
    h = _rmsnorm(x_ref[...], g_ref[...]) * (1.0 + _mod_row(mod_ref, 4)) + _mod_row(mod_ref, 3)
    p = jnp.dot(h.astype(BF16), w_ref[...], preferred_element_type=F32)
    if latent:
        cos_ref, slo_ref, shi_ref, dft_ref, q_ref, k_ref, v_ref, yc_ref, ys_ref = refs[4:]
        cos, slo, shi = cos_ref[...], slo_ref[...], shi_ref[...]
        for hd in range(N_HEADS):
            lo = hd * HEAD_W
            q_ref[:, lo:lo + HEAD_W] = (_rope(p[:, lo:lo + HEAD_W], cos, slo, shi) * Q_SCALE).astype(BF16)
            k_ref[:, lo:lo + HEAD_W] = _rope(p[:, D_DIFF + lo:D_DIFF + lo + HEAD_W], cos, slo, shi).astype(BF16)
        y = jnp.dot(p[:, 3 * D_DIFF:].astype(BF16), dft_ref[...], preferred_element_type=F32)
        yc_ref[...] = y[:, :D_FOURIER].astype(BF16)
        ys_ref[...] = y[:, D_FOURIER:].astype(BF16)
    else:
        k_ref, v_ref = refs[4:]
        k_ref[...] = p[:, D_DIFF:2 * D_DIFF].astype(BF16)
    v_ref[...] = p[:, 2 * D_DIFF:3 * D_DIFF].astype(BF16)


def _inproj(x, mod3, mod_index, g, w_in, *, tm, rope=None, chan_dft=None):
    t = x.shape[0]
    latent = rope is not None
    tile = lambda w: pl.BlockSpec((tm, w), lambda i: (i, 0))
    staged = _StagedWeights([w_in])
    in_specs = [tile(D_MODEL),
                pl.BlockSpec((1, 1, N_MOD * D_MODEL), lambda i: (mod_index(i), 0, 0)),
                _const_spec((1, D_MODEL))] + staged.in_specs
    args = [x, mod3, g, w_in]
    wide = jax.ShapeDtypeStruct((t, D_DIFF), BF16)
    if latent:
        tiles_per_seq = rope[0].shape[0] // tm
        rope_spec = pl.BlockSpec((tm, HEAD_W), lambda i: (i % tiles_per_seq, 0))
        in_specs += [rope_spec, rope_spec, rope_spec, _const_spec((D_FOURIER, 2 * D_FOURIER))]
        args += [*rope, chan_dft]
        narrow = jax.ShapeDtypeStruct((tiles_per_seq * tm, t // (tiles_per_seq * tm) * D_FOURIER), BF16)
        by_batch = pl.BlockSpec((tm, D_FOURIER), lambda i: (i % tiles_per_seq, i // tiles_per_seq))
        out_shape = [wide, wide, wide, narrow, narrow]
        out_specs = [tile(D_DIFF)] * 3 + [by_batch] * 2
    else:
        out_shape = [wide, wide]
        out_specs = [tile(D_DIFF)] * 2
    return pl.pallas_call(
        functools.partial(_inproj_kernel, staged=staged, latent=latent),
        grid=(t // tm,),
        in_specs=in_specs,
        out_specs=out_specs,
        out_shape=out_shape,
        scratch_shapes=staged.scratch,
        compiler_params=_params(("arbitrary",)),
        name="inproj" if latent else "inproj_ctx",
    )(*args)


def _max_key_norm_sq(kk):
    sq = (kk.astype(F32) ** 2).astype(BF16)
    r = lax.broadcasted_iota(jnp.int32, (HEAD_W, HEAD_W), 0)
    c = lax.broadcasted_iota(jnp.int32, (HEAD_W, HEAD_W), 1)
    sel = jnp.where((c == 0) == (r < HEAD_DIM), 1.0, 0.0) * jnp.where(c < 2, 1.0, 0.0)
    sums = jnp.dot(sq, sel.astype(BF16), preferred_element_type=F32)
    biggest = jnp.max(sums, axis=0, keepdims=True) * (1.0 + 2.0 ** -7)
    return biggest[:, 0:1], biggest[:, 1:2]


def _attn_kernel(lam_ref, g_ref, q_ref, k_ref, kc_ref, v_ref, vc_ref, o_ref, vt_ref, vct_ref, kn_ref,
                 l_ref, acc_ref, *, tk, blocks_per_head, n_blocks):
    tq = q_ref.shape[1]
    j = pl.program_id(0)

    lv = lam_ref[...]
    lam = (jnp.exp(jnp.sum(lv[0:1] * lv[1:2], axis=1, keepdims=True))
           - jnp.exp(jnp.sum(lv[2:3] * lv[3:4], axis=1, keepdims=True)) + LAMBDA_INIT)

    def write_previous_block():
        l, acc = l_ref[...], acc_ref[...]
        ot = acc[:, :tq] * (1.0 / l[:, :tq]) - acc[:, tq:] * (lam / l[:, tq:])
        scale = lax.rsqrt(jnp.mean(ot * ot, axis=0, keepdims=True) + RMS_EPS)
        o_ref[0] = (ot * scale * (g_ref[...] * (1.0 - LAMBDA_INIT))).astype(BF16).T

    @pl.when(j == 0)
    def _():
        l_ref[...] = jnp.ones_like(l_ref)
        acc_ref[...] = jnp.zeros_like(acc_ref)

    @pl.when(j == n_blocks)
    def _():
        write_previous_block()

    @pl.when((j % blocks_per_head == 0) & (j < n_blocks))
    def _():
        vt_ref[...] = v_ref[0].T
        vct_ref[...] = vc_ref[0].T
        (a1, a2), (b1, b2) = _max_key_norm_sq(k_ref[0]), _max_key_norm_sq(kc_ref[0])
        kn_ref[...] = jnp.sqrt(jnp.concatenate([jnp.broadcast_to(jnp.maximum(a1, b1), (1, tq)),
                                                 jnp.broadcast_to(jnp.maximum(a2, b2), (1, tq))], axis=1))

    pl.when(j < n_blocks)(functools.partial(_attn_block, q_ref, k_ref, kc_ref, vt_ref, vct_ref, kn_ref,
                                            l_ref, acc_ref, write_previous_block, tk))


def _attn_block(q_ref, k_ref, kc_ref, vt_ref, vct_ref, kn_ref, l_ref, acc_ref, write_previous_block, tk):
    tq = q_ref.shape[1]
    qt = q_ref[0].T
    none = jnp.zeros((HEAD_DIM, tq), BF16)
    qcat = jnp.concatenate([jnp.concatenate([qt[:HEAD_DIM], none], axis=0),
                            jnp.concatenate([none, qt[HEAD_DIM:]], axis=0)], axis=1)
    chunks = [(k_ref[0, c * tk:(c + 1) * tk, :], vt_ref[:, c * tk:(c + 1) * tk]) for c in range(k_ref.shape[1] // tk)]
    chunks.append((kc_ref[0], vct_ref[...]))

    def scores(kk):
        return jnp.dot(kk, qcat, preferred_element_type=F32)

    def softmax_sums(shift):
        lsum, acc = 0.0, 0.0
        for kk, vt in chunks:
            e = jnp.exp2(scores(kk) - shift)
            lsum += jnp.sum(e.reshape(-1, 8, 2 * tq), axis=0)
            acc += jnp.dot(vt, e.astype(BF16), preferred_element_type=F32)
        return jnp.sum(lsum, axis=0, keepdims=True), acc

    def park(l, acc):
        l_ref[...] = l
        acc_ref[...] = acc

    sq = qt.astype(F32) ** 2
    qn = jnp.sqrt(jnp.concatenate([jnp.sum(sq[:HEAD_DIM], axis=0, keepdims=True),
                                   jnp.sum(sq[HEAD_DIM:], axis=0, keepdims=True)], axis=1))
    l, acc = softmax_sums(qn * kn_ref[...])
    write_previous_block()
    park(l, acc)

    @pl.when(jnp.min(l) < UNDERFLOW_GUARD)
    def _():
        m = None
        for kk, _ in chunks:
            cm = jnp.max(scores(kk), axis=0, keepdims=True)
            m = cm if m is None else jnp.maximum(m, cm)
        park(*softmax_sums(m))


def _attention(lamvec, subln_g, q, k, kc, v, vc, *, tq, tk):
    b, n, _ = q.shape
    c = kc.shape[1]
    blocks_per_head = n // tq
    n_blocks = b * N_HEADS * blocks_per_head

    def block_of(j):
        return j // (N_HEADS * blocks_per_head), j % blocks_per_head, (j // blocks_per_head) % N_HEADS

    def computing(index):
        return lambda j: index(*block_of(jnp.minimum(j, n_blocks - 1)))

    def draining(index):
        return lambda j: index(*block_of(jnp.maximum(j - 1, 0)))

    head_block = lambda rows: pl.BlockSpec((1, rows, HEAD_W), computing(lambda bi, qi, hi: (bi, 0, hi)))
    query_block = lambda which: pl.BlockSpec((1, tq, HEAD_W), which(lambda bi, qi, hi: (bi, qi, hi)))
    return pl.pallas_call(
        functools.partial(_attn_kernel, tk=tk, blocks_per_head=blocks_per_head, n_blocks=n_blocks),
        grid=(n_blocks + 1,),
        in_specs=[_const_spec(lamvec.shape), _const_spec((HEAD_W, 1)), query_block(computing),
                  head_block(n), head_block(c), head_block(n), head_block(c)],
        out_specs=query_block(draining),
        out_shape=jax.ShapeDtypeStruct((b, n, D_DIFF), BF16),
        scratch_shapes=[pltpu.VMEM((HEAD_W, n), BF16), pltpu.VMEM((HEAD_W, c), BF16),
                        pltpu.VMEM((1, 2 * tq), F32), pltpu.VMEM((1, 2 * tq), F32),
                        pltpu.VMEM((HEAD_W, 2 * tq), F32)],
        compiler_params=_params(("arbitrary",)),
        name="diff_attn",
    )(lamvec, subln_g, q, k, kc, v, vc)


def _dft_kernel(yc_ref, ys_ref, m1_ref, twr_ref, twi_ref, w2_ref, wf_ref, perm_ref, o_ref, br_ref, bi_ref):
    n1, g, cols = yc_ref.shape
    n2 = br_ref.shape[0] // n1
    n_groups = n2 // g
    i = pl.program_id(0)

    @pl.when(i < n_groups)
    def _():
        u = jnp.concatenate([yc_ref[...].reshape(n1 * g, cols), ys_ref[...].reshape(n1 * g, cols)], axis=0)
        a = jnp.dot(m1_ref[...], u, preferred_element_type=F32)
        ar, ai = a[:n1 * g], a[n1 * g:]
        tr, ti = twr_ref[i], twi_ref[i]
        br = (ar * tr - ai * ti).astype(BF16)
        bi = (ar * ti + ai * tr).astype(BF16)
        for c in range(n1):
            dst = pl.ds(pl.multiple_of(c * n2 + i * g, g), g)
            br_ref[dst, :] = br[c * g:(c + 1) * g]
            bi_ref[dst, :] = bi[c * g:(c + 1) * g]

    @pl.when((i >= n_groups) & (i < n_groups + n1))
    def _():
        src = pl.ds(pl.multiple_of((i - n_groups) * n2, n2), n2)
        b = jnp.concatenate([br_ref[src, :], bi_ref[src, :]], axis=0)
        x = jnp.dot(w2_ref[...], b, preferred_element_type=F32)
        w = wf_ref.shape[0]
        wf = wf_ref[...].astype(BF16)
        br_ref[src, :] = jnp.concatenate(
            [jnp.dot(x[:, lo:lo + w].astype(BF16), wf, preferred_element_type=F32).astype(BF16)
             for lo in range(0, cols, w)], axis=1)

    @pl.when(i >= n_groups + n1)
    def _():
        run = o_ref.shape[0] // n1
        first = (i - n_groups - n1) * run
        rows = jnp.concatenate([br_ref[pl.ds(pl.multiple_of(c * n2 + first, run), run), :] for c in range(n1)], axis=0)
        o_ref[...] = jnp.dot(perm_ref[...], rows, preferred_element_type=F32).astype(BF16)


def _position_dft(yc, ys, wf, *, tile_rows):
    n, cols = yc.shape
    n1, g = DFT_N1, BF16_SUBLANES
    n2 = n // n1
    n_groups = n2 // g
    n_tiles = n // tile_rows
    m1, twr, twi, w2 = _position_dft_tables(n, n1, g)
    perm = _row_permutation(n1, tile_rows)
    group = pl.BlockSpec((n1, g, cols), lambda i: (0, jnp.minimum(i, n_groups - 1), 0))
    return pl.pallas_call(
        _dft_kernel,
        grid=(n_groups + n1 + n_tiles,),
        in_specs=[group, group, _const_spec(m1.shape), _const_spec(twr.shape), _const_spec(twi.shape),
                  _const_spec(w2.shape), _const_spec(wf.shape), _const_spec(perm.shape)],
        out_specs=pl.BlockSpec((tile_rows, cols), lambda i: (jnp.maximum(i - n_groups - n1, 0), 0)),
        out_shape=jax.ShapeDtypeStruct((n, cols), BF16),
        scratch_shapes=[pltpu.VMEM((n, cols), BF16), pltpu.VMEM((n, cols), BF16)],
        compiler_params=_params(("arbitrary",)),
        name="position_dft",
    )(yc.reshape(n1, n2, cols), ys.reshape(n1, n2, cols), m1, twr, twi, w2, wf, perm)


def _rope_tables(n):
    rows = n // GRID_W
    lane = np.arange(HEAD_W)
    sub = lane % HEAD_DIM
    on_row_axis = jnp.asarray(sub // (2 * ROPE_PAIRS) == 0)[None, :]
    second_half = jnp.asarray((sub % (2 * ROPE_PAIRS)) // ROPE_PAIRS == 1)[None, :]
    inv_freq = ROPE_BASE ** (-jnp.asarray(sub % ROPE_PAIRS, dtype=F32) / ROPE_PAIRS)

    def table(fn, keep):
        def side(size, mine):
            t = fn(jnp.arange(size, dtype=F32)[:, None] * inv_freq[None, :])
            return jnp.where(mine & keep, t, 0.0)
        return (side(rows, on_row_axis)[:, None, :] + side(GRID_W, ~on_row_axis)[None, :, :]).reshape(n, HEAD_W)

    return (table(jnp.cos, True), table(lambda a: -jnp.sin(a), ~second_half), table(jnp.sin, second_half))


def _channel_dft_table():
    c = np.arange(FOURIER_GROUP_DIM)
    ang = 2.0 * np.pi * np.outer(c, c) / FOURIER_GROUP_DIM
    eye = np.eye(D_FOURIER // FOURIER_GROUP_DIM)
    scale = FOURIER_GROUP_DIM ** -0.5
    return np.concatenate([np.kron(eye, np.cos(ang)), np.kron(eye, np.sin(ang))], axis=1) * scale


def _position_dft_tables(n, n1, g):
    n2 = n // n1
    w1 = np.exp(-2j * np.pi * np.outer(np.arange(n1), np.arange(n1)) / n1)
    m = np.kron(w1, np.eye(g)) * n ** -0.5
    m1 = np.block([[m.real, m.imag], [m.imag, -m.real]])
    b = (g * np.arange(n2 // g)[:, None, None] + np.arange(g)[None, None, :])
    tw = np.exp(-2j * np.pi * b * np.arange(n1)[None, :, None] / n).reshape(n2 // g, n1 * g, 1)
    ang2 = 2.0 * np.pi * np.outer(np.arange(n2), np.arange(n2)) / n2
    w2 = np.concatenate([np.cos(ang2), np.sin(ang2)], axis=1)
    f32 = lambda t: jnp.asarray(t, dtype=F32)
    return f32(m1).astype(BF16), f32(tw.real), f32(tw.imag), f32(w2).astype(BF16)


def _row_permutation(n1, rows):
    d_per_tile = rows // n1
    p = np.zeros((rows, rows), np.float32)
    c, d = np.meshgrid(np.arange(n1), np.arange(d_per_tile), indexing="ij")
    p[(c + n1 * d).ravel(), (c * d_per_tile + d).ravel()] = 1.0
    return jnp.asarray(p).astype(BF16)


def kernel(x, c, ctx, c_ctx, w_ada, b_ada, norm1_g, ffn1_w_gate, ffn1_w_up, ffn1_w_down, norm_mix_g, w_in,
           lambda_q1, lambda_k1, lambda_q2, lambda_k2, subln_g, w_fourier, w_out, norm2_g, ffn2_w_gate,
           ffn2_w_up, ffn2_w_down, final_norm_g):
    b, n, d = x.shape
    n_ctx = ctx.shape[1]
    assert (d, w_ada.shape[0]) == (D_MODEL, 1) and b + 1 <= MOD_ROWS
    tm = 512
    tiles_per_seq = n // tm
    mod_row = lambda i: jnp.minimum(i // tiles_per_seq, b)
    row = lambda g: g.reshape(1, -1)

    cc = jnp.zeros((MOD_ROWS, d), F32).at[:b].set(c).at[b].set(c_ctx)
    mod3 = _ada(cc, w_ada[0], b_ada).reshape(MOD_ROWS, 1, N_MOD * d)

    x1, c1 = _ffn(x.reshape(b * n, d), mod3, mod_row, row(norm1_g), ffn1_w_gate[0], ffn1_w_up[0], ffn1_w_down[0],
                  tm=tm, mod_base=0, tail=ctx.reshape(b * n_ctx, d))

    chan_dft = jnp.asarray(_channel_dft_table(), dtype=F32).astype(BF16)
    q, k, v, yc, ys = _inproj(x1, mod3, mod_row, row(norm_mix_g), w_in[0], tm=tm,
                              rope=_rope_tables(n), chan_dft=chan_dft)
    kc, vc = _inproj(c1, mod3, lambda i: b, row(norm_mix_g), w_in[0], tm=tm)

    lamvec = jnp.concatenate([lambda_q1, lambda_k1, lambda_q2, lambda_k2], axis=0)
    seq = lambda a, rows: a.reshape(b, rows, a.shape[-1])
    att = _attention(lamvec, subln_g.reshape(HEAD_W, 1), seq(q, n), seq(k, n), seq(kc, n_ctx), seq(v, n), seq(vc, n_ctx),
                     tq=1024, tk=1024)

    zf = _position_dft(yc, ys, w_fourier[0], tile_rows=tm)

    out = _ffn(x1, mod3, mod_row, row(norm2_g), ffn2_w_gate[0], ffn2_w_up[0], ffn2_w_down[0],
               tm=tm, mod_base=6, mix=(att.reshape(b * n, D_DIFF), zf, w_out[0]), tiles_per_seq=tiles_per_seq,
               final_g=row(final_norm_g))
    return out.reshape(b, n, d)
```

```python
import functools
import math
from typing import NamedTuple

import numpy as np
import jax
import jax.numpy as jnp
from jax import lax
from jax.experimental import pallas as pl
from jax.experimental.pallas import tpu as pltpu

D_MODEL = 1024
GRID_W = 64
D_FOURIER = 256
D_DIFF = 768
HEAD_DIM = 64
HEAD_W = 2 * HEAD_DIM
N_HEADS = D_DIFF // HEAD_W
FOURIER_GROUP_DIM = 64
D_IN_PROJ = 3 * D_DIFF + D_FOURIER
D_FF = 2816
N_MOD = 9
ROPE_BASE = 10000.0
ROPE_PAIRS = HEAD_DIM // 4
RMS_EPS = 1e-6
ATTN_SCALE = HEAD_DIM ** -0.5
Q_SCALE = ATTN_SCALE * math.log2(math.e)
LAMBDA_INIT = 0.8 - 0.6 * math.exp(-0.3 * 0)
UNDERFLOW_GUARD = 2.0 ** -80

F32 = jnp.float32
BF16 = jnp.bfloat16

V7X_VMEM_LIMIT_BYTES = 56 * 1024 * 1024
BF16_SUBLANES = 16
STAGE_CHUNK_BYTES = 3 * 1024 * 1024
MOD_ROWS = 8
DFT_N1 = 16
DFT_UNITS_PER_STEP = 2


class _Tiles(NamedTuple):
    ffn_rows: int = 512
    inproj_rows: int = 1024
    attn_queries: int = 1024
    attn_keys: int = 1024


TILES = _Tiles()
FF_CHUNKS = ((0, 1536), (1536, 1280))


def _const_spec(shape):
    return pl.BlockSpec(shape, lambda *_: (0,) * len(shape), pipeline_mode=pl.Buffered(1))


def _params(semantics):
    return pltpu.CompilerParams(dimension_semantics=semantics, vmem_limit_bytes=V7X_VMEM_LIMIT_BYTES)


def _rmsnorm(x, g):
    return x * lax.rsqrt(jnp.mean(x * x, axis=-1, keepdims=True) + RMS_EPS) * g


def _mod_row(mod_ref, k):
    return mod_ref[0, :, k * D_MODEL:(k + 1) * D_MODEL]


def _ada_kernel(cc_ref, w_ref, b_ref, o_ref):
    cc = cc_ref[...]
    s = cc * jax.nn.sigmoid(cc)
    o_ref[...] = jnp.dot(s.astype(BF16), w_ref[...].astype(BF16), preferred_element_type=F32) + b_ref[...]


def _ada(cc, w_ada, b_ada):
    n_out = w_ada.shape[1]
    tn = n_out // 4
    return pl.pallas_call(
        _ada_kernel,
        grid=(n_out // tn,),
        in_specs=[_const_spec((MOD_ROWS, D_MODEL)),
                  pl.BlockSpec((D_MODEL, tn), lambda j: (0, j)),
                  pl.BlockSpec((1, tn), lambda j: (0, j))],
        out_specs=pl.BlockSpec((MOD_ROWS, tn), lambda j: (0, j)),
        out_shape=jax.ShapeDtypeStruct((MOD_ROWS, n_out), F32),
        compiler_params=_params(("arbitrary",)),
        name="ada",
    )(cc, w_ada, b_ada)


def _stage_rows(k, n):
    fits = [r for r in range(BF16_SUBLANES, k + 1, BF16_SUBLANES) if k % r == 0 and r * n * 4 <= STAGE_CHUNK_BYTES]
    return max(fits)


def _stage_weight(src_hbm, dst_ref, stage_ref, sem_ref):
    rows = stage_ref.shape[1]
    n_chunks = src_hbm.shape[0] // rows

    def copy(c):
        return pltpu.make_async_copy(src_hbm.at[pl.ds(c * rows, rows)], stage_ref.at[c % 2], sem_ref.at[c % 2])

    copy(0).start()
    for c in range(n_chunks):
        if c + 1 < n_chunks:
            copy(c + 1).start()
        copy(c).wait()
        dst_ref[c * rows:(c + 1) * rows, :] = stage_ref[c % 2].astype(BF16)


class _StagedWeights:
    def __init__(self, weights):
        self.weights = list(weights)
        self.in_specs = [pl.BlockSpec(memory_space=pl.ANY)] * len(self.weights)
        chunk_shapes = sorted({(_stage_rows(*w.shape), w.shape[1]) for w in self.weights})
        self.stage_of = [chunk_shapes.index((_stage_rows(*w.shape), w.shape[1])) for w in self.weights]
        self.scratch = ([pltpu.VMEM(w.shape, BF16) for w in self.weights]
                        + [pltpu.VMEM((2,) + s, F32) for s in chunk_shapes]
                        + [pltpu.SemaphoreType.DMA((2,))])

    def load(self, hbm_refs, scratch_refs):
        n = len(self.weights)
        dst, stages, sem = scratch_refs[:n], scratch_refs[n:-1], scratch_refs[-1]

        @pl.when(pl.program_id(0) == 0)
        def _():
            for src, d, s in zip(hbm_refs, dst, self.stage_of):
                _stage_weight(src, d, stages[s], sem)

        return dst


def _ffn_kernel(*refs, staged, mod_base, has_mix, final_norm, n_main):
    n_w = len(staged.weights)
    n_scratch = len(staged.scratch)
    x_ref, mod_ref, g_ref = refs[:3]
    w_hbm = refs[3:3 + n_w]
    rest = list(refs[3 + n_w:len(refs) - n_scratch])
    weights = staged.load(w_hbm, refs[len(refs) - n_scratch:])
    wg_ref, wu_ref, wd_ref = weights[:3]
    has_tail = n_main is not None
    if has_tail:
        tail_ref = rest[0]
        tail_o_ref = rest[-1]
        rest = rest[1:-1]
        in_tail = pl.program_id(0) >= n_main
    if has_mix:
        att_ref, zf_ref = rest[:2]
        rest = rest[2:]
        wo_ref = weights[3]
    if final_norm:
        gf_ref = rest[0]
        rest = rest[1:]
    (o_ref,) = rest

    x = x_ref[...]
    if has_tail:
        x = jnp.where(in_tail, tail_ref[...], x)
    if has_mix:
        mix = jnp.dot(att_ref[...], wo_ref[:D_DIFF, :], preferred_element_type=F32)
        mix += jnp.dot(zf_ref[...], wo_ref[D_DIFF:, :], preferred_element_type=F32)
        x = x + _mod_row(mod_ref, 5) * mix
    h = _rmsnorm(x, g_ref[...]) * (1.0 + _mod_row(mod_ref, mod_base + 1)) + _mod_row(mod_ref, mod_base)
    h = h.astype(BF16)
    acc = None
    for start, width in FF_CHUNKS:
        gate = jnp.dot(h, wg_ref[:, start:start + width], preferred_element_type=F32)
        up = jnp.dot(h, wu_ref[:, start:start + width], preferred_element_type=F32)
        a = (gate * jax.nn.sigmoid(gate) * up).astype(BF16)
        part = jnp.dot(a, wd_ref[start:start + width, :], preferred_element_type=F32)
        acc = part if acc is None else acc + part
    x = x + (0.5 * _mod_row(mod_ref, mod_base + 2)) * acc
    if final_norm:
        x = _rmsnorm(x, gf_ref[...])
    if has_tail:
        @pl.when(in_tail)
        def _():
            tail_o_ref[...] = x

        @pl.when(jnp.logical_not(in_tail))
        def _():
            o_ref[...] = x
    else:
        o_ref[...] = x


def _ffn(x, mod3, mod_index, g, wg, wu, wd, *, tm, mod_base, tail=None, mix=None, tiles_per_seq=None, final_g=None):
    t = x.shape[0]
    n_main = t // tm
    n_tiles = n_main + (tail.shape[0] // tm if tail is not None else 0)
    tile = lambda w: pl.BlockSpec((tm, w), lambda i: (jnp.minimum(i, n_main - 1), 0))
    tail_tile = pl.BlockSpec((tm, D_MODEL), lambda i: (jnp.maximum(i - n_main, 0), 0))
    staged = _StagedWeights([wg, wu, wd] + ([mix[2]] if mix is not None else []))
    in_specs = [tile(D_MODEL),
                pl.BlockSpec((1, 1, N_MOD * D_MODEL), lambda i: (mod_index(i), 0, 0)),
                _const_spec((1, D_MODEL))] + staged.in_specs
    args = [x, mod3, g] + staged.weights
    out_specs, out_shape = tile(D_MODEL), jax.ShapeDtypeStruct((t, D_MODEL), F32)
    if tail is not None:
        in_specs.append(tail_tile)
        args.append(tail)
        out_specs, out_shape = [out_specs, tail_tile], [out_shape, jax.ShapeDtypeStruct(tail.shape, F32)]
    if mix is not None:
        att, zf, _ = mix
        by_batch = pl.BlockSpec((tm, D_FOURIER), lambda i: (i % tiles_per_seq, i // tiles_per_seq))
        in_specs += [tile(D_DIFF), by_batch]
        args += [att, zf]
    if final_g is not None:
        in_specs.append(_const_spec((1, D_MODEL)))
        args.append(final_g)
    kern = functools.partial(_ffn_kernel, staged=staged, mod_base=mod_base, has_mix=mix is not None,
                             final_norm=final_g is not None, n_main=n_main if tail is not None else None)
    return pl.pallas_call(
        kern,
        grid=(n_tiles,),
        in_specs=in_specs,
        out_specs=out_specs,
        out_shape=out_shape,
        scratch_shapes=staged.scratch,
        compiler_params=_params(("arbitrary",)),
        name="ffn_mix" if mix is not None else "ffn",
    )(*args)


def _rope(x, cos, sin_lo, sin_hi):
    return (x * cos + pltpu.roll(x, HEAD_W - ROPE_PAIRS, 1) * sin_lo
            + pltpu.roll(x, ROPE_PAIRS, 1) * sin_hi)


def _inproj_kernel(*refs, staged, latent):
    n_scratch = len(staged.scratch)
    x_ref, mod_ref, g_ref = refs[:3]
    (w_ref,) = staged.load(refs[3:4], refs[len(refs) - n_scratch:])
    refs = refs[:len(refs) - n_scratch]
    h = _rmsnorm(x_ref[...], g_ref[...]) * (1.0 + _mod_row(mod_ref, 4)) + _mod_row(mod_ref, 3)
    p = jnp.dot(h.astype(BF16), w_ref[...], preferred_element_type=F32)
    if latent:
        cos_ref, slo_ref, shi_ref, dft_ref, q_ref, k_ref, v_ref, yc_ref, ys_ref = refs[4:]
        cos, slo, shi = cos_ref[...], slo_ref[...], shi_ref[...]
        for hd in range(N_HEADS):
            lo = hd * HEAD_W
            q_ref[:, lo:lo + HEAD_W] = (_rope(p[:, lo:lo + HEAD_W], cos, slo, shi) * Q_SCALE).astype(BF16)
            k_ref[:, lo:lo + HEAD_W] = _rope(p[:, D_DIFF + lo:D_DIFF + lo + HEAD_W], cos, slo, shi).astype(BF16)
        y = jnp.dot(p[:, 3 * D_DIFF:].astype(BF16), dft_ref[...], preferred_element_type=F32)
        yc_ref[...] = y[:, :D_FOURIER].astype(BF16)
        ys_ref[...] = y[:, D_FOURIER:].astype(BF16)
    else:
        k_ref, v_ref = refs[4:]
        k_ref[...] = p[:, D_DIFF:2 * D_DIFF].astype(BF16)
    v_ref[...] = p[:, 2 * D_DIFF:3 * D_DIFF].astype(BF16)


def _inproj(x, mod3, mod_index, g, w_in, *, tm, rope=None, chan_dft=None):
    t = x.shape[0]
    latent = rope is not None
    tile = lambda w: pl.BlockSpec((tm, w), lambda i: (i, 0))
    staged = _StagedWeights([w_in])
    in_specs = [tile(D_MODEL),
                pl.BlockSpec((1, 1, N_MOD * D_MODEL), lambda i: (mod_index(i), 0, 0)),
                _const_spec((1, D_MODEL))] + staged.in_specs
    args = [x, mod3, g, w_in]
    wide = jax.ShapeDtypeStruct((t, D_DIFF), BF16)
    if latent:
        tiles_per_seq = rope[0].shape[0] // tm
        rope_spec = pl.BlockSpec((tm, HEAD_W), lambda i: (i % tiles_per_seq, 0))
        in_specs += [rope_spec, rope_spec, rope_spec, _const_spec((D_FOURIER, 2 * D_FOURIER))]
        args += [*rope, chan_dft]
        narrow = jax.ShapeDtypeStruct((tiles_per_seq * tm, t // (tiles_per_seq * tm) * D_FOURIER), BF16)
        by_batch = pl.BlockSpec((tm, D_FOURIER), lambda i: (i % tiles_per_seq, i // tiles_per_seq))
        out_shape = [wide, wide, wide, narrow, narrow]
        out_specs = [tile(D_DIFF)] * 3 + [by_batch] * 2
    else:
        out_shape = [wide, wide]
        out_specs = [tile(D_DIFF)] * 2
    return pl.pallas_call(
        functools.partial(_inproj_kernel, staged=staged, latent=latent),
        grid=(t // tm,),
        in_specs=in_specs,
        out_specs=out_specs,
        out_shape=out_shape,
        scratch_shapes=staged.scratch,
        compiler_params=_params(("arbitrary",)),
        name="inproj" if latent else "inproj_ctx",
    )(*args)


def _max_key_norm_sq(kk):
    sq = (kk.astype(F32) ** 2).astype(BF16)
    r = lax.broadcasted_iota(jnp.int32, (HEAD_W, HEAD_W), 0)
    c = lax.broadcasted_iota(jnp.int32, (HEAD_W, HEAD_W), 1)
    sel = jnp.where((c == 0) == (r < HEAD_DIM), 1.0, 0.0) * jnp.where(c < 2, 1.0, 0.0)
    sums = jnp.dot(sq, sel.astype(BF16), preferred_element_type=F32)
    biggest = jnp.max(sums, axis=0, keepdims=True) * (1.0 + 2.0 ** -7)
    return biggest[:, 0:1], biggest[:, 1:2]


def _attn_kernel(lam_ref, g_ref, q_ref, k_ref, kc_ref, v_ref, vc_ref, o_ref, vt_ref, vct_ref, kn_ref, *, tk):
    tq = q_ref.shape[1]

    @pl.when(pl.program_id(2) == 0)
    def _():
        vt_ref[...] = v_ref[0].T
        vct_ref[...] = vc_ref[0].T
        (a1, a2), (b1, b2) = _max_key_norm_sq(k_ref[0]), _max_key_norm_sq(kc_ref[0])
        kn_ref[...] = jnp.sqrt(jnp.concatenate([jnp.broadcast_to(jnp.maximum(a1, b1), (1, tq)),
                                                 jnp.broadcast_to(jnp.maximum(a2, b2), (1, tq))], axis=1))

    lv = lam_ref[...]
    lam = (jnp.exp(jnp.sum(lv[0:1] * lv[1:2], axis=1, keepdims=True))
           - jnp.exp(jnp.sum(lv[2:3] * lv[3:4], axis=1, keepdims=True)) + LAMBDA_INIT)

    qt = q_ref[0].T
    none = jnp.zeros((HEAD_DIM, tq), BF16)
    qcat = jnp.concatenate([jnp.concatenate([qt[:HEAD_DIM], none], axis=0),
                            jnp.concatenate([none, qt[HEAD_DIM:]], axis=0)], axis=1)
    chunks = [(k_ref[0, c * tk:(c + 1) * tk, :], vt_ref[:, c * tk:(c + 1) * tk]) for c in range(k_ref.shape[1] // tk)]
    chunks.append((kc_ref[0], vct_ref[...]))

    def scores(kk):
        return jnp.dot(kk, qcat, preferred_element_type=F32)

    def softmax_sums(shift):
        lsum, acc = 0.0, 0.0
        for kk, vt in chunks:
            e = jnp.exp2(scores(kk) - shift)
            lsum += jnp.sum(e.reshape(-1, 8, 2 * tq), axis=0)
            acc += jnp.dot(vt, e.astype(BF16), preferred_element_type=F32)
        return jnp.sum(lsum, axis=0, keepdims=True), acc

    def finish(l, acc):
        ot = acc[:, :tq] * (1.0 / l[:, :tq]) - acc[:, tq:] * (lam / l[:, tq:])
        scale = lax.rsqrt(jnp.mean(ot * ot, axis=0, keepdims=True) + RMS_EPS)
        o_ref[0] = (ot * scale * (g_ref[...] * (1.0 - LAMBDA_INIT))).astype(BF16).T

    sq = qt.astype(F32) ** 2
    qn = jnp.sqrt(jnp.concatenate([jnp.sum(sq[:HEAD_DIM], axis=0, keepdims=True),
                                   jnp.sum(sq[HEAD_DIM:], axis=0, keepdims=True)], axis=1))
    l, acc = softmax_sums(qn * kn_ref[...])
    finish(l, acc)

    @pl.when(jnp.min(l) < UNDERFLOW_GUARD)
    def _():
        m = None
        for kk, _ in chunks:
            cm = jnp.max(scores(kk), axis=0, keepdims=True)
            m = cm if m is None else jnp.maximum(m, cm)
        finish(*softmax_sums(m))


def _attention(lamvec, subln_g, q, k, kc, v, vc, *, tq, tk):
    b, n, _ = q.shape
    c = kc.shape[1]
    head_block = lambda rows: pl.BlockSpec((1, rows, HEAD_W), lambda bi, hi, qi: (bi, 0, hi))
    return pl.pallas_call(
        functools.partial(_attn_kernel, tk=tk),
        grid=(b, N_HEADS, n // tq),
        in_specs=[_const_spec(lamvec.shape), _const_spec((HEAD_W, 1)),
                  pl.BlockSpec((1, tq, HEAD_W), lambda bi, hi, qi: (bi, qi, hi)),
                  head_block(n), head_block(c), head_block(n), head_block(c)],
        out_specs=pl.BlockSpec((1, tq, HEAD_W), lambda bi, hi, qi: (bi, qi, hi)),
        out_shape=jax.ShapeDtypeStruct((b, n, D_DIFF), BF16),
        scratch_shapes=[pltpu.VMEM((HEAD_W, n), BF16), pltpu.VMEM((HEAD_W, c), BF16),
                        pltpu.VMEM((1, 2 * tq), F32)],
        compiler_params=_params(("parallel", "parallel", "arbitrary")),
        name="diff_attn",
    )(lamvec, subln_g, q, k, kc, v, vc)


def _dft_kernel(yc_ref, ys_ref, m1_ref, twr_ref, twi_ref, w2_ref, wf_ref, perm_ref, o_ref, br_ref, bi_ref):
    n1, cols = yc_ref.shape[0], yc_ref.shape[2]
    g = m1_ref.shape[0] // (2 * n1)
    per_step = yc_ref.shape[1] // g
    n2 = br_ref.shape[0] // n1
    steps1, steps2 = n2 // g // per_step, n1 // per_step
    i = pl.program_id(0)

    @pl.when(i < steps1)
    def _():
        for s in range(per_step):
            group = i * per_step + s
            mine = slice(s * g, (s + 1) * g)
            u = jnp.concatenate([yc_ref[:, mine, :].reshape(n1 * g, cols),
                                 ys_ref[:, mine, :].reshape(n1 * g, cols)], axis=0)
            a = jnp.dot(m1_ref[...], u, preferred_element_type=F32)
            ar, ai = a[:n1 * g], a[n1 * g:]
            tr, ti = twr_ref[group], twi_ref[group]
            br = (ar * tr - ai * ti).astype(BF16)
            bi = (ar * ti + ai * tr).astype(BF16)
            for c in range(n1):
                dst = pl.ds(pl.multiple_of(c * n2 + group * g, g), g)
                br_ref[dst, :] = br[c * g:(c + 1) * g]
                bi_ref[dst, :] = bi[c * g:(c + 1) * g]

    @pl.when((i >= steps1) & (i < steps1 + steps2))
    def _():
        w = wf_ref.shape[0]
        wf = wf_ref[...].astype(BF16)
        for s in range(per_step):
            src = pl.ds(pl.multiple_of(((i - steps1) * per_step + s) * n2, n2), n2)
            b = jnp.concatenate([br_ref[src, :], bi_ref[src, :]], axis=0)
            x = jnp.dot(w2_ref[...], b, preferred_element_type=F32)
            br_ref[src, :] = jnp.concatenate(
                [jnp.dot(x[:, lo:lo + w].astype(BF16), wf, preferred_element_type=F32).astype(BF16)
                 for lo in range(0, cols, w)], axis=1)

    @pl.when(i >= steps1 + steps2)
    def _():
        run = o_ref.shape[0] // n1
        first = (i - steps1 - steps2) * run
        rows = jnp.concatenate([br_ref[pl.ds(pl.multiple_of(c * n2 + first, run), run), :] for c in range(n1)], axis=0)
        o_ref[...] = jnp.dot(perm_ref[...], rows, preferred_element_type=F32).astype(BF16)


def _position_dft(yc, ys, wf, *, tile_rows):
    n, cols = yc.shape
    n1, g, per_step = DFT_N1, BF16_SUBLANES, DFT_UNITS_PER_STEP
    n2 = n // n1
    steps1, steps2, steps3 = n2 // g // per_step, n1 // per_step, n // tile_rows
    m1, twr, twi, w2 = _position_dft_tables(n, n1, g)
    perm = _row_permutation(n1, tile_rows)
    group = pl.BlockSpec((n1, g * per_step, cols), lambda i: (0, jnp.minimum(i, steps1 - 1), 0))
    return pl.pallas_call(
        _dft_kernel,
        grid=(steps1 + steps2 + steps3,),
        in_specs=[group, group, _const_spec(m1.shape), _const_spec(twr.shape), _const_spec(twi.shape),
                  _const_spec(w2.shape), _const_spec(wf.shape), _const_spec(perm.shape)],
        out_specs=pl.BlockSpec((tile_rows, cols), lambda i: (jnp.maximum(i - steps1 - steps2, 0), 0)),
        out_shape=jax.ShapeDtypeStruct((n, cols), BF16),
        scratch_shapes=[pltpu.VMEM((n, cols), BF16), pltpu.VMEM((n, cols), BF16)],
        compiler_params=_params(("arbitrary",)),
        name="position_dft",
    )(yc.reshape(n1, n2, cols), ys.reshape(n1, n2, cols), m1, twr, twi, w2, wf, perm)


def _rope_tables(n):
    rows = n // GRID_W
    lane = np.arange(HEAD_W)
    sub = lane % HEAD_DIM
    on_row_axis = jnp.asarray(sub // (2 * ROPE_PAIRS) == 0)[None, :]
    second_half = jnp.asarray((sub % (2 * ROPE_PAIRS)) // ROPE_PAIRS == 1)[None, :]
    inv_freq = ROPE_BASE ** (-jnp.asarray(sub % ROPE_PAIRS, dtype=F32) / ROPE_PAIRS)

    def table(fn, keep):
        def side(size, mine):
            t = fn(jnp.arange(size, dtype=F32)[:, None] * inv_freq[None, :])
            return jnp.where(mine & keep, t, 0.0)
        return (side(rows, on_row_axis)[:, None, :] + side(GRID_W, ~on_row_axis)[None, :, :]).reshape(n, HEAD_W)

    return (table(jnp.cos, True), table(lambda a: -jnp.sin(a), ~second_half), table(jnp.sin, second_half))


def _channel_dft_table():
    c = np.arange(FOURIER_GROUP_DIM)
    ang = 2.0 * np.pi * np.outer(c, c) / FOURIER_GROUP_DIM
    eye = np.eye(D_FOURIER // FOURIER_GROUP_DIM)
    scale = FOURIER_GROUP_DIM ** -0.5
    return np.concatenate([np.kron(eye, np.cos(ang)), np.kron(eye, np.sin(ang))], axis=1) * scale


def _position_dft_tables(n, n1, g):
    n2 = n // n1
    w1 = np.exp(-2j * np.pi * np.outer(np.arange(n1), np.arange(n1)) / n1)
    m = np.kron(w1, np.eye(g)) * n ** -0.5
    m1 = np.block([[m.real, m.imag], [m.imag, -m.real]])
    b = (g * np.arange(n2 // g)[:, None, None] + np.arange(g)[None, None, :])
    tw = np.exp(-2j * np.pi * b * np.arange(n1)[None, :, None] / n).reshape(n2 // g, n1 * g, 1)
    ang2 = 2.0 * np.pi * np.outer(np.arange(n2), np.arange(n2)) / n2
    w2 = np.concatenate([np.cos(ang2), np.sin(ang2)], axis=1)
    f32 = lambda t: jnp.asarray(t, dtype=F32)
    return f32(m1).astype(BF16), f32(tw.real), f32(tw.imag), f32(w2).astype(BF16)


def _row_permutation(n1, rows):
    d_per_tile = rows // n1
    p = np.zeros((rows, rows), np.float32)
    c, d = np.meshgrid(np.arange(n1), np.arange(d_per_tile), indexing="ij")
    p[(c + n1 * d).ravel(), (c * d_per_tile + d).ravel()] = 1.0
    return jnp.asarray(p).astype(BF16)


def kernel(x, c, ctx, c_ctx, w_ada, b_ada, norm1_g, ffn1_w_gate, ffn1_w_up, ffn1_w_down, norm_mix_g, w_in,
           lambda_q1, lambda_k1, lambda_q2, lambda_k2, subln_g, w_fourier, w_out, norm2_g, ffn2_w_gate,
           ffn2_w_up, ffn2_w_down, final_norm_g):
    b, n, d = x.shape
    n_ctx = ctx.shape[1]
    assert (d, w_ada.shape[0]) == (D_MODEL, 1) and b + 1 <= MOD_ROWS
    tiles = TILES
    mod_row = lambda rows: (lambda i: jnp.minimum(i // (n // rows), b))
    row = lambda g: g.reshape(1, -1)

    cc = jnp.zeros((MOD_ROWS, d), F32).at[:b].set(c).at[b].set(c_ctx)
    mod3 = _ada(cc, w_ada[0], b_ada).reshape(MOD_ROWS, 1, N_MOD * d)

    x1, c1 = _ffn(x.reshape(b * n, d), mod3, mod_row(tiles.ffn_rows), row(norm1_g), ffn1_w_gate[0], ffn1_w_up[0],
                  ffn1_w_down[0], tm=tiles.ffn_rows, mod_base=0, tail=ctx.reshape(b * n_ctx, d))

    chan_dft = jnp.asarray(_channel_dft_table(), dtype=F32).astype(BF16)
    q, k, v, yc, ys = _inproj(x1, mod3, mod_row(tiles.inproj_rows), row(norm_mix_g), w_in[0], tm=tiles.inproj_rows,
                              rope=_rope_tables(n), chan_dft=chan_dft)
    kc, vc = _inproj(c1, mod3, lambda i: b, row(norm_mix_g), w_in[0], tm=tiles.inproj_rows)

    lamvec = jnp.concatenate([lambda_q1, lambda_k1, lambda_q2, lambda_k2], axis=0)
    seq = lambda a, rows: a.reshape(b, rows, a.shape[-1])
    att = _attention(lamvec, subln_g.reshape(HEAD_W, 1), seq(q, n), seq(k, n), seq(kc, n_ctx), seq(v, n), seq(vc, n_ctx),
                     tq=tiles.attn_queries, tk=tiles.attn_keys)

    zf = _position_dft(yc, ys, w_fourier[0], tile_rows=tiles.ffn_rows)

    out = _ffn(x1, mod3, mod_row(tiles.ffn_rows), row(norm2_g), ffn2_w_gate[0], ffn2_w_up[0], ffn2_w_down[0],
               tm=tiles.ffn_rows, mod_base=6, mix=(att.reshape(b * n, D_DIFF), zf, w_out[0]),
               tiles_per_seq=n // tiles.ffn_rows, final_g=row(final_norm_g))
    return out.reshape(b, n, d)
```

```python
import functools
import math
from typing import NamedTuple

import numpy as np
import jax
import jax.numpy as jnp
from jax import lax
from jax.experimental import pallas as pl
from jax.experimental.pallas import tpu as pltpu

D_MODEL = 1024
GRID_W = 64
D_FOURIER = 256
D_DIFF = 768
HEAD_DIM = 64
HEAD_W = 2 * HEAD_DIM
N_HEADS = D_DIFF // HEAD_W
FOURIER_GROUP_DIM = 64
D_IN_PROJ = 3 * D_DIFF + D_FOURIER
D_FF = 2816
N_MOD = 9
ROPE_BASE = 10000.0
ROPE_PAIRS = HEAD_DIM // 4
RMS_EPS = 1e-6
ATTN_SCALE = HEAD_DIM ** -0.5
Q_SCALE = ATTN_SCALE * math.log2(math.e)
LAMBDA_INIT = 0.8 - 0.6 * math.exp(-0.3 * 0)
UNDERFLOW_GUARD = 2.0 ** -80

F32 = jnp.float32
BF16 = jnp.bfloat16

V7X_VMEM_LIMIT_BYTES = 56 * 1024 * 1024
F32_SUBLANES = 8
BF16_SUBLANES = 16
STAGE_CHUNK_BYTES = 3 * 1024 * 1024
MOD_ROWS = F32_SUBLANES
DFT_N1 = 16
DFT_UNITS_PER_STEP = 2


class _Tiles(NamedTuple):
    ffn_rows: int = 512
    inproj_rows: int = 1024
    attn_queries: int = 1024
    attn_keys: int = 1024


TILES = _Tiles()
FF_CHUNKS = ((0, 1536), (1536, 1280))


def _const_spec(shape):
    return pl.BlockSpec(shape, lambda *_: (0,) * len(shape), pipeline_mode=pl.Buffered(1))


def _params(semantics):
    return pltpu.CompilerParams(dimension_semantics=semantics, vmem_limit_bytes=V7X_VMEM_LIMIT_BYTES)


def _rmsnorm(x, g):
    return x * lax.rsqrt(jnp.mean(x * x, axis=-1, keepdims=True) + RMS_EPS) * g


def _mod_row(mod_ref, k):
    return mod_ref[0, :, k * D_MODEL:(k + 1) * D_MODEL]


def _ada_kernel(cc_ref, w_ref, b_ref, o_ref):
    cc = cc_ref[...]
    s = cc * jax.nn.sigmoid(cc)
    o_ref[...] = jnp.dot(s.astype(BF16), w_ref[...].astype(BF16), preferred_element_type=F32) + b_ref[...]


def _ada(cc, w_ada, b_ada):
    n_out = w_ada.shape[1]
    tn = n_out // 4
    return pl.pallas_call(
        _ada_kernel,
        grid=(n_out // tn,),
        in_specs=[_const_spec((MOD_ROWS, D_MODEL)),
                  pl.BlockSpec((D_MODEL, tn), lambda j: (0, j)),
                  pl.BlockSpec((1, tn), lambda j: (0, j))],
        out_specs=pl.BlockSpec((MOD_ROWS, tn), lambda j: (0, j)),
        out_shape=jax.ShapeDtypeStruct((MOD_ROWS, n_out), F32),
        compiler_params=_params(("arbitrary",)),
        name="ada",
    )(cc, w_ada, b_ada)


def _stage_rows(k, n):
    fits = [r for r in range(BF16_SUBLANES, k + 1, BF16_SUBLANES) if k % r == 0 and r * n * 4 <= STAGE_CHUNK_BYTES]
    return max(fits)


def _stage_weight(src_hbm, dst_ref, stage_ref, sem_ref):
    rows = stage_ref.shape[1]
    n_chunks = src_hbm.shape[0] // rows

    def copy(c):
        return pltpu.make_async_copy(src_hbm.at[pl.ds(c * rows, rows)], stage_ref.at[c % 2], sem_ref.at[c % 2])

    copy(0).start()
    for c in range(n_chunks):
        if c + 1 < n_chunks:
            copy(c + 1).start()
        copy(c).wait()
        dst_ref[c * rows:(c + 1) * rows, :] = stage_ref[c % 2].astype(BF16)


class _StagedWeights:
    def __init__(self, weights):
        self.weights = list(weights)
        self.in_specs = [pl.BlockSpec(memory_space=pl.ANY)] * len(self.weights)
        chunk_shapes = sorted({(_stage_rows(*w.shape), w.shape[1]) for w in self.weights})
        self.stage_of = [chunk_shapes.index((_stage_rows(*w.shape), w.shape[1])) for w in self.weights]
        self.scratch = ([pltpu.VMEM(w.shape, BF16) for w in self.weights]
                        + [pltpu.VMEM((2,) + s, F32) for s in chunk_shapes]
                        + [pltpu.SemaphoreType.DMA((2,))])

    def load(self, hbm_refs, scratch_refs):
        n = len(self.weights)
        dst, stages, sem = scratch_refs[:n], scratch_refs[n:-1], scratch_refs[-1]

        @pl.when(pl.program_id(0) == 0)
        def _():
            for src, d, s in zip(hbm_refs, dst, self.stage_of):
                _stage_weight(src, d, stages[s], sem)

        return dst


def _ffn_kernel(*refs, staged, mod_base, has_mix, final_norm, n_main):
    n_w = len(staged.weights)
    n_scratch = len(staged.scratch)
    x_ref, mod_ref, g_ref = refs[:3]
    w_hbm = refs[3:3 + n_w]
    rest = list(refs[3 + n_w:len(refs) - n_scratch])
    weights = staged.load(w_hbm, refs[len(refs) - n_scratch:])
    wg_ref, wu_ref, wd_ref = weights[:3]
    has_tail = n_main is not None
    if has_tail:
        tail_ref = rest[0]
        tail_o_ref = rest[-1]
        rest = rest[1:-1]
        in_tail = pl.program_id(0) >= n_main
    if has_mix:
        att_ref, zf_ref = rest[:2]
        rest = rest[2:]
        wo_ref = weights[3]
    if final_norm:
        gf_ref = rest[0]
        rest = rest[1:]
    (o_ref,) = rest

    x = x_ref[...]
    if has_tail:
        x = jnp.where(in_tail, tail_ref[...], x)
    if has_mix:
        mix = jnp.dot(att_ref[...], wo_ref[:D_DIFF, :], preferred_element_type=F32)
        mix += jnp.dot(zf_ref[...], wo_ref[D_DIFF:, :], preferred_element_type=F32)
        x = x + _mod_row(mod_ref, 5) * mix
    h = _rmsnorm(x, g_ref[...]) * (1.0 + _mod_row(mod_ref, mod_base + 1)) + _mod_row(mod_ref, mod_base)
    h = h.astype(BF16)
    acc = None
    for start, width in FF_CHUNKS:
        gate = jnp.dot(h, wg_ref[:, start:start + width], preferred_element_type=F32)
        up = jnp.dot(h, wu_ref[:, start:start + width], preferred_element_type=F32)
        a = (gate * jax.nn.sigmoid(gate) * up).astype(BF16)
        part = jnp.dot(a, wd_ref[start:start + width, :], preferred_element_type=F32)
        acc = part if acc is None else acc + part
    x = x + (0.5 * _mod_row(mod_ref, mod_base + 2)) * acc
    if final_norm:
        x = _rmsnorm(x, gf_ref[...])
    if has_tail:
        @pl.when(in_tail)
        def _():
            tail_o_ref[...] = x

        @pl.when(jnp.logical_not(in_tail))
        def _():
            o_ref[...] = x
    else:
        o_ref[...] = x


def _ffn(x, mod3, mod_index, g, wg, wu, wd, *, tm, mod_base, tail=None, mix=None, tiles_per_seq=None, final_g=None):
    t = x.shape[0]
    n_main = t // tm
    n_tiles = n_main + (tail.shape[0] // tm if tail is not None else 0)
    tile = lambda w: pl.BlockSpec((tm, w), lambda i: (jnp.minimum(i, n_main - 1), 0))
    tail_tile = pl.BlockSpec((tm, D_MODEL), lambda i: (jnp.maximum(i - n_main, 0), 0))
    staged = _StagedWeights([wg, wu, wd] + ([mix[2]] if mix is not None else []))
    in_specs = [tile(D_MODEL),
                pl.BlockSpec((1, 1, N_MOD * D_MODEL), lambda i: (mod_index(i), 0, 0)),
                _const_spec((1, D_MODEL))] + staged.in_specs
    args = [x, mod3, g] + staged.weights
    out_specs, out_shape = tile(D_MODEL), jax.ShapeDtypeStruct((t, D_MODEL), F32)
    if tail is not None:
        in_specs.append(tail_tile)
        args.append(tail)
        out_specs, out_shape = [out_specs, tail_tile], [out_shape, jax.ShapeDtypeStruct(tail.shape, F32)]
    if mix is not None:
        att, zf, _ = mix
        by_batch = pl.BlockSpec((tm, D_FOURIER), lambda i: (i % tiles_per_seq, i // tiles_per_seq))
        in_specs += [tile(D_DIFF), by_batch]
        args += [att, zf]
    if final_g is not None:
        in_specs.append(_const_spec((1, D_MODEL)))
        args.append(final_g)
    kern = functools.partial(_ffn_kernel, staged=staged, mod_base=mod_base, has_mix=mix is not None,
                             final_norm=final_g is not None, n_main=n_main if tail is not None else None)
    return pl.pallas_call(
        kern,
        grid=(n_tiles,),
        in_specs=in_specs,
        out_specs=out_specs,
        out_shape=out_shape,
        scratch_shapes=staged.scratch,
        compiler_params=_params(("arbitrary",)),
        name="ffn_mix" if mix is not None else "ffn",
    )(*args)


def _rope(x, cos, sin_lo, sin_hi):
    return (x * cos + pltpu.roll(x, HEAD_W - ROPE_PAIRS, 1) * sin_lo
            + pltpu.roll(x, ROPE_PAIRS, 1) * sin_hi)


def _inproj_kernel(*refs, staged, latent):
    n_scratch = len(staged.scratch)
    x_ref, mod_ref, g_ref = refs[:3]
    (w_ref,) = staged.load(refs[3:4], refs[len(refs) - n_scratch:])
    refs = refs[:len(refs) - n_scratch]
    h = _rmsnorm(x_ref[...], g_ref[...]) * (1.0 + _mod_row(mod_ref, 4)) + _mod_row(mod_ref, 3)
    p = jnp.dot(h.astype(BF16), w_ref[...], preferred_element_type=F32)
    if latent:
        cos_ref, slo_ref, shi_ref, dft_ref, q_ref, k_ref, v_ref, yc_ref, ys_ref = refs[4:]
        cos, slo, shi = cos_ref[...], slo_ref[...], shi_ref[...]
        for hd in range(N_HEADS):
            lo = hd * HEAD_W
            q_ref[:, lo:lo + HEAD_W] = (_rope(p[:, lo:lo + HEAD_W], cos, slo, shi) * Q_SCALE).astype(BF16)
            k_ref[:, lo:lo + HEAD_W] = _rope(p[:, D_DIFF + lo:D_DIFF + lo + HEAD_W], cos, slo, shi).astype(BF16)
        y = jnp.dot(p[:, 3 * D_DIFF:].astype(BF16), dft_ref[...], preferred_element_type=F32)
        yc_ref[...] = y[:, :D_FOURIER].astype(BF16)
        ys_ref[...] = y[:, D_FOURIER:].astype(BF16)
    else:
        k_ref, v_ref = refs[4:]
        k_ref[...] = p[:, D_DIFF:2 * D_DIFF].astype(BF16)
    v_ref[...] = p[:, 2 * D_DIFF:3 * D_DIFF].astype(BF16)


def _inproj(x, mod3, mod_index, g, w_in, *, tm, rope=None, chan_dft=None):
    t = x.shape[0]
    latent = rope is not None
    tile = lambda w: pl.BlockSpec((tm, w), lambda i: (i, 0))
    staged = _StagedWeights([w_in])
    in_specs = [tile(D_MODEL),
                pl.BlockSpec((1, 1, N_MOD * D_MODEL), lambda i: (mod_index(i), 0, 0)),
                _const_spec((1, D_MODEL))] + staged.in_specs
    args = [x, mod3, g, w_in]
    wide = jax.ShapeDtypeStruct((t, D_DIFF), BF16)
    if latent:
        tiles_per_seq = rope[0].shape[0] // tm
        rope_spec = pl.BlockSpec((tm, HEAD_W), lambda i: (i % tiles_per_seq, 0))
        in_specs += [rope_spec, rope_spec, rope_spec, _const_spec((D_FOURIER, 2 * D_FOURIER))]
        args += [*rope, chan_dft]
        narrow = jax.ShapeDtypeStruct((tiles_per_seq * tm, t // (tiles_per_seq * tm) * D_FOURIER), BF16)
        by_batch = pl.BlockSpec((tm, D_FOURIER), lambda i: (i % tiles_per_seq, i // tiles_per_seq))
        out_shape = [wide, wide, wide, narrow, narrow]
        out_specs = [tile(D_DIFF)] * 3 + [by_batch] * 2
    else:
        out_shape = [wide, wide]
        out_specs = [tile(D_DIFF)] * 2
    return pl.pallas_call(
        functools.partial(_inproj_kernel, staged=staged, latent=latent),
        grid=(t // tm,),
        in_specs=in_specs,
        out_specs=out_specs,
        out_shape=out_shape,
        scratch_shapes=staged.scratch,
        compiler_params=_params(("arbitrary",)),
        name="inproj" if latent else "inproj_ctx",
    )(*args)


def _max_key_norm_sq(kk):
    sq = (kk.astype(F32) ** 2).astype(BF16)
    r = lax.broadcasted_iota(jnp.int32, (HEAD_W, HEAD_W), 0)
    c = lax.broadcasted_iota(jnp.int32, (HEAD_W, HEAD_W), 1)
    sel = jnp.where((c == 0) == (r < HEAD_DIM), 1.0, 0.0) * jnp.where(c < 2, 1.0, 0.0)
    sums = jnp.dot(sq, sel.astype(BF16), preferred_element_type=F32)
    biggest = jnp.max(sums, axis=0, keepdims=True) * (1.0 + 2.0 ** -7)
    return biggest[:, 0:1], biggest[:, 1:2]


def _attn_kernel(lam_ref, g_ref, q_ref, k_ref, kc_ref, v_ref, vc_ref, o_ref, vt_ref, vct_ref, kn_ref, *, tk):
    tq = q_ref.shape[1]

    @pl.when(pl.program_id(2) == 0)
    def _():
        vt_ref[...] = v_ref[0].T
        vct_ref[...] = vc_ref[0].T
        (a1, a2), (b1, b2) = _max_key_norm_sq(k_ref[0]), _max_key_norm_sq(kc_ref[0])
        kn_ref[...] = jnp.sqrt(jnp.concatenate([jnp.broadcast_to(jnp.maximum(a1, b1), (1, tq)),
                                                 jnp.broadcast_to(jnp.maximum(a2, b2), (1, tq))], axis=1))

    lv = lam_ref[...]
    lam = (jnp.exp(jnp.sum(lv[0:1] * lv[1:2], axis=1, keepdims=True))
           - jnp.exp(jnp.sum(lv[2:3] * lv[3:4], axis=1, keepdims=True)) + LAMBDA_INIT)

    qt = q_ref[0].T
    none = jnp.zeros((HEAD_DIM, tq), BF16)
    qcat = jnp.concatenate([jnp.concatenate([qt[:HEAD_DIM], none], axis=0),
                            jnp.concatenate([none, qt[HEAD_DIM:]], axis=0)], axis=1)
    chunks = [(k_ref[0, c * tk:(c + 1) * tk, :], vt_ref[:, c * tk:(c + 1) * tk]) for c in range(k_ref.shape[1] // tk)]
    chunks.append((kc_ref[0], vct_ref[...]))

    def scores(kk):
        return jnp.dot(kk, qcat, preferred_element_type=F32)

    def softmax_sums(shift):
        lsum, acc = 0.0, 0.0
        for kk, vt in chunks:
            e = jnp.exp2(scores(kk) - shift)
            lsum += jnp.sum(e.reshape(-1, F32_SUBLANES, 2 * tq), axis=0)
            acc += jnp.dot(vt, e.astype(BF16), preferred_element_type=F32)
        return jnp.sum(lsum, axis=0, keepdims=True), acc

    def finish(l, acc):
        ot = acc[:, :tq] * (1.0 / l[:, :tq]) - acc[:, tq:] * (lam / l[:, tq:])
        scale = lax.rsqrt(jnp.mean(ot * ot, axis=0, keepdims=True) + RMS_EPS)
        o_ref[0] = (ot * scale * (g_ref[...] * (1.0 - LAMBDA_INIT))).astype(BF16).T

    sq = qt.astype(F32) ** 2
    qn = jnp.sqrt(jnp.concatenate([jnp.sum(sq[:HEAD_DIM], axis=0, keepdims=True),
                                   jnp.sum(sq[HEAD_DIM:], axis=0, keepdims=True)], axis=1))
    l, acc = softmax_sums(qn * kn_ref[...])
    finish(l, acc)

    @pl.when(jnp.min(l) < UNDERFLOW_GUARD)
    def _():
        m = None
        for kk, _ in chunks:
            cm = jnp.max(scores(kk), axis=0, keepdims=True)
            m = cm if m is None else jnp.maximum(m, cm)
        finish(*softmax_sums(m))


def _attention(lamvec, subln_g, q, k, kc, v, vc, *, tq, tk):
    b, n, _ = q.shape
    c = kc.shape[1]
    head_block = lambda rows: pl.BlockSpec((1, rows, HEAD_W), lambda bi, hi, qi: (bi, 0, hi))
    return pl.pallas_call(
        functools.partial(_attn_kernel, tk=tk),
        grid=(b, N_HEADS, n // tq),
        in_specs=[_const_spec(lamvec.shape), _const_spec((HEAD_W, 1)),
                  pl.BlockSpec((1, tq, HEAD_W), lambda bi, hi, qi: (bi, qi, hi)),
                  head_block(n), head_block(c), head_block(n), head_block(c)],
        out_specs=pl.BlockSpec((1, tq, HEAD_W), lambda bi, hi, qi: (bi, qi, hi)),
        out_shape=jax.ShapeDtypeStruct((b, n, D_DIFF), BF16),
        scratch_shapes=[pltpu.VMEM((HEAD_W, n), BF16), pltpu.VMEM((HEAD_W, c), BF16),
                        pltpu.VMEM((1, 2 * tq), F32)],
        compiler_params=_params(("parallel", "parallel", "arbitrary")),
        name="diff_attn",
    )(lamvec, subln_g, q, k, kc, v, vc)


def _dft_kernel(yc_ref, ys_ref, m1_ref, twr_ref, twi_ref, w2_ref, wf_ref, perm_ref, o_ref, br_ref, bi_ref):
    n1, cols = yc_ref.shape[0], yc_ref.shape[2]
    g = m1_ref.shape[0] // (2 * n1)
    per_step = yc_ref.shape[1] // g
    n2 = br_ref.shape[0] // n1
    steps1, steps2 = n2 // g // per_step, n1 // per_step
    i = pl.program_id(0)

    @pl.when(i < steps1)
    def _():
        for s in range(per_step):
            group = i * per_step + s
            mine = slice(s * g, (s + 1) * g)
            u = jnp.concatenate([yc_ref[:, mine, :].reshape(n1 * g, cols),
                                 ys_ref[:, mine, :].reshape(n1 * g, cols)], axis=0)
            a = jnp.dot(m1_ref[...], u, preferred_element_type=F32)
            ar, ai = a[:n1 * g], a[n1 * g:]
            tr, ti = twr_ref[group], twi_ref[group]
            br = (ar * tr - ai * ti).astype(BF16)
            bi = (ar * ti + ai * tr).astype(BF16)
            for c in range(n1):
                dst = pl.ds(pl.multiple_of(c * n2 + group * g, g), g)
                br_ref[dst, :] = br[c * g:(c + 1) * g]
                bi_ref[dst, :] = bi[c * g:(c + 1) * g]

    @pl.when((i >= steps1) & (i < steps1 + steps2))
    def _():
        w = wf_ref.shape[0]
        wf = wf_ref[...].astype(BF16)
        for s in range(per_step):
            src = pl.ds(pl.multiple_of(((i - steps1) * per_step + s) * n2, n2), n2)
            b = jnp.concatenate([br_ref[src, :], bi_ref[src, :]], axis=0)
            x = jnp.dot(w2_ref[...], b, preferred_element_type=F32)
            br_ref[src, :] = jnp.concatenate(
                [jnp.dot(x[:, lo:lo + w].astype(BF16), wf, preferred_element_type=F32).astype(BF16)
                 for lo in range(0, cols, w)], axis=1)

    @pl.when(i >= steps1 + steps2)
    def _():
        run = o_ref.shape[0] // n1
        first = (i - steps1 - steps2) * run
        rows = jnp.concatenate([br_ref[pl.ds(pl.multiple_of(c * n2 + first, run), run), :] for c in range(n1)], axis=0)
        o_ref[...] = jnp.dot(perm_ref[...], rows, preferred_element_type=F32).astype(BF16)


def _position_dft(yc, ys, wf, *, tile_rows):
    n, cols = yc.shape
    n1, g, per_step = DFT_N1, BF16_SUBLANES, DFT_UNITS_PER_STEP
    n2 = n // n1
    steps1, steps2, steps3 = n2 // g // per_step, n1 // per_step, n // tile_rows
    m1, twr, twi, w2 = _position_dft_tables(n, n1, g)
    perm = _row_permutation(n1, tile_rows)
    group = pl.BlockSpec((n1, g * per_step, cols), lambda i: (0, jnp.minimum(i, steps1 - 1), 0))
    return pl.pallas_call(
        _dft_kernel,
        grid=(steps1 + steps2 + steps3,),
        in_specs=[group, group, _const_spec(m1.shape), _const_spec(twr.shape), _const_spec(twi.shape),
                  _const_spec(w2.shape), _const_spec(wf.shape), _const_spec(perm.shape)],
        out_specs=pl.BlockSpec((tile_rows, cols), lambda i: (jnp.maximum(i - steps1 - steps2, 0), 0)),
        out_shape=jax.ShapeDtypeStruct((n, cols), BF16),
        scratch_shapes=[pltpu.VMEM((n, cols), BF16), pltpu.VMEM((n, cols), BF16)],
        compiler_params=_params(("arbitrary",)),
        name="position_dft",
    )(yc.reshape(n1, n2, cols), ys.reshape(n1, n2, cols), m1, twr, twi, w2, wf, perm)


def _rope_tables(n):
    rows = n // GRID_W
    lane = np.arange(HEAD_W)
    sub = lane % HEAD_DIM
    on_row_axis = jnp.asarray(sub // (2 * ROPE_PAIRS) == 0)[None, :]
    second_half = jnp.asarray((sub % (2 * ROPE_PAIRS)) // ROPE_PAIRS == 1)[None, :]
    inv_freq = ROPE_BASE ** (-jnp.asarray(sub % ROPE_PAIRS, dtype=F32) / ROPE_PAIRS)

    def table(fn, keep):
        def side(size, mine):
            t = fn(jnp.arange(size, dtype=F32)[:, None] * inv_freq[None, :])
            return jnp.where(mine & keep, t, 0.0)
        return (side(rows, on_row_axis)[:, None, :] + side(GRID_W, ~on_row_axis)[None, :, :]).reshape(n, HEAD_W)

    return (table(jnp.cos, True), table(lambda a: -jnp.sin(a), ~second_half), table(jnp.sin, second_half))


def _channel_dft_table():
    c = np.arange(FOURIER_GROUP_DIM)
    ang = 2.0 * np.pi * np.outer(c, c) / FOURIER_GROUP_DIM
    eye = np.eye(D_FOURIER // FOURIER_GROUP_DIM)
    scale = FOURIER_GROUP_DIM ** -0.5
    return np.concatenate([np.kron(eye, np.cos(ang)), np.kron(eye, np.sin(ang))], axis=1) * scale


def _position_dft_tables(n, n1, g):
    n2 = n // n1
    w1 = np.exp(-2j * np.pi * np.outer(np.arange(n1), np.arange(n1)) / n1)
    m = np.kron(w1, np.eye(g)) * n ** -0.5
    m1 = np.block([[m.real, m.imag], [m.imag, -m.real]])
    b = (g * np.arange(n2 // g)[:, None, None] + np.arange(g)[None, None, :])
    tw = np.exp(-2j * np.pi * b * np.arange(n1)[None, :, None] / n).reshape(n2 // g, n1 * g, 1)
    ang2 = 2.0 * np.pi * np.outer(np.arange(n2), np.arange(n2)) / n2
    w2 = np.concatenate([np.cos(ang2), np.sin(ang2)], axis=1)
    f32 = lambda t: jnp.asarray(t, dtype=F32)
    return f32(m1).astype(BF16), f32(tw.real), f32(tw.imag), f32(w2).astype(BF16)


def _row_permutation(n1, rows):
    d_per_tile = rows // n1
    p = np.zeros((rows, rows), np.float32)
    c, d = np.meshgrid(np.arange(n1), np.arange(d_per_tile), indexing="ij")
    p[(c + n1 * d).ravel(), (c * d_per_tile + d).ravel()] = 1.0
    return jnp.asarray(p).astype(BF16)


def kernel(x, c, ctx, c_ctx, w_ada, b_ada, norm1_g, ffn1_w_gate, ffn1_w_up, ffn1_w_down, norm_mix_g, w_in,
           lambda_q1, lambda_k1, lambda_q2, lambda_k2, subln_g, w_fourier, w_out, norm2_g, ffn2_w_gate,
           ffn2_w_up, ffn2_w_down, final_norm_g):
    b, n, d = x.shape
    n_ctx = ctx.shape[1]
    assert (d, w_ada.shape[0]) == (D_MODEL, 1) and b + 1 <= MOD_ROWS
    tiles = TILES
    mod_row = lambda rows: (lambda i: jnp.minimum(i // (n // rows), b))
    row = lambda g: g.reshape(1, -1)

    cc = jnp.zeros((MOD_ROWS, d), F32).at[:b].set(c).at[b].set(c_ctx)
    mod3 = _ada(cc, w_ada[0], b_ada).reshape(MOD_ROWS, 1, N_MOD * d)

    x1, c1 = _ffn(x.reshape(b * n, d), mod3, mod_row(tiles.ffn_rows), row(norm1_g), ffn1_w_gate[0], ffn1_w_up[0],
                  ffn1_w_down[0], tm=tiles.ffn_rows, mod_base=0, tail=ctx.reshape(b * n_ctx, d))

    chan_dft = jnp.asarray(_channel_dft_table(), dtype=F32).astype(BF16)
    q, k, v, yc, ys = _inproj(x1, mod3, mod_row(tiles.inproj_rows), row(norm_mix_g), w_in[0], tm=tiles.inproj_rows,
                              rope=_rope_tables(n), chan_dft=chan_dft)
    kc, vc = _inproj(c1, mod3, lambda i: b, row(norm_mix_g), w_in[0], tm=tiles.inproj_rows)

    lamvec = jnp.concatenate([lambda_q1, lambda_k1, lambda_q2, lambda_k2], axis=0)
    seq = lambda a, rows: a.reshape(b, rows, a.shape[-1])
    att = _attention(lamvec, subln_g.reshape(HEAD_W, 1), seq(q, n), seq(k, n), seq(kc, n_ctx), seq(v, n), seq(vc, n_ctx),
                     tq=tiles.attn_queries, tk=tiles.attn_keys)

    zf = _position_dft(yc, ys, w_fourier[0], tile_rows=tiles.ffn_rows)

    out = _ffn(x1, mod3, mod_row(tiles.ffn_rows), row(norm2_g), ffn2_w_gate[0], ffn2_w_up[0], ffn2_w_down[0],
               tm=tiles.ffn_rows, mod_base=6, mix=(att.reshape(b * n, D_DIFF), zf, w_out[0]),
               tiles_per_seq=n // tiles.ffn_rows, final_g=row(final_norm_g))
    return out.reshape(b, n, d)
```

```python
import functools
import math
from typing import NamedTuple

import numpy as np
import jax
import jax.numpy as jnp
from jax import lax
from jax.experimental import pallas as pl
from jax.experimental.pallas import tpu as pltpu

D_MODEL = 1024
GRID_W = 64
D_FOURIER = 256
D_DIFF = 768
HEAD_DIM = 64
HEAD_W = 2 * HEAD_DIM
N_HEADS = D_DIFF // HEAD_W
FOURIER_GROUP_DIM = 64
D_IN_PROJ = 3 * D_DIFF + D_FOURIER
D_FF = 2816
N_MOD = 9
ROPE_BASE = 10000.0
ROPE_PAIRS = HEAD_DIM // 4
RMS_EPS = 1e-6
ATTN_SCALE = HEAD_DIM ** -0.5
Q_SCALE = ATTN_SCALE * math.log2(math.e)
LAMBDA_INIT = 0.8 - 0.6 * math.exp(-0.3 * 0)
UNDERFLOW_GUARD = 2.0 ** -80

F32 = jnp.float32
BF16 = jnp.bfloat16

V7X_VMEM_LIMIT_BYTES = 56 * 1024 * 1024
F32_SUBLANES = 8
BF16_SUBLANES = 16
STAGE_CHUNK_BYTES = 3 * 1024 * 1024
MOD_ROWS = F32_SUBLANES
DFT_N1 = 16
DFT_UNITS_PER_STEP = 4


class _Tiles(NamedTuple):
    ffn_rows: int = 512
    inproj_rows: int = 1024
    attn_queries: int = 1024
    attn_keys: int = 1024


TILES = _Tiles()
FF_CHUNKS = ((0, 1536), (1536, 1280))


def _const_spec(shape):
    return pl.BlockSpec(shape, lambda *_: (0,) * len(shape), pipeline_mode=pl.Buffered(1))


def _params(semantics):
    return pltpu.CompilerParams(dimension_semantics=semantics, vmem_limit_bytes=V7X_VMEM_LIMIT_BYTES)


def _rmsnorm(x, g):
    return x * lax.rsqrt(jnp.mean(x * x, axis=-1, keepdims=True) + RMS_EPS) * g


def _mod_row(mod_ref, k):
    return mod_ref[0, :, k * D_MODEL:(k + 1) * D_MODEL]


def _ada_kernel(cc_ref, w_ref, b_ref, o_ref):
    cc = cc_ref[...]
    s = cc * jax.nn.sigmoid(cc)
    o_ref[...] = jnp.dot(s.astype(BF16), w_ref[...].astype(BF16), preferred_element_type=F32) + b_ref[...]


def _ada(cc, w_ada, b_ada):
    n_out = w_ada.shape[1]
    tn = n_out // 4
    return pl.pallas_call(
        _ada_kernel,
        grid=(n_out // tn,),
        in_specs=[_const_spec((MOD_ROWS, D_MODEL)),
                  pl.BlockSpec((D_MODEL, tn), lambda j: (0, j)),
                  pl.BlockSpec((1, tn), lambda j: (0, j))],
        out_specs=pl.BlockSpec((MOD_ROWS, tn), lambda j: (0, j)),
        out_shape=jax.ShapeDtypeStruct((MOD_ROWS, n_out), F32),
        compiler_params=_params(("arbitrary",)),
        name="ada",
    )(cc, w_ada, b_ada)


def _stage_rows(k, n):
    fits = [r for r in range(BF16_SUBLANES, k + 1, BF16_SUBLANES) if k % r == 0 and r * n * 4 <= STAGE_CHUNK_BYTES]
    return max(fits)


def _stage_weight(src_hbm, dst_ref, stage_ref, sem_ref):
    rows = stage_ref.shape[1]
    n_chunks = src_hbm.shape[0] // rows

    def copy(c):
        return pltpu.make_async_copy(src_hbm.at[pl.ds(c * rows, rows)], stage_ref.at[c % 2], sem_ref.at[c % 2])

    copy(0).start()
    for c in range(n_chunks):
        if c + 1 < n_chunks:
            copy(c + 1).start()
        copy(c).wait()
        dst_ref[c * rows:(c + 1) * rows, :] = stage_ref[c % 2].astype(BF16)


class _StagedWeights:
    def __init__(self, weights):
        self.weights = list(weights)
        self.in_specs = [pl.BlockSpec(memory_space=pl.ANY)] * len(self.weights)
        chunk_shapes = sorted({(_stage_rows(*w.shape), w.shape[1]) for w in self.weights})
        self.stage_of = [chunk_shapes.index((_stage_rows(*w.shape), w.shape[1])) for w in self.weights]
        self.scratch = ([pltpu.VMEM(w.shape, BF16) for w in self.weights]
                        + [pltpu.VMEM((2,) + s, F32) for s in chunk_shapes]
                        + [pltpu.SemaphoreType.DMA((2,))])

    def load(self, hbm_refs, scratch_refs):
        n = len(self.weights)
        dst, stages, sem = scratch_refs[:n], scratch_refs[n:-1], scratch_refs[-1]

        @pl.when(pl.program_id(0) == 0)
        def _():
            for src, d, s in zip(hbm_refs, dst, self.stage_of):
                _stage_weight(src, d, stages[s], sem)

        return dst


def _ffn_kernel(*refs, staged, mod_base, has_mix, final_norm, n_main):
    n_w = len(staged.weights)
    n_scratch = len(staged.scratch)
    x_ref, mod_ref, g_ref = refs[:3]
    w_hbm = refs[3:3 + n_w]
    rest = list(refs[3 + n_w:len(refs) - n_scratch])
    weights = staged.load(w_hbm, refs[len(refs) - n_scratch:])
    wg_ref, wu_ref, wd_ref = weights[:3]
    has_tail = n_main is not None
    if has_tail:
        tail_ref = rest[0]
        tail_o_ref = rest[-1]
        rest = rest[1:-1]
        in_tail = pl.program_id(0) >= n_main
    if has_mix:
        att_ref, zf_ref = rest[:2]
        rest = rest[2:]
        wo_ref = weights[3]
    if final_norm:
        gf_ref = rest[0]
        rest = rest[1:]
    (o_ref,) = rest

    x = x_ref[...]
    if has_tail:
        x = jnp.where(in_tail, tail_ref[...], x)
    if has_mix:
        mix = jnp.dot(att_ref[...], wo_ref[:D_DIFF, :], preferred_element_type=F32)
        mix += jnp.dot(zf_ref[...], wo_ref[D_DIFF:, :], preferred_element_type=F32)
        x = x + _mod_row(mod_ref, 5) * mix
    h = _rmsnorm(x, g_ref[...]) * (1.0 + _mod_row(mod_ref, mod_base + 1)) + _mod_row(mod_ref, mod_base)
    h = h.astype(BF16)
    acc = None
    for start, width in FF_CHUNKS:
        gate = jnp.dot(h, wg_ref[:, start:start + width], preferred_element_type=F32)
        up = jnp.dot(h, wu_ref[:, start:start + width], preferred_element_type=F32)
        a = (gate * jax.nn.sigmoid(gate) * up).astype(BF16)
        part = jnp.dot(a, wd_ref[start:start + width, :], preferred_element_type=F32)
        acc = part if acc is None else acc + part
    x = x + (0.5 * _mod_row(mod_ref, mod_base + 2)) * acc
    if final_norm:
        x = _rmsnorm(x, gf_ref[...])
    if has_tail:
        @pl.when(in_tail)
        def _():
            tail_o_ref[...] = x

        @pl.when(jnp.logical_not(in_tail))
        def _():
            o_ref[...] = x
    else:
        o_ref[...] = x


def _ffn(x, mod3, mod_index, g, wg, wu, wd, *, tm, mod_base, tail=None, mix=None, tiles_per_seq=None, final_g=None):
    t = x.shape[0]
    n_main = t // tm
    n_tiles = n_main + (tail.shape[0] // tm if tail is not None else 0)
    tile = lambda w: pl.BlockSpec((tm, w), lambda i: (jnp.minimum(i, n_main - 1), 0))
    tail_tile = pl.BlockSpec((tm, D_MODEL), lambda i: (jnp.maximum(i - n_main, 0), 0))
    staged = _StagedWeights([wg, wu, wd] + ([mix[2]] if mix is not None else []))
    in_specs = [tile(D_MODEL),
                pl.BlockSpec((1, 1, N_MOD * D_MODEL), lambda i: (mod_index(i), 0, 0)),
                _const_spec((1, D_MODEL))] + staged.in_specs
    args = [x, mod3, g] + staged.weights
    out_specs, out_shape = tile(D_MODEL), jax.ShapeDtypeStruct((t, D_MODEL), F32)
    if tail is not None:
        in_specs.append(tail_tile)
        args.append(tail)
        out_specs, out_shape = [out_specs, tail_tile], [out_shape, jax.ShapeDtypeStruct(tail.shape, F32)]
    if mix is not None:
        att, zf, _ = mix
        by_batch = pl.BlockSpec((tm, D_FOURIER), lambda i: (i % tiles_per_seq, i // tiles_per_seq))
        in_specs += [tile(D_DIFF), by_batch]
        args += [att, zf]
    if final_g is not None:
        in_specs.append(_const_spec((1, D_MODEL)))
        args.append(final_g)
    kern = functools.partial(_ffn_kernel, staged=staged, mod_base=mod_base, has_mix=mix is not None,
                             final_norm=final_g is not None, n_main=n_main if tail is not None else None)
    return pl.pallas_call(
        kern,
        grid=(n_tiles,),
        in_specs=in_specs,
        out_specs=out_specs,
        out_shape=out_shape,
        scratch_shapes=staged.scratch,
        compiler_params=_params(("arbitrary",)),
        name="ffn_mix" if mix is not None else "ffn",
    )(*args)


def _rope(x, cos, sin_lo, sin_hi):
    return (x * cos + pltpu.roll(x, HEAD_W - ROPE_PAIRS, 1) * sin_lo
            + pltpu.roll(x, ROPE_PAIRS, 1) * sin_hi)


def _inproj_kernel(x_ref, c_ref, mod_ref, g_ref, w_hbm, cos_ref, slo_ref, shi_ref, dft_ref,
                   q_ref, k_ref, v_ref, yc_ref, ys_ref, *scratch, staged, n_latent):
    (w_ref,) = staged.load([w_hbm], scratch)
    is_ctx = pl.program_id(0) >= n_latent
    x = jnp.where(is_ctx, c_ref[...], x_ref[...])
    h = _rmsnorm(x, g_ref[...]) * (1.0 + _mod_row(mod_ref, 4)) + _mod_row(mod_ref, 3)
    p = jnp.dot(h.astype(BF16), w_ref[...], preferred_element_type=F32)
    cos = jnp.where(is_ctx, 1.0, cos_ref[...])
    slo, shi = jnp.where(is_ctx, 0.0, slo_ref[...]), jnp.where(is_ctx, 0.0, shi_ref[...])
    for hd in range(N_HEADS):
        lo = hd * HEAD_W
        q_ref[:, lo:lo + HEAD_W] = (_rope(p[:, lo:lo + HEAD_W], cos, slo, shi) * Q_SCALE).astype(BF16)
        k_ref[:, lo:lo + HEAD_W] = _rope(p[:, D_DIFF + lo:D_DIFF + lo + HEAD_W], cos, slo, shi).astype(BF16)
    v_ref[...] = p[:, 2 * D_DIFF:3 * D_DIFF].astype(BF16)
    y = jnp.dot(p[:, 3 * D_DIFF:].astype(BF16), dft_ref[...], preferred_element_type=F32)
    yc_ref[...] = y[:, :D_FOURIER].astype(BF16)
    ys_ref[...] = y[:, D_FOURIER:].astype(BF16)


def _inproj(x, ctx, mod3, mod_index, g, w_in, rope, chan_dft, *, tm):
    t, tc = x.shape[0], ctx.shape[0]
    n = rope[0].shape[0]
    tiles_per_seq, n_latent, n_ctx_tiles = n // tm, t // tm, tc // tm
    assert t % n == 0 and n % tm == 0 and tc % tm == 0 and n_ctx_tiles <= tiles_per_seq
    tile = lambda w: pl.BlockSpec((tm, w), lambda i: (i, 0))
    rope_spec = pl.BlockSpec((tm, HEAD_W), lambda i: (i % tiles_per_seq, 0))
    by_batch = pl.BlockSpec((tm, D_FOURIER), lambda i: (i % tiles_per_seq, i // tiles_per_seq))
    staged = _StagedWeights([w_in])
    wide = jax.ShapeDtypeStruct((t + tc, D_DIFF), BF16)
    narrow = jax.ShapeDtypeStruct((n, (t // n + 1) * D_FOURIER), BF16)
    return pl.pallas_call(
        functools.partial(_inproj_kernel, staged=staged, n_latent=n_latent),
        grid=(n_latent + n_ctx_tiles,),
        in_specs=[pl.BlockSpec((tm, D_MODEL), lambda i: (jnp.minimum(i, n_latent - 1), 0)),
                  pl.BlockSpec((tm, D_MODEL), lambda i: (jnp.maximum(i - n_latent, 0), 0)),
                  pl.BlockSpec((1, 1, N_MOD * D_MODEL), lambda i: (mod_index(i), 0, 0)),
                  _const_spec((1, D_MODEL))] + staged.in_specs
                 + [rope_spec, rope_spec, rope_spec, _const_spec((D_FOURIER, 2 * D_FOURIER))],
        out_specs=[tile(D_DIFF)] * 3 + [by_batch] * 2,
        out_shape=[wide] * 3 + [narrow] * 2,
        scratch_shapes=staged.scratch,
        compiler_params=_params(("arbitrary",)),
        name="inproj",
    )(x, ctx, mod3, g, w_in, *rope, chan_dft)


def _max_key_norm_sq(kk):
    sq = (kk.astype(F32) ** 2).astype(BF16)
    r = lax.broadcasted_iota(jnp.int32, (HEAD_W, HEAD_W), 0)
    c = lax.broadcasted_iota(jnp.int32, (HEAD_W, HEAD_W), 1)
    sel = jnp.where((c == 0) == (r < HEAD_DIM), 1.0, 0.0) * jnp.where(c < 2, 1.0, 0.0)
    sums = jnp.dot(sq, sel.astype(BF16), preferred_element_type=F32)
    biggest = jnp.max(sums, axis=0, keepdims=True) * (1.0 + 2.0 ** -7)
    return biggest[:, 0:1], biggest[:, 1:2]


def _attn_kernel(lam_ref, g_ref, q_ref, k_ref, kc_ref, v_ref, vc_ref, o_ref, vt_ref, vct_ref, kn_ref, *, tk):
    tq = q_ref.shape[0]

    @pl.when(pl.program_id(2) == 0)
    def _():
        vt_ref[...] = v_ref[...].T
        vct_ref[...] = vc_ref[...].T
        (a1, a2), (b1, b2) = _max_key_norm_sq(k_ref[...]), _max_key_norm_sq(kc_ref[...])
        kn_ref[...] = jnp.sqrt(jnp.concatenate([jnp.broadcast_to(jnp.maximum(a1, b1), (1, tq)),
                                                 jnp.broadcast_to(jnp.maximum(a2, b2), (1, tq))], axis=1))

    lv = lam_ref[...]
    lam = (jnp.exp(jnp.sum(lv[0:1] * lv[1:2], axis=1, keepdims=True))
           - jnp.exp(jnp.sum(lv[2:3] * lv[3:4], axis=1, keepdims=True)) + LAMBDA_INIT)

    qt = q_ref[...].T
    none = jnp.zeros((HEAD_DIM, tq), BF16)
    qcat = jnp.concatenate([jnp.concatenate([qt[:HEAD_DIM], none], axis=0),
                            jnp.concatenate([none, qt[HEAD_DIM:]], axis=0)], axis=1)
    chunks = [(k_ref[c * tk:(c + 1) * tk, :], vt_ref[:, c * tk:(c + 1) * tk]) for c in range(k_ref.shape[0] // tk)]
    chunks.append((kc_ref[...], vct_ref[...]))

    def scores(kk):
        return jnp.dot(kk, qcat, preferred_element_type=F32)

    def softmax_sums(shift):
        lsum, acc = 0.0, 0.0
        for kk, vt in chunks:
            e = jnp.exp2(scores(kk) - shift)
            lsum += jnp.sum(e.reshape(-1, F32_SUBLANES, 2 * tq), axis=0)
            acc += jnp.dot(vt, e.astype(BF16), preferred_element_type=F32)
        return jnp.sum(lsum, axis=0, keepdims=True), acc

    def finish(l, acc):
        ot = acc[:, :tq] * (1.0 / l[:, :tq]) - acc[:, tq:] * (lam / l[:, tq:])
        scale = lax.rsqrt(jnp.mean(ot * ot, axis=0, keepdims=True) + RMS_EPS)
        o_ref[...] = (ot * scale * (g_ref[...] * (1.0 - LAMBDA_INIT))).astype(BF16).T

    sq = qt.astype(F32) ** 2
    qn = jnp.sqrt(jnp.concatenate([jnp.sum(sq[:HEAD_DIM], axis=0, keepdims=True),
                                   jnp.sum(sq[HEAD_DIM:], axis=0, keepdims=True)], axis=1))
    l, acc = softmax_sums(qn * kn_ref[...])
    finish(l, acc)

    @pl.when(jnp.min(l) < UNDERFLOW_GUARD)
    def _():
        m = None
        for kk, _ in chunks:
            cm = jnp.max(scores(kk), axis=0, keepdims=True)
            m = cm if m is None else jnp.maximum(m, cm)
        finish(*softmax_sums(m))


def _attention(lamvec, subln_g, q, k, v, *, batch, n, n_ctx, tq, tk):
    t = batch * n
    assert t % n_ctx == 0
    blocks_per_seq = n // tq
    query_block = pl.BlockSpec((tq, HEAD_W), lambda bi, hi, qi: (bi * blocks_per_seq + qi, hi))
    latent_keys = pl.BlockSpec((n, HEAD_W), lambda bi, hi, qi: (bi, hi))
    context_keys = pl.BlockSpec((n_ctx, HEAD_W), lambda bi, hi, qi: (t // n_ctx + bi, hi))
    return pl.pallas_call(
        functools.partial(_attn_kernel, tk=tk),
        grid=(batch, N_HEADS, blocks_per_seq),
        in_specs=[_const_spec(lamvec.shape), _const_spec((HEAD_W, 1)), query_block,
                  latent_keys, context_keys, latent_keys, context_keys],
        out_specs=query_block,
        out_shape=jax.ShapeDtypeStruct((t, D_DIFF), BF16),
        scratch_shapes=[pltpu.VMEM((HEAD_W, n), BF16), pltpu.VMEM((HEAD_W, n_ctx), BF16),
                        pltpu.VMEM((1, 2 * tq), F32)],
        compiler_params=_params(("parallel", "parallel", "arbitrary")),
        name="diff_attn",
    )(lamvec, subln_g, q, k, k, v, v)


def _dft_kernel(yc_ref, ys_ref, m1_ref, twr_ref, twi_ref, w2_ref, wf_ref, perm_ref, o_ref, br_ref, bi_ref):
    n1, cols = yc_ref.shape[0], yc_ref.shape[2]
    g = m1_ref.shape[0] // (2 * n1)
    per_step = yc_ref.shape[1] // g
    n2 = br_ref.shape[0] // n1
    steps1, steps2 = n2 // g // per_step, n1 // per_step
    i = pl.program_id(0)

    @pl.when(i < steps1)
    def _():
        for s in range(per_step):
            group = i * per_step + s
            mine = slice(s * g, (s + 1) * g)
            u = jnp.concatenate([yc_ref[:, mine, :].reshape(n1 * g, cols),
                                 ys_ref[:, mine, :].reshape(n1 * g, cols)], axis=0)
            a = jnp.dot(m1_ref[...], u, preferred_element_type=F32)
            ar, ai = a[:n1 * g], a[n1 * g:]
            tr, ti = twr_ref[group], twi_ref[group]
            br = (ar * tr - ai * ti).astype(BF16)
            bi = (ar * ti + ai * tr).astype(BF16)
            for c in range(n1):
                dst = pl.ds(pl.multiple_of(c * n2 + group * g, g), g)
                br_ref[dst, :] = br[c * g:(c + 1) * g]
                bi_ref[dst, :] = bi[c * g:(c + 1) * g]

    @pl.when((i >= steps1) & (i < steps1 + steps2))
    def _():
        w = wf_ref.shape[0]
        wf = wf_ref[...].astype(BF16)
        for s in range(per_step):
            src = pl.ds(pl.multiple_of(((i - steps1) * per_step + s) * n2, n2), n2)
            b = jnp.concatenate([br_ref[src, :], bi_ref[src, :]], axis=0)
            x = jnp.dot(w2_ref[...], b, preferred_element_type=F32)
            br_ref[src, :] = jnp.concatenate(
                [jnp.dot(x[:, lo:lo + w].astype(BF16), wf, preferred_element_type=F32).astype(BF16)
                 for lo in range(0, cols, w)], axis=1)

    @pl.when(i >= steps1 + steps2)
    def _():
        run = o_ref.shape[0] // n1
        first = (i - steps1 - steps2) * run
        rows = jnp.concatenate([br_ref[pl.ds(pl.multiple_of(c * n2 + first, run), run), :] for c in range(n1)], axis=0)
        o_ref[...] = jnp.dot(perm_ref[...], rows, preferred_element_type=F32).astype(BF16)


def _position_dft(yc, ys, wf, *, cols, tile_rows):
    n, all_cols = yc.shape
    n1, g, per_step = DFT_N1, BF16_SUBLANES, DFT_UNITS_PER_STEP
    n2 = n // n1
    steps1, steps2, steps3 = n2 // g // per_step, n1 // per_step, n // tile_rows
    m1, twr, twi, w2 = _position_dft_tables(n, n1, g)
    perm = _row_permutation(n1, tile_rows)
    group = pl.BlockSpec((n1, g * per_step, cols), lambda i: (0, jnp.minimum(i, steps1 - 1), 0))
    return pl.pallas_call(
        _dft_kernel,
        grid=(steps1 + steps2 + steps3,),
        in_specs=[group, group, _const_spec(m1.shape), _const_spec(twr.shape), _const_spec(twi.shape),
                  _const_spec(w2.shape), _const_spec(wf.shape), _const_spec(perm.shape)],
        out_specs=pl.BlockSpec((tile_rows, cols), lambda i: (jnp.maximum(i - steps1 - steps2, 0), 0)),
        out_shape=jax.ShapeDtypeStruct((n, cols), BF16),
        scratch_shapes=[pltpu.VMEM((n, cols), BF16), pltpu.VMEM((n, cols), BF16)],
        compiler_params=_params(("arbitrary",)),
        name="position_dft",
    )(yc.reshape(n1, n2, all_cols), ys.reshape(n1, n2, all_cols), m1, twr, twi, w2, wf, perm)


def _rope_tables(n):
    rows = n // GRID_W
    lane = np.arange(HEAD_W)
    sub = lane % HEAD_DIM
    on_row_axis = jnp.asarray(sub // (2 * ROPE_PAIRS) == 0)[None, :]
    second_half = jnp.asarray((sub % (2 * ROPE_PAIRS)) // ROPE_PAIRS == 1)[None, :]
    inv_freq = ROPE_BASE ** (-jnp.asarray(sub % ROPE_PAIRS, dtype=F32) / ROPE_PAIRS)

    def table(fn, keep):
        def side(size, mine):
            t = fn(jnp.arange(size, dtype=F32)[:, None] * inv_freq[None, :])
            return jnp.where(mine & keep, t, 0.0)
        return (side(rows, on_row_axis)[:, None, :] + side(GRID_W, ~on_row_axis)[None, :, :]).reshape(n, HEAD_W)

    return (table(jnp.cos, True), table(lambda a: -jnp.sin(a), ~second_half), table(jnp.sin, second_half))


def _channel_dft_table():
    c = np.arange(FOURIER_GROUP_DIM)
    ang = 2.0 * np.pi * np.outer(c, c) / FOURIER_GROUP_DIM
    eye = np.eye(D_FOURIER // FOURIER_GROUP_DIM)
    scale = FOURIER_GROUP_DIM ** -0.5
    return np.concatenate([np.kron(eye, np.cos(ang)), np.kron(eye, np.sin(ang))], axis=1) * scale


def _position_dft_tables(n, n1, g):
    n2 = n // n1
    w1 = np.exp(-2j * np.pi * np.outer(np.arange(n1), np.arange(n1)) / n1)
    m = np.kron(w1, np.eye(g)) * n ** -0.5
    m1 = np.block([[m.real, m.imag], [m.imag, -m.real]])
    b = (g * np.arange(n2 // g)[:, None, None] + np.arange(g)[None, None, :])
    tw = np.exp(-2j * np.pi * b * np.arange(n1)[None, :, None] / n).reshape(n2 // g, n1 * g, 1)
    ang2 = 2.0 * np.pi * np.outer(np.arange(n2), np.arange(n2)) / n2
    w2 = np.concatenate([np.cos(ang2), np.sin(ang2)], axis=1)
    f32 = lambda t: jnp.asarray(t, dtype=F32)
    return f32(m1).astype(BF16), f32(tw.real), f32(tw.imag), f32(w2).astype(BF16)


def _row_permutation(n1, rows):
    d_per_tile = rows // n1
    p = np.zeros((rows, rows), np.float32)
    c, d = np.meshgrid(np.arange(n1), np.arange(d_per_tile), indexing="ij")
    p[(c + n1 * d).ravel(), (c * d_per_tile + d).ravel()] = 1.0
    return jnp.asarray(p).astype(BF16)


def kernel(x, c, ctx, c_ctx, w_ada, b_ada, norm1_g, ffn1_w_gate, ffn1_w_up, ffn1_w_down, norm_mix_g, w_in,
           lambda_q1, lambda_k1, lambda_q2, lambda_k2, subln_g, w_fourier, w_out, norm2_g, ffn2_w_gate,
           ffn2_w_up, ffn2_w_down, final_norm_g):
    b, n, d = x.shape
    n_ctx = ctx.shape[1]
    assert (d, w_ada.shape[0]) == (D_MODEL, 1) and b + 1 <= MOD_ROWS
    tiles = TILES
    mod_row = lambda rows: (lambda i: jnp.minimum(i // (n // rows), b))
    row = lambda g: g.reshape(1, -1)

    cc = jnp.zeros((MOD_ROWS, d), F32).at[:b].set(c).at[b].set(c_ctx)
    mod3 = _ada(cc, w_ada[0], b_ada).reshape(MOD_ROWS, 1, N_MOD * d)

    x1, c1 = _ffn(x.reshape(b * n, d), mod3, mod_row(tiles.ffn_rows), row(norm1_g), ffn1_w_gate[0], ffn1_w_up[0],
                  ffn1_w_down[0], tm=tiles.ffn_rows, mod_base=0, tail=ctx.reshape(b * n_ctx, d))

    chan_dft = jnp.asarray(_channel_dft_table(), dtype=F32).astype(BF16)
    q, k, v, yc, ys = _inproj(x1, c1, mod3, mod_row(tiles.inproj_rows), row(norm_mix_g), w_in[0], _rope_tables(n),
                              chan_dft, tm=tiles.inproj_rows)

    lamvec = jnp.concatenate([lambda_q1, lambda_k1, lambda_q2, lambda_k2], axis=0)
    att = _attention(lamvec, subln_g.reshape(HEAD_W, 1), q, k, v, batch=b, n=n, n_ctx=n_ctx,
                     tq=tiles.attn_queries, tk=tiles.attn_keys)

    zf = _position_dft(yc, ys, w_fourier[0], cols=b * D_FOURIER, tile_rows=tiles.ffn_rows)

    out = _ffn(x1, mod3, mod_row(tiles.ffn_rows), row(norm2_g), ffn2_w_gate[0], ffn2_w_up[0], ffn2_w_down[0],
               tm=tiles.ffn_rows, mod_base=6, mix=(att, zf, w_out[0]),
               tiles_per_seq=n // tiles.ffn_rows, final_g=row(final_norm_g))
    return out.reshape(b, n, d)
```

```python
import functools
import math
from typing import NamedTuple

import numpy as np
import jax
import jax.numpy as jnp
from jax import lax
from jax.experimental import pallas as pl
from jax.experimental.pallas import tpu as pltpu

D_MODEL = 1024
GRID_W = 64
D_FOURIER = 256
D_DIFF = 768
HEAD_DIM = 64
HEAD_W = 2 * HEAD_DIM
N_HEADS = D_DIFF // HEAD_W
FOURIER_GROUP_DIM = 64
D_IN_PROJ = 3 * D_DIFF + D_FOURIER
D_FF = 2816
N_MOD = 9
ROPE_BASE = 10000.0
ROPE_PAIRS = HEAD_DIM // 4
RMS_EPS = 1e-6
ATTN_SCALE = HEAD_DIM ** -0.5
Q_SCALE = ATTN_SCALE * math.log2(math.e)
LAMBDA_INIT = 0.8 - 0.6 * math.exp(-0.3 * 0)
UNDERFLOW_GUARD = 2.0 ** -80

F32 = jnp.float32
BF16 = jnp.bfloat16

V7X_VMEM_LIMIT_BYTES = 56 * 1024 * 1024
F32_SUBLANES = 8
BF16_SUBLANES = 16
STAGE_CHUNK_BYTES = 3 * 1024 * 1024
MOD_ROWS = F32_SUBLANES
DFT_N1 = 16
DFT_UNITS_PER_STEP = 2


class _Tiles(NamedTuple):
    ffn_rows: int = 512
    inproj_rows: int = 1024
    attn_queries: int = 1024
    attn_keys: int = 1024


TILES = _Tiles()
FF_CHUNKS = ((0, 1536), (1536, 1280))


def _const_spec(shape):
    return pl.BlockSpec(shape, lambda *_: (0,) * len(shape), pipeline_mode=pl.Buffered(1))


def _params(semantics):
    return pltpu.CompilerParams(dimension_semantics=semantics, vmem_limit_bytes=V7X_VMEM_LIMIT_BYTES)


def _rmsnorm(x, g):
    return x * lax.rsqrt(jnp.mean(x * x, axis=-1, keepdims=True) + RMS_EPS) * g


def _mod_row(mod_ref, k):
    return mod_ref[0, :, k * D_MODEL:(k + 1) * D_MODEL]


def _ada_kernel(cc_ref, w_ref, b_ref, o_ref):
    cc = cc_ref[...]
    s = cc * jax.nn.sigmoid(cc)
    o_ref[...] = jnp.dot(s.astype(BF16), w_ref[...].astype(BF16), preferred_element_type=F32) + b_ref[...]


def _ada(cc, w_ada, b_ada):
    n_out = w_ada.shape[1]
    tn = n_out // 4
    return pl.pallas_call(
        _ada_kernel,
        grid=(n_out // tn,),
        in_specs=[_const_spec((MOD_ROWS, D_MODEL)),
                  pl.BlockSpec((D_MODEL, tn), lambda j: (0, j)),
                  pl.BlockSpec((1, tn), lambda j: (0, j))],
        out_specs=pl.BlockSpec((MOD_ROWS, tn), lambda j: (0, j)),
        out_shape=jax.ShapeDtypeStruct((MOD_ROWS, n_out), F32),
        compiler_params=_params(("arbitrary",)),
        name="ada",
    )(cc, w_ada, b_ada)


def _stage_rows(k, n):
    fits = [r for r in range(BF16_SUBLANES, k + 1, BF16_SUBLANES) if k % r == 0 and r * n * 4 <= STAGE_CHUNK_BYTES]
    return max(fits)


def _stage_weight(src_hbm, dst_ref, stage_ref, sem_ref):
    rows = stage_ref.shape[1]
    n_chunks = src_hbm.shape[0] // rows

    def copy(c):
        return pltpu.make_async_copy(src_hbm.at[pl.ds(c * rows, rows)], stage_ref.at[c % 2], sem_ref.at[c % 2])

    copy(0).start()
    for c in range(n_chunks):
        if c + 1 < n_chunks:
            copy(c + 1).start()
        copy(c).wait()
        dst_ref[c * rows:(c + 1) * rows, :] = stage_ref[c % 2].astype(BF16)


class _StagedWeights:
    def __init__(self, weights):
        self.weights = list(weights)
        self.in_specs = [pl.BlockSpec(memory_space=pl.ANY)] * len(self.weights)
        chunk_shapes = sorted({(_stage_rows(*w.shape), w.shape[1]) for w in self.weights})
        self.stage_of = [chunk_shapes.index((_stage_rows(*w.shape), w.shape[1])) for w in self.weights]
        self.scratch = ([pltpu.VMEM(w.shape, BF16) for w in self.weights]
                        + [pltpu.VMEM((2,) + s, F32) for s in chunk_shapes]
                        + [pltpu.SemaphoreType.DMA((2,))])

    def load(self, hbm_refs, scratch_refs):
        n = len(self.weights)
        dst, stages, sem = scratch_refs[:n], scratch_refs[n:-1], scratch_refs[-1]

        @pl.when(pl.program_id(0) == 0)
        def _():
            for src, d, s in zip(hbm_refs, dst, self.stage_of):
                _stage_weight(src, d, stages[s], sem)

        return dst


def _ffn_kernel(*refs, staged, mod_base, has_mix, final_norm, n_main):
    n_w = len(staged.weights)
    n_scratch = len(staged.scratch)
    x_ref, mod_ref, g_ref = refs[:3]
    w_hbm = refs[3:3 + n_w]
    rest = list(refs[3 + n_w:len(refs) - n_scratch])
    weights = staged.load(w_hbm, refs[len(refs) - n_scratch:])
    wg_ref, wu_ref, wd_ref = weights[:3]
    has_tail = n_main is not None
    if has_tail:
        tail_ref = rest[0]
        tail_o_ref = rest[-1]
        rest = rest[1:-1]
        in_tail = pl.program_id(0) >= n_main
    if has_mix:
        att_ref, zf_ref = rest[:2]
        rest = rest[2:]
        wo_ref = weights[3]
    if final_norm:
        gf_ref = rest[0]
        rest = rest[1:]
    (o_ref,) = rest

    x = x_ref[...]
    if has_tail:
        x = jnp.where(in_tail, tail_ref[...], x)
    if has_mix:
        mix = jnp.dot(att_ref[...], wo_ref[:D_DIFF, :], preferred_element_type=F32)
        mix += jnp.dot(zf_ref[...], wo_ref[D_DIFF:, :], preferred_element_type=F32)
        x = x + _mod_row(mod_ref, 5) * mix
    h = _rmsnorm(x, g_ref[...]) * (1.0 + _mod_row(mod_ref, mod_base + 1)) + _mod_row(mod_ref, mod_base)
    h = h.astype(BF16)
    acc = None
    for start, width in FF_CHUNKS:
        gate = jnp.dot(h, wg_ref[:, start:start + width], preferred_element_type=F32)
        up = jnp.dot(h, wu_ref[:, start:start + width], preferred_element_type=F32)
        a = (gate * jax.nn.sigmoid(gate) * up).astype(BF16)
        part = jnp.dot(a, wd_ref[start:start + width, :], preferred_element_type=F32)
        acc = part if acc is None else acc + part
    x = x + (0.5 * _mod_row(mod_ref, mod_base + 2)) * acc
    if final_norm:
        x = _rmsnorm(x, gf_ref[...])
    if has_tail:
        @pl.when(in_tail)
        def _():
            tail_o_ref[...] = x

        @pl.when(jnp.logical_not(in_tail))
        def _():
            o_ref[...] = x
    else:
        o_ref[...] = x


def _ffn(x, mod3, mod_index, g, wg, wu, wd, *, tm, mod_base, tail=None, mix=None, tiles_per_seq=None, final_g=None):
    t = x.shape[0]
    n_main = t // tm
    n_tiles = n_main + (tail.shape[0] // tm if tail is not None else 0)
    tile = lambda w: pl.BlockSpec((tm, w), lambda i: (jnp.minimum(i, n_main - 1), 0))
    tail_tile = pl.BlockSpec((tm, D_MODEL), lambda i: (jnp.maximum(i - n_main, 0), 0))
    staged = _StagedWeights([wg, wu, wd] + ([mix[2]] if mix is not None else []))
    in_specs = [tile(D_MODEL),
                pl.BlockSpec((1, 1, N_MOD * D_MODEL), lambda i: (mod_index(i), 0, 0)),
                _const_spec((1, D_MODEL))] + staged.in_specs
    args = [x, mod3, g] + staged.weights
    out_specs, out_shape = tile(D_MODEL), jax.ShapeDtypeStruct((t, D_MODEL), F32)
    if tail is not None:
        in_specs.append(tail_tile)
        args.append(tail)
        out_specs, out_shape = [out_specs, tail_tile], [out_shape, jax.ShapeDtypeStruct(tail.shape, F32)]
    if mix is not None:
        att, zf, _ = mix
        by_batch = pl.BlockSpec((tm, D_FOURIER), lambda i: (i % tiles_per_seq, i // tiles_per_seq))
        in_specs += [tile(D_DIFF), by_batch]
        args += [att, zf]
    if final_g is not None:
        in_specs.append(_const_spec((1, D_MODEL)))
        args.append(final_g)
    kern = functools.partial(_ffn_kernel, staged=staged, mod_base=mod_base, has_mix=mix is not None,
                             final_norm=final_g is not None, n_main=n_main if tail is not None else None)
    return pl.pallas_call(
        kern,
        grid=(n_tiles,),
        in_specs=in_specs,
        out_specs=out_specs,
        out_shape=out_shape,
        scratch_shapes=staged.scratch,
        compiler_params=_params(("arbitrary",)),
        name="ffn_mix" if mix is not None else "ffn",
    )(*args)


def _rope(x, cos, sin_lo, sin_hi):
    return (x * cos + pltpu.roll(x, HEAD_W - ROPE_PAIRS, 1) * sin_lo
            + pltpu.roll(x, ROPE_PAIRS, 1) * sin_hi)


def _inproj_kernel(*refs, staged, latent, tiles_per_seq):
    n_scratch = len(staged.scratch)
    x_ref, mod_ref, g_ref = refs[:3]
    (w_ref,) = staged.load(refs[3:4], refs[len(refs) - n_scratch:])
    refs = refs[:len(refs) - n_scratch]
    h = _rmsnorm(x_ref[...], g_ref[...]) * (1.0 + _mod_row(mod_ref, 4)) + _mod_row(mod_ref, 3)
    p = jnp.dot(h.astype(BF16), w_ref[...], preferred_element_type=F32)
    if latent:
        by_row_ref, by_col_ref, dft_ref, q_ref, k_ref, v_ref, yc_ref, ys_ref = refs[4:]
        tm, grid_w = x_ref.shape[0], by_col_ref.shape[1]
        first_row = pl.multiple_of((pl.program_id(0) % tiles_per_seq) * (tm // grid_w), tm // grid_w)
        cos, slo, shi = [(by_row_ref[j, pl.ds(first_row, tm // grid_w), :][:, None, :]
                          + by_col_ref[j][None, :, :]).reshape(tm, HEAD_W) for j in range(3)]
        for hd in range(N_HEADS):
            lo = hd * HEAD_W
            q_ref[:, lo:lo + HEAD_W] = (_rope(p[:, lo:lo + HEAD_W], cos, slo, shi) * Q_SCALE).astype(BF16)
            k_ref[:, lo:lo + HEAD_W] = _rope(p[:, D_DIFF + lo:D_DIFF + lo + HEAD_W], cos, slo, shi).astype(BF16)
        y = jnp.dot(p[:, 3 * D_DIFF:].astype(BF16), dft_ref[...], preferred_element_type=F32)
        yc_ref[...] = y[:, :D_FOURIER].astype(BF16)
        ys_ref[...] = y[:, D_FOURIER:].astype(BF16)
    else:
        k_ref, v_ref = refs[4:]
        k_ref[...] = p[:, D_DIFF:2 * D_DIFF].astype(BF16)
    v_ref[...] = p[:, 2 * D_DIFF:3 * D_DIFF].astype(BF16)


def _inproj(x, mod3, mod_index, g, w_in, *, tm, rope=None, chan_dft=None):
    t = x.shape[0]
    latent = rope is not None
    tile = lambda w: pl.BlockSpec((tm, w), lambda i: (i, 0))
    staged = _StagedWeights([w_in])
    in_specs = [tile(D_MODEL),
                pl.BlockSpec((1, 1, N_MOD * D_MODEL), lambda i: (mod_index(i), 0, 0)),
                _const_spec((1, D_MODEL))] + staged.in_specs
    args = [x, mod3, g, w_in]
    wide = jax.ShapeDtypeStruct((t, D_DIFF), BF16)
    tiles_per_seq = None
    if latent:
        by_row, by_col = rope
        tiles_per_seq = by_row.shape[1] * by_col.shape[1] // tm
        in_specs += [_const_spec(by_row.shape), _const_spec(by_col.shape), _const_spec((D_FOURIER, 2 * D_FOURIER))]
        args += [by_row, by_col, chan_dft]
        narrow = jax.ShapeDtypeStruct((tiles_per_seq * tm, t // (tiles_per_seq * tm) * D_FOURIER), BF16)
        by_batch = pl.BlockSpec((tm, D_FOURIER), lambda i: (i % tiles_per_seq, i // tiles_per_seq))
        out_shape = [wide, wide, wide, narrow, narrow]
        out_specs = [tile(D_DIFF)] * 3 + [by_batch] * 2
    else:
        out_shape = [wide, wide]
        out_specs = [tile(D_DIFF)] * 2
    return pl.pallas_call(
        functools.partial(_inproj_kernel, staged=staged, latent=latent, tiles_per_seq=tiles_per_seq),
        grid=(t // tm,),
        in_specs=in_specs,
        out_specs=out_specs,
        out_shape=out_shape,
        scratch_shapes=staged.scratch,
        compiler_params=_params(("arbitrary",)),
        name="inproj" if latent else "inproj_ctx",
    )(*args)


def _max_key_norm_sq(kk):
    sq = (kk.astype(F32) ** 2).astype(BF16)
    r = lax.broadcasted_iota(jnp.int32, (HEAD_W, HEAD_W), 0)
    c = lax.broadcasted_iota(jnp.int32, (HEAD_W, HEAD_W), 1)
    sel = jnp.where((c == 0) == (r < HEAD_DIM), 1.0, 0.0) * jnp.where(c < 2, 1.0, 0.0)
    sums = jnp.dot(sq, sel.astype(BF16), preferred_element_type=F32)
    biggest = jnp.max(sums, axis=0, keepdims=True) * (1.0 + 2.0 ** -7)
    return biggest[:, 0:1], biggest[:, 1:2]


def _attn_kernel(lam_ref, g_ref, q_ref, k_ref, kc_ref, v_ref, vc_ref, o_ref, vt_ref, vct_ref, kn_ref, *, tk):
    tq = q_ref.shape[1]

    @pl.when(pl.program_id(2) == 0)
    def _():
        vt_ref[...] = v_ref[0].T
        vct_ref[...] = vc_ref[0].T
        (a1, a2), (b1, b2) = _max_key_norm_sq(k_ref[0]), _max_key_norm_sq(kc_ref[0])
        kn_ref[...] = jnp.sqrt(jnp.concatenate([jnp.broadcast_to(jnp.maximum(a1, b1), (1, tq)),
                                                 jnp.broadcast_to(jnp.maximum(a2, b2), (1, tq))], axis=1))

    lv = lam_ref[...]
    lam = (jnp.exp(jnp.sum(lv[0:1] * lv[1:2], axis=1, keepdims=True))
           - jnp.exp(jnp.sum(lv[2:3] * lv[3:4], axis=1, keepdims=True)) + LAMBDA_INIT)

    qt = q_ref[0].T
    none = jnp.zeros((HEAD_DIM, tq), BF16)
    qcat = jnp.concatenate([jnp.concatenate([qt[:HEAD_DIM], none], axis=0),
                            jnp.concatenate([none, qt[HEAD_DIM:]], axis=0)], axis=1)
    chunks = [(k_ref[0, c * tk:(c + 1) * tk, :], vt_ref[:, c * tk:(c + 1) * tk]) for c in range(k_ref.shape[1] // tk)]
    chunks.append((kc_ref[0], vct_ref[...]))

    def scores(kk):
        return jnp.dot(kk, qcat, preferred_element_type=F32)

    def softmax_sums(shift):
        lsum, acc = 0.0, 0.0
        for kk, vt in chunks:
            e = jnp.exp2(scores(kk) - shift)
            lsum += jnp.sum(e.reshape(-1, F32_SUBLANES, 2 * tq), axis=0)
            acc += jnp.dot(vt, e.astype(BF16), preferred_element_type=F32)
        return jnp.sum(lsum, axis=0, keepdims=True), acc

    def finish(l, acc):
        ot = acc[:, :tq] * (1.0 / l[:, :tq]) - acc[:, tq:] * (lam / l[:, tq:])
        scale = lax.rsqrt(jnp.mean(ot * ot, axis=0, keepdims=True) + RMS_EPS)
        o_ref[0] = (ot * scale * (g_ref[...] * (1.0 - LAMBDA_INIT))).astype(BF16).T

    sq = qt.astype(F32) ** 2
    qn = jnp.sqrt(jnp.concatenate([jnp.sum(sq[:HEAD_DIM], axis=0, keepdims=True),
                                   jnp.sum(sq[HEAD_DIM:], axis=0, keepdims=True)], axis=1))
    l, acc = softmax_sums(qn * kn_ref[...])
    finish(l, acc)

    @pl.when(jnp.min(l) < UNDERFLOW_GUARD)
    def _():
        m = None
        for kk, _ in chunks:
            cm = jnp.max(scores(kk), axis=0, keepdims=True)
            m = cm if m is None else jnp.maximum(m, cm)
        finish(*softmax_sums(m))


def _attention(lamvec, subln_g, q, k, kc, v, vc, *, tq, tk):
    b, n, _ = q.shape
    c = kc.shape[1]
    head_block = lambda rows: pl.BlockSpec((1, rows, HEAD_W), lambda bi, hi, qi: (bi, 0, hi))
    return pl.pallas_call(
        functools.partial(_attn_kernel, tk=tk),
        grid=(b, N_HEADS, n // tq),
        in_specs=[_const_spec(lamvec.shape), _const_spec((HEAD_W, 1)),
                  pl.BlockSpec((1, tq, HEAD_W), lambda bi, hi, qi: (bi, qi, hi)),
                  head_block(n), head_block(c), head_block(n), head_block(c)],
        out_specs=pl.BlockSpec((1, tq, HEAD_W), lambda bi, hi, qi: (bi, qi, hi)),
        out_shape=jax.ShapeDtypeStruct((b, n, D_DIFF), BF16),
        scratch_shapes=[pltpu.VMEM((HEAD_W, n), BF16), pltpu.VMEM((HEAD_W, c), BF16),
                        pltpu.VMEM((1, 2 * tq), F32)],
        compiler_params=_params(("parallel", "parallel", "arbitrary")),
        name="diff_attn",
    )(lamvec, subln_g, q, k, kc, v, vc)


def _dft_kernel(yc_ref, ys_ref, m1_ref, twr_ref, twi_ref, w2_ref, wf_ref, perm_ref, o_ref, br_ref, bi_ref):
    n1, cols = yc_ref.shape[0], yc_ref.shape[2]
    g = m1_ref.shape[0] // (2 * n1)
    per_step = yc_ref.shape[1] // g
    n2 = br_ref.shape[0] // n1
    steps1, steps2 = n2 // g // per_step, n1 // per_step
    i = pl.program_id(0)

    @pl.when(i < steps1)
    def _():
        for s in range(per_step):
            group = i * per_step + s
            mine = slice(s * g, (s + 1) * g)
            u = jnp.concatenate([yc_ref[:, mine, :].reshape(n1 * g, cols),
                                 ys_ref[:, mine, :].reshape(n1 * g, cols)], axis=0)
            a = jnp.dot(m1_ref[...], u, preferred_element_type=F32)
            ar, ai = a[:n1 * g], a[n1 * g:]
            tr, ti = twr_ref[group], twi_ref[group]
            br = (ar * tr - ai * ti).astype(BF16)
            bi = (ar * ti + ai * tr).astype(BF16)
            for c in range(n1):
                dst = pl.ds(pl.multiple_of(c * n2 + group * g, g), g)
                br_ref[dst, :] = br[c * g:(c + 1) * g]
                bi_ref[dst, :] = bi[c * g:(c + 1) * g]

    @pl.when((i >= steps1) & (i < steps1 + steps2))
    def _():
        w = wf_ref.shape[0]
        wf = wf_ref[...].astype(BF16)
        for s in range(per_step):
            src = pl.ds(pl.multiple_of(((i - steps1) * per_step + s) * n2, n2), n2)
            b = jnp.concatenate([br_ref[src, :], bi_ref[src, :]], axis=0)
            x = jnp.dot(w2_ref[...], b, preferred_element_type=F32)
            br_ref[src, :] = jnp.concatenate(
                [jnp.dot(x[:, lo:lo + w].astype(BF16), wf, preferred_element_type=F32).astype(BF16)
                 for lo in range(0, cols, w)], axis=1)

    @pl.when(i >= steps1 + steps2)
    def _():
        run = o_ref.shape[0] // n1
        first = (i - steps1 - steps2) * run
        rows = jnp.concatenate([br_ref[pl.ds(pl.multiple_of(c * n2 + first, run), run), :] for c in range(n1)], axis=0)
        o_ref[...] = jnp.dot(perm_ref[...], rows, preferred_element_type=F32).astype(BF16)


def _position_dft(yc, ys, wf, *, tile_rows):
    n, cols = yc.shape
    n1, g, per_step = DFT_N1, BF16_SUBLANES, DFT_UNITS_PER_STEP
    n2 = n // n1
    steps1, steps2, steps3 = n2 // g // per_step, n1 // per_step, n // tile_rows
    m1, twr, twi, w2 = _position_dft_tables(n, n1, g)
    perm = _row_permutation(n1, tile_rows)
    group = pl.BlockSpec((n1, g * per_step, cols), lambda i: (0, jnp.minimum(i, steps1 - 1), 0))
    return pl.pallas_call(
        _dft_kernel,
        grid=(steps1 + steps2 + steps3,),
        in_specs=[group, group, _const_spec(m1.shape), _const_spec(twr.shape), _const_spec(twi.shape),
                  _const_spec(w2.shape), _const_spec(wf.shape), _const_spec(perm.shape)],
        out_specs=pl.BlockSpec((tile_rows, cols), lambda i: (jnp.maximum(i - steps1 - steps2, 0), 0)),
        out_shape=jax.ShapeDtypeStruct((n, cols), BF16),
        scratch_shapes=[pltpu.VMEM((n, cols), BF16), pltpu.VMEM((n, cols), BF16)],
        compiler_params=_params(("arbitrary",)),
        name="position_dft",
    )(yc.reshape(n1, n2, cols), ys.reshape(n1, n2, cols), m1, twr, twi, w2, wf, perm)


def _rope_tables(n):
    rows = n // GRID_W
    lane = np.arange(HEAD_W)
    sub = lane % HEAD_DIM
    on_row_axis = jnp.asarray(sub // (2 * ROPE_PAIRS) == 0)[None, :]
    second_half = jnp.asarray((sub % (2 * ROPE_PAIRS)) // ROPE_PAIRS == 1)[None, :]
    inv_freq = ROPE_BASE ** (-jnp.asarray(sub % ROPE_PAIRS, dtype=F32) / ROPE_PAIRS)
    kinds = ((jnp.cos, True), (lambda a: -jnp.sin(a), ~second_half), (jnp.sin, second_half))

    def side(size, mine):
        ang = jnp.arange(size, dtype=F32)[:, None] * inv_freq[None, :]
        return jnp.stack([jnp.where(mine & keep, fn(ang), 0.0) for fn, keep in kinds])

    return side(rows, on_row_axis), side(GRID_W, ~on_row_axis)


def _channel_dft_table():
    c = np.arange(FOURIER_GROUP_DIM)
    ang = 2.0 * np.pi * np.outer(c, c) / FOURIER_GROUP_DIM
    eye = np.eye(D_FOURIER // FOURIER_GROUP_DIM)
    scale = FOURIER_GROUP_DIM ** -0.5
    return np.concatenate([np.kron(eye, np.cos(ang)), np.kron(eye, np.sin(ang))], axis=1) * scale


def _position_dft_tables(n, n1, g):
    n2 = n // n1
    w1 = np.exp(-2j * np.pi * np.outer(np.arange(n1), np.arange(n1)) / n1)
    m = np.kron(w1, np.eye(g)) * n ** -0.5
    m1 = np.block([[m.real, m.imag], [m.imag, -m.real]])
    b = (g * np.arange(n2 // g)[:, None, None] + np.arange(g)[None, None, :])
    tw = np.exp(-2j * np.pi * b * np.arange(n1)[None, :, None] / n).reshape(n2 // g, n1 * g, 1)
    ang2 = 2.0 * np.pi * np.outer(np.arange(n2), np.arange(n2)) / n2
    w2 = np.concatenate([np.cos(ang2), np.sin(ang2)], axis=1)
    f32 = lambda t: jnp.asarray(t, dtype=F32)
    return f32(m1).astype(BF16), f32(tw.real), f32(tw.imag), f32(w2).astype(BF16)


def _row_permutation(n1, rows):
    d_per_tile = rows // n1
    p = np.zeros((rows, rows), np.float32)
    c, d = np.meshgrid(np.arange(n1), np.arange(d_per_tile), indexing="ij")
    p[(c + n1 * d).ravel(), (c * d_per_tile + d).ravel()] = 1.0
    return jnp.asarray(p).astype(BF16)


def kernel(x, c, ctx, c_ctx, w_ada, b_ada, norm1_g, ffn1_w_gate, ffn1_w_up, ffn1_w_down, norm_mix_g, w_in,
           lambda_q1, lambda_k1, lambda_q2, lambda_k2, subln_g, w_fourier, w_out, norm2_g, ffn2_w_gate,
           ffn2_w_up, ffn2_w_down, final_norm_g):
    b, n, d = x.shape
    n_ctx = ctx.shape[1]
    assert (d, w_ada.shape[0]) == (D_MODEL, 1) and b + 1 <= MOD_ROWS
    tiles = TILES
    mod_row = lambda rows: (lambda i: jnp.minimum(i // (n // rows), b))
    row = lambda g: g.reshape(1, -1)

    cc = jnp.zeros((MOD_ROWS, d), F32).at[:b].set(c).at[b].set(c_ctx)
    mod3 = _ada(cc, w_ada[0], b_ada).reshape(MOD_ROWS, 1, N_MOD * d)

    x1, c1 = _ffn(x.reshape(b * n, d), mod3, mod_row(tiles.ffn_rows), row(norm1_g), ffn1_w_gate[0], ffn1_w_up[0],
                  ffn1_w_down[0], tm=tiles.ffn_rows, mod_base=0, tail=ctx.reshape(b * n_ctx, d))

    chan_dft = jnp.asarray(_channel_dft_table(), dtype=F32).astype(BF16)
    q, k, v, yc, ys = _inproj(x1, mod3, mod_row(tiles.inproj_rows), row(norm_mix_g), w_in[0], tm=tiles.inproj_rows,
                              rope=_rope_tables(n), chan_dft=chan_dft)
    kc, vc = _inproj(c1, mod3, lambda i: b, row(norm_mix_g), w_in[0], tm=tiles.inproj_rows)

    lamvec = jnp.concatenate([lambda_q1, lambda_k1, lambda_q2, lambda_k2], axis=0)
    seq = lambda a, rows: a.reshape(b, rows, a.shape[-1])
    att = _attention(lamvec, subln_g.reshape(HEAD_W, 1), seq(q, n), seq(k, n), seq(kc, n_ctx), seq(v, n), seq(vc, n_ctx),
                     tq=tiles.attn_queries, tk=tiles.attn_keys)

    zf = _position_dft(yc, ys, w_fourier[0], tile_rows=tiles.ffn_rows)

    out = _ffn(x1, mod3, mod_row(tiles.ffn_rows), row(norm2_g), ffn2_w_gate[0], ffn2_w_up[0], ffn2_w_down[0],
               tm=tiles.ffn_rows, mod_base=6, mix=(att.reshape(b * n, D_DIFF), zf, w_out[0]),
               tiles_per_seq=n // tiles.ffn_rows, final_g=row(final_norm_g))
    return out.reshape(b, n, d)
```

```python
import functools
import math
from typing import NamedTuple

import numpy as np
import jax
import jax.numpy as jnp
from jax import lax
from jax.experimental import pallas as pl
from jax.experimental.pallas import tpu as pltpu

D_MODEL = 1024
GRID_W = 64
D_FOURIER = 256
D_DIFF = 768
HEAD_DIM = 64
HEAD_W = 2 * HEAD_DIM
N_HEADS = D_DIFF // HEAD_W
FOURIER_GROUP_DIM = 64
D_IN_PROJ = 3 * D_DIFF + D_FOURIER
D_FF = 2816
N_MOD = 9
ROPE_BASE = 10000.0
ROPE_PAIRS = HEAD_DIM // 4
RMS_EPS = 1e-6
ATTN_SCALE = HEAD_DIM ** -0.5
Q_SCALE = ATTN_SCALE * math.log2(math.e)
LAMBDA_INIT = 0.8 - 0.6 * math.exp(-0.3 * 0)
UNDERFLOW_GUARD = 2.0 ** -80

F32 = jnp.float32
BF16 = jnp.bfloat16

V7X_VMEM_LIMIT_BYTES = 56 * 1024 * 1024
F32_SUBLANES = 8
BF16_SUBLANES = 16
STAGE_CHUNK_BYTES = 3 * 1024 * 1024
MOD_ROWS = F32_SUBLANES
DFT_N1 = 16
DFT_UNITS_PER_STEP = 2


class _Tiles(NamedTuple):
    ffn_rows: int = 512
    inproj_rows: int = 1024
    attn_queries: int = 1024
    attn_keys: int = 1024


TILES = _Tiles()
FF_CHUNKS = ((0, 1536), (1536, 1280))


def _const_spec(shape):
    return pl.BlockSpec(shape, lambda *_: (0,) * len(shape), pipeline_mode=pl.Buffered(1))


def _params(semantics):
    return pltpu.CompilerParams(dimension_semantics=semantics, vmem_limit_bytes=V7X_VMEM_LIMIT_BYTES)


def _rmsnorm(x, g):
    return x * lax.rsqrt(jnp.mean(x * x, axis=-1, keepdims=True) + RMS_EPS) * g


def _mod_row(mod_ref, k):
    return mod_ref[0, :, k * D_MODEL:(k + 1) * D_MODEL]


def _ada_kernel(cc_ref, w_ref, b_ref, o_ref):
    cc = cc_ref[...]
    s = cc * jax.nn.sigmoid(cc)
    o_ref[...] = jnp.dot(s.astype(BF16), w_ref[...].astype(BF16), preferred_element_type=F32) + b_ref[...]


def _ada(cc, w_ada, b_ada):
    n_out = w_ada.shape[1]
    tn = n_out // 4
    return pl.pallas_call(
        _ada_kernel,
        grid=(n_out // tn,),
        in_specs=[_const_spec((MOD_ROWS, D_MODEL)),
                  pl.BlockSpec((D_MODEL, tn), lambda j: (0, j)),
                  pl.BlockSpec((1, tn), lambda j: (0, j))],
        out_specs=pl.BlockSpec((MOD_ROWS, tn), lambda j: (0, j)),
        out_shape=jax.ShapeDtypeStruct((MOD_ROWS, n_out), F32),
        compiler_params=_params(("arbitrary",)),
        name="ada",
    )(cc, w_ada, b_ada)


def _stage_rows(k, n):
    fits = [r for r in range(BF16_SUBLANES, k + 1, BF16_SUBLANES) if k % r == 0 and r * n * 4 <= STAGE_CHUNK_BYTES]
    return max(fits)


def _stage_weight(src_hbm, first_col, dst_ref, stage_ref, sem_ref):
    rows, width = stage_ref.shape[1], dst_ref.shape[1]
    n_chunks = src_hbm.shape[0] // rows

    def copy(c):
        return pltpu.make_async_copy(src_hbm.at[pl.ds(c * rows, rows), pl.ds(first_col, width)],
                                     stage_ref.at[c % 2, :, pl.ds(0, width)], sem_ref.at[c % 2])

    copy(0).start()
    for c in range(n_chunks):
        if c + 1 < n_chunks:
            copy(c + 1).start()
        copy(c).wait()
        dst_ref[c * rows:(c + 1) * rows, :] = stage_ref[c % 2, :, 0:width].astype(BF16)


class _StagedWeights:
    def __init__(self, weights):
        entries = [w if isinstance(w, tuple) else (w, (0, w.shape[1])) for w in weights]
        self.weights = [w for w, _ in entries]
        self.first_cols = [first for _, (first, _) in entries]
        shapes = [(w.shape[0], width) for w, (_, width) in entries]
        self.in_specs = [pl.BlockSpec(memory_space=pl.ANY)] * len(self.weights)
        stages, self.stage_of = [], [None] * len(shapes)
        for j in sorted(range(len(shapes)), key=lambda j: -shapes[j][1]):
            k, n = shapes[j]
            fits = [s for s, (rows, width) in enumerate(stages) if width >= n and k % rows == 0]
            if not fits:
                stages.append((_stage_rows(k, n), n))
                fits = [len(stages) - 1]
            self.stage_of[j] = fits[0]
        self.scratch = ([pltpu.VMEM(s, BF16) for s in shapes]
                        + [pltpu.VMEM((2,) + s, F32) for s in stages]
                        + [pltpu.SemaphoreType.DMA((2,))])

    def load(self, hbm_refs, scratch_refs):
        n = len(self.weights)
        dst, stages, sem = scratch_refs[:n], scratch_refs[n:-1], scratch_refs[-1]

        @pl.when(pl.program_id(0) == 0)
        def _():
            for src, first, d, s in zip(hbm_refs, self.first_cols, dst, self.stage_of):
                _stage_weight(src, first, d, stages[s], sem)

        return dst


def _ffn_kernel(*refs, staged, mod_base, has_mix, final_norm, n_main):
    n_w = len(staged.weights)
    n_scratch = len(staged.scratch)
    x_ref, mod_ref, g_ref = refs[:3]
    w_hbm = refs[3:3 + n_w]
    rest = list(refs[3 + n_w:len(refs) - n_scratch])
    weights = staged.load(w_hbm, refs[len(refs) - n_scratch:])
    wg_ref, wu_ref, wd_ref = weights[:3]
    has_tail = n_main is not None
    if has_tail:
        tail_ref, tail_g_ref = rest[:2]
        kc_ref, vc_ref = rest[-2:]
        rest = rest[2:-2]
        wkv_ref = weights[-1]
        in_tail = pl.program_id(0) >= n_main
    if has_mix:
        att_ref, zf_ref = rest[:2]
        rest = rest[2:]
        wo_ref = weights[3]
    if final_norm:
        gf_ref = rest[0]
        rest = rest[1:]
    (o_ref,) = rest

    x = x_ref[...]
    if has_tail:
        x = jnp.where(in_tail, tail_ref[...], x)
    if has_mix:
        mix = jnp.dot(att_ref[...], wo_ref[:D_DIFF, :], preferred_element_type=F32)
        mix += jnp.dot(zf_ref[...], wo_ref[D_DIFF:, :], preferred_element_type=F32)
        x = x + _mod_row(mod_ref, 5) * mix
    h = _rmsnorm(x, g_ref[...]) * (1.0 + _mod_row(mod_ref, mod_base + 1)) + _mod_row(mod_ref, mod_base)
    h = h.astype(BF16)
    acc = None
    for start, width in FF_CHUNKS:
        gate = jnp.dot(h, wg_ref[:, start:start + width], preferred_element_type=F32)
        up = jnp.dot(h, wu_ref[:, start:start + width], preferred_element_type=F32)
        a = (gate * jax.nn.sigmoid(gate) * up).astype(BF16)
        part = jnp.dot(a, wd_ref[start:start + width, :], preferred_element_type=F32)
        acc = part if acc is None else acc + part
    x = x + (0.5 * _mod_row(mod_ref, mod_base + 2)) * acc
    if final_norm:
        x = _rmsnorm(x, gf_ref[...])
    if has_tail:
        @pl.when(in_tail)
        def _():
            hc = _rmsnorm(x, tail_g_ref[...]) * (1.0 + _mod_row(mod_ref, 4)) + _mod_row(mod_ref, 3)
            kv = jnp.dot(hc.astype(BF16), wkv_ref[...], preferred_element_type=F32)
            kc_ref[...] = kv[:, :D_DIFF].astype(BF16)
            vc_ref[...] = kv[:, D_DIFF:].astype(BF16)

        @pl.when(jnp.logical_not(in_tail))
        def _():
            o_ref[...] = x
    else:
        o_ref[...] = x


def _ffn(x, mod3, mod_index, g, wg, wu, wd, *, tm, mod_base, tail=None, mix=None, tiles_per_seq=None, final_g=None):
    t = x.shape[0]
    n_main = t // tm
    n_tiles = n_main + (tail[0].shape[0] // tm if tail is not None else 0)
    tile = lambda w: pl.BlockSpec((tm, w), lambda i: (jnp.minimum(i, n_main - 1), 0))
    tail_tile = lambda w: pl.BlockSpec((tm, w), lambda i: (jnp.maximum(i - n_main, 0), 0))
    staged = _StagedWeights([wg, wu, wd] + ([mix[2]] if mix is not None else [])
                            + ([(tail[2], (D_DIFF, 2 * D_DIFF))] if tail is not None else []))
    in_specs = [tile(D_MODEL),
                pl.BlockSpec((1, 1, N_MOD * D_MODEL), lambda i: (mod_index(i), 0, 0)),
                _const_spec((1, D_MODEL))] + staged.in_specs
    args = [x, mod3, g] + staged.weights
    out_specs, out_shape = tile(D_MODEL), jax.ShapeDtypeStruct((t, D_MODEL), F32)
    if tail is not None:
        ctx, tail_g, _ = tail
        in_specs += [tail_tile(D_MODEL), _const_spec((1, D_MODEL))]
        args += [ctx, tail_g]
        out_specs = [out_specs, tail_tile(D_DIFF), tail_tile(D_DIFF)]
        out_shape = [out_shape] + [jax.ShapeDtypeStruct((ctx.shape[0], D_DIFF), BF16)] * 2
    if mix is not None:
        att, zf, _ = mix
        by_batch = pl.BlockSpec((tm, D_FOURIER), lambda i: (i % tiles_per_seq, i // tiles_per_seq))
        in_specs += [tile(D_DIFF), by_batch]
        args += [att, zf]
    if final_g is not None:
        in_specs.append(_const_spec((1, D_MODEL)))
        args.append(final_g)
    kern = functools.partial(_ffn_kernel, staged=staged, mod_base=mod_base, has_mix=mix is not None,
                             final_norm=final_g is not None, n_main=n_main if tail is not None else None)
    return pl.pallas_call(
        kern,
        grid=(n_tiles,),
        in_specs=in_specs,
        out_specs=out_specs,
        out_shape=out_shape,
        scratch_shapes=staged.scratch,
        compiler_params=_params(("arbitrary",)),
        name="ffn_mix" if mix is not None else "ffn",
    )(*args)


def _rope(x, cos, sin_lo, sin_hi):
    return (x * cos + pltpu.roll(x, HEAD_W - ROPE_PAIRS, 1) * sin_lo
            + pltpu.roll(x, ROPE_PAIRS, 1) * sin_hi)


def _inproj_kernel(x_ref, mod_ref, g_ref, w_hbm, by_row_ref, by_col_ref, dft_ref,
                   q_ref, k_ref, v_ref, yc_ref, ys_ref, *scratch, staged, tiles_per_seq):
    (w_ref,) = staged.load([w_hbm], scratch)
    h = _rmsnorm(x_ref[...], g_ref[...]) * (1.0 + _mod_row(mod_ref, 4)) + _mod_row(mod_ref, 3)
    p = jnp.dot(h.astype(BF16), w_ref[...], preferred_element_type=F32)
    tm, grid_w = x_ref.shape[0], by_col_ref.shape[1]
    first_row = pl.multiple_of((pl.program_id(0) % tiles_per_seq) * (tm // grid_w), tm // grid_w)
    cos, slo, shi = [(by_row_ref[j, pl.ds(first_row, tm // grid_w), :][:, None, :]
                      + by_col_ref[j][None, :, :]).reshape(tm, HEAD_W) for j in range(3)]
    for hd in range(N_HEADS):
        lo = hd * HEAD_W
        q_ref[:, lo:lo + HEAD_W] = (_rope(p[:, lo:lo + HEAD_W], cos, slo, shi) * Q_SCALE).astype(BF16)
        k_ref[:, lo:lo + HEAD_W] = _rope(p[:, D_DIFF + lo:D_DIFF + lo + HEAD_W], cos, slo, shi).astype(BF16)
    v_ref[...] = p[:, 2 * D_DIFF:3 * D_DIFF].astype(BF16)
    y = jnp.dot(p[:, 3 * D_DIFF:].astype(BF16), dft_ref[...], preferred_element_type=F32)
    yc_ref[...] = y[:, :D_FOURIER].astype(BF16)
    ys_ref[...] = y[:, D_FOURIER:].astype(BF16)


def _inproj(x, mod3, mod_index, g, w_in, rope, chan_dft, *, tm):
    t = x.shape[0]
    by_row, by_col = rope
    tiles_per_seq = by_row.shape[1] * by_col.shape[1] // tm
    tile = lambda w: pl.BlockSpec((tm, w), lambda i: (i, 0))
    by_batch = pl.BlockSpec((tm, D_FOURIER), lambda i: (i % tiles_per_seq, i // tiles_per_seq))
    staged = _StagedWeights([w_in])
    wide = jax.ShapeDtypeStruct((t, D_DIFF), BF16)
    narrow = jax.ShapeDtypeStruct((tiles_per_seq * tm, t // (tiles_per_seq * tm) * D_FOURIER), BF16)
    return pl.pallas_call(
        functools.partial(_inproj_kernel, staged=staged, tiles_per_seq=tiles_per_seq),
        grid=(t // tm,),
        in_specs=[tile(D_MODEL),
                  pl.BlockSpec((1, 1, N_MOD * D_MODEL), lambda i: (mod_index(i), 0, 0)),
                  _const_spec((1, D_MODEL))] + staged.in_specs
                 + [_const_spec(by_row.shape), _const_spec(by_col.shape), _const_spec((D_FOURIER, 2 * D_FOURIER))],
        out_specs=[tile(D_DIFF)] * 3 + [by_batch] * 2,
        out_shape=[wide] * 3 + [narrow] * 2,
        scratch_shapes=staged.scratch,
        compiler_params=_params(("arbitrary",)),
        name="inproj",
    )(x, mod3, g, w_in, by_row, by_col, chan_dft)


def _max_key_norm_sq(kk):
    sq = (kk.astype(F32) ** 2).astype(BF16)
    r = lax.broadcasted_iota(jnp.int32, (HEAD_W, HEAD_W), 0)
    c = lax.broadcasted_iota(jnp.int32, (HEAD_W, HEAD_W), 1)
    sel = jnp.where((c == 0) == (r < HEAD_DIM), 1.0, 0.0) * jnp.where(c < 2, 1.0, 0.0)
    sums = jnp.dot(sq, sel.astype(BF16), preferred_element_type=F32)
    biggest = jnp.max(sums, axis=0, keepdims=True) * (1.0 + 2.0 ** -7)
    return biggest[:, 0:1], biggest[:, 1:2]


def _attn_kernel(lam_ref, g_ref, q_ref, k_ref, kc_ref, v_ref, vc_ref, o_ref, vt_ref, vct_ref, kn_ref, *, tk):
    tq = q_ref.shape[1]

    @pl.when(pl.program_id(2) == 0)
    def _():
        vt_ref[...] = v_ref[0].T
        vct_ref[...] = vc_ref[0].T
        (a1, a2), (b1, b2) = _max_key_norm_sq(k_ref[0]), _max_key_norm_sq(kc_ref[0])
        kn_ref[...] = jnp.sqrt(jnp.concatenate([jnp.broadcast_to(jnp.maximum(a1, b1), (1, tq)),
                                                 jnp.broadcast_to(jnp.maximum(a2, b2), (1, tq))], axis=1))

    lv = lam_ref[...]
    lam = (jnp.exp(jnp.sum(lv[0:1] * lv[1:2], axis=1, keepdims=True))
           - jnp.exp(jnp.sum(lv[2:3] * lv[3:4], axis=1, keepdims=True)) + LAMBDA_INIT)

    qt = q_ref[0].T
    none = jnp.zeros((HEAD_DIM, tq), BF16)
    qcat = jnp.concatenate([jnp.concatenate([qt[:HEAD_DIM], none], axis=0),
                            jnp.concatenate([none, qt[HEAD_DIM:]], axis=0)], axis=1)
    chunks = [(k_ref[0, c * tk:(c + 1) * tk, :], vt_ref[:, c * tk:(c + 1) * tk]) for c in range(k_ref.shape[1] // tk)]
    chunks.append((kc_ref[0], vct_ref[...]))

    def scores(kk):
        return jnp.dot(kk, qcat, preferred_element_type=F32)

    def softmax_sums(shift):
        lsum, acc = 0.0, 0.0
        for kk, vt in chunks:
            e = jnp.exp2(scores(kk) - shift)
            lsum += jnp.sum(e.reshape(-1, F32_SUBLANES, 2 * tq), axis=0)
            acc += jnp.dot(vt, e.astype(BF16), preferred_element_type=F32)
        return jnp.sum(lsum, axis=0, keepdims=True), acc

    def finish(l, acc):
        ot = acc[:, :tq] * (1.0 / l[:, :tq]) - acc[:, tq:] * (lam / l[:, tq:])
        scale = lax.rsqrt(jnp.mean(ot * ot, axis=0, keepdims=True) + RMS_EPS)
        o_ref[0] = (ot * scale * (g_ref[...] * (1.0 - LAMBDA_INIT))).astype(BF16).T

    sq = qt.astype(F32) ** 2
    qn = jnp.sqrt(jnp.concatenate([jnp.sum(sq[:HEAD_DIM], axis=0, keepdims=True),
                                   jnp.sum(sq[HEAD_DIM:], axis=0, keepdims=True)], axis=1))
    l, acc = softmax_sums(qn * kn_ref[...])
    finish(l, acc)

    @pl.when(jnp.min(l) < UNDERFLOW_GUARD)
    def _():
        m = None
        for kk, _ in chunks:
            cm = jnp.max(scores(kk), axis=0, keepdims=True)
            m = cm if m is None else jnp.maximum(m, cm)
        finish(*softmax_sums(m))


def _attention(lamvec, subln_g, q, k, kc, v, vc, *, tq, tk):
    b, n, _ = q.shape
    c = kc.shape[1]
    head_block = lambda rows: pl.BlockSpec((1, rows, HEAD_W), lambda bi, hi, qi: (bi, 0, hi))
    return pl.pallas_call(
        functools.partial(_attn_kernel, tk=tk),
        grid=(b, N_HEADS, n // tq),
        in_specs=[_const_spec(lamvec.shape), _const_spec((HEAD_W, 1)),
                  pl.BlockSpec((1, tq, HEAD_W), lambda bi, hi, qi: (bi, qi, hi)),
                  head_block(n), head_block(c), head_block(n), head_block(c)],
        out_specs=pl.BlockSpec((1, tq, HEAD_W), lambda bi, hi, qi: (bi, qi, hi)),
        out_shape=jax.ShapeDtypeStruct((b, n, D_DIFF), BF16),
        scratch_shapes=[pltpu.VMEM((HEAD_W, n), BF16), pltpu.VMEM((HEAD_W, c), BF16),
                        pltpu.VMEM((1, 2 * tq), F32)],
        compiler_params=_params(("parallel", "parallel", "arbitrary")),
        name="diff_attn",
    )(lamvec, subln_g, q, k, kc, v, vc)


def _dft_kernel(yc_ref, ys_ref, m1_ref, twr_ref, twi_ref, w2_ref, wf_ref, perm_ref, o_ref, br_ref, bi_ref):
    n1, cols = yc_ref.shape[0], yc_ref.shape[2]
    g = m1_ref.shape[0] // (2 * n1)
    per_step = yc_ref.shape[1] // g
    n2 = br_ref.shape[0] // n1
    steps1, steps2 = n2 // g // per_step, n1 // per_step
    i = pl.program_id(0)

    @pl.when(i < steps1)
    def _():
        for s in range(per_step):
            group = i * per_step + s
            mine = slice(s * g, (s + 1) * g)
            u = jnp.concatenate([yc_ref[:, mine, :].reshape(n1 * g, cols),
                                 ys_ref[:, mine, :].reshape(n1 * g, cols)], axis=0)
            a = jnp.dot(m1_ref[...], u, preferred_element_type=F32)
            ar, ai = a[:n1 * g], a[n1 * g:]
            tr, ti = twr_ref[group], twi_ref[group]
            br = (ar * tr - ai * ti).astype(BF16)
            bi = (ar * ti + ai * tr).astype(BF16)
            for c in range(n1):
                dst = pl.ds(pl.multiple_of(c * n2 + group * g, g), g)
                br_ref[dst, :] = br[c * g:(c + 1) * g]
                bi_ref[dst, :] = bi[c * g:(c + 1) * g]

    @pl.when((i >= steps1) & (i < steps1 + steps2))
    def _():
        w = wf_ref.shape[0]
        wf = wf_ref[...].astype(BF16)
        for s in range(per_step):
            src = pl.ds(pl.multiple_of(((i - steps1) * per_step + s) * n2, n2), n2)
            b = jnp.concatenate([br_ref[src, :], bi_ref[src, :]], axis=0)
            x = jnp.dot(w2_ref[...], b, preferred_element_type=F32)
            br_ref[src, :] = jnp.concatenate(
                [jnp.dot(x[:, lo:lo + w].astype(BF16), wf, preferred_element_type=F32).astype(BF16)
                 for lo in range(0, cols, w)], axis=1)

    @pl.when(i >= steps1 + steps2)
    def _():
        run = o_ref.shape[0] // n1
        first = (i - steps1 - steps2) * run
        rows = jnp.concatenate([br_ref[pl.ds(pl.multiple_of(c * n2 + first, run), run), :] for c in range(n1)], axis=0)
        o_ref[...] = jnp.dot(perm_ref[...], rows, preferred_element_type=F32).astype(BF16)


def _position_dft(yc, ys, wf, *, tile_rows):
    n, cols = yc.shape
    n1, g, per_step = DFT_N1, BF16_SUBLANES, DFT_UNITS_PER_STEP
    n2 = n // n1
    steps1, steps2, steps3 = n2 // g // per_step, n1 // per_step, n // tile_rows
    m1, twr, twi, w2 = _position_dft_tables(n, n1, g)
    perm = _row_permutation(n1, tile_rows)
    group = pl.BlockSpec((n1, g * per_step, cols), lambda i: (0, jnp.minimum(i, steps1 - 1), 0))
    return pl.pallas_call(
        _dft_kernel,
        grid=(steps1 + steps2 + steps3,),
        in_specs=[group, group, _const_spec(m1.shape), _const_spec(twr.shape), _const_spec(twi.shape),
                  _const_spec(w2.shape), _const_spec(wf.shape), _const_spec(perm.shape)],
        out_specs=pl.BlockSpec((tile_rows, cols), lambda i: (jnp.maximum(i - steps1 - steps2, 0), 0)),
        out_shape=jax.ShapeDtypeStruct((n, cols), BF16),
        scratch_shapes=[pltpu.VMEM((n, cols), BF16), pltpu.VMEM((n, cols), BF16)],
        compiler_params=_params(("arbitrary",)),
        name="position_dft",
    )(yc.reshape(n1, n2, cols), ys.reshape(n1, n2, cols), m1, twr, twi, w2, wf, perm)


def _rope_tables(n):
    rows = n // GRID_W
    lane = np.arange(HEAD_W)
    sub = lane % HEAD_DIM
    on_row_axis = jnp.asarray(sub // (2 * ROPE_PAIRS) == 0)[None, :]
    second_half = jnp.asarray((sub % (2 * ROPE_PAIRS)) // ROPE_PAIRS == 1)[None, :]
    inv_freq = ROPE_BASE ** (-jnp.asarray(sub % ROPE_PAIRS, dtype=F32) / ROPE_PAIRS)
    kinds = ((jnp.cos, True), (lambda a: -jnp.sin(a), ~second_half), (jnp.sin, second_half))

    def side(size, mine):
        ang = jnp.arange(size, dtype=F32)[:, None] * inv_freq[None, :]
        return jnp.stack([jnp.where(mine & keep, fn(ang), 0.0) for fn, keep in kinds])

    return side(rows, on_row_axis), side(GRID_W, ~on_row_axis)


def _channel_dft_table():
    c = np.arange(FOURIER_GROUP_DIM)
    ang = 2.0 * np.pi * np.outer(c, c) / FOURIER_GROUP_DIM
    eye = np.eye(D_FOURIER // FOURIER_GROUP_DIM)
    scale = FOURIER_GROUP_DIM ** -0.5
    return np.concatenate([np.kron(eye, np.cos(ang)), np.kron(eye, np.sin(ang))], axis=1) * scale


def _position_dft_tables(n, n1, g):
    n2 = n // n1
    w1 = np.exp(-2j * np.pi * np.outer(np.arange(n1), np.arange(n1)) / n1)
    m = np.kron(w1, np.eye(g)) * n ** -0.5
    m1 = np.block([[m.real, m.imag], [m.imag, -m.real]])
    b = (g * np.arange(n2 // g)[:, None, None] + np.arange(g)[None, None, :])
    tw = np.exp(-2j * np.pi * b * np.arange(n1)[None, :, None] / n).reshape(n2 // g, n1 * g, 1)
    ang2 = 2.0 * np.pi * np.outer(np.arange(n2), np.arange(n2)) / n2
    w2 = np.concatenate([np.cos(ang2), np.sin(ang2)], axis=1)
    f32 = lambda t: jnp.asarray(t, dtype=F32)
    return f32(m1).astype(BF16), f32(tw.real), f32(tw.imag), f32(w2).astype(BF16)


def _row_permutation(n1, rows):
    d_per_tile = rows // n1
    p = np.zeros((rows, rows), np.float32)
    c, d = np.meshgrid(np.arange(n1), np.arange(d_per_tile), indexing="ij")
    p[(c + n1 * d).ravel(), (c * d_per_tile + d).ravel()] = 1.0
    return jnp.asarray(p).astype(BF16)


def kernel(x, c, ctx, c_ctx, w_ada, b_ada, norm1_g, ffn1_w_gate, ffn1_w_up, ffn1_w_down, norm_mix_g, w_in,
           lambda_q1, lambda_k1, lambda_q2, lambda_k2, subln_g, w_fourier, w_out, norm2_g, ffn2_w_gate,
           ffn2_w_up, ffn2_w_down, final_norm_g):
    b, n, d = x.shape
    n_ctx = ctx.shape[1]
    assert (d, w_ada.shape[0]) == (D_MODEL, 1) and b + 1 <= MOD_ROWS
    tiles = TILES
    mod_row = lambda rows: (lambda i: jnp.minimum(i // (n // rows), b))
    row = lambda g: g.reshape(1, -1)

    cc = jnp.zeros((MOD_ROWS, d), F32).at[:b].set(c).at[b].set(c_ctx)
    mod3 = _ada(cc, w_ada[0], b_ada).reshape(MOD_ROWS, 1, N_MOD * d)

    x1, kc, vc = _ffn(x.reshape(b * n, d), mod3, mod_row(tiles.ffn_rows), row(norm1_g), ffn1_w_gate[0], ffn1_w_up[0],
                      ffn1_w_down[0], tm=tiles.ffn_rows, mod_base=0,
                      tail=(ctx.reshape(b * n_ctx, d), row(norm_mix_g), w_in[0]))

    chan_dft = jnp.asarray(_channel_dft_table(), dtype=F32).astype(BF16)
    q, k, v, yc, ys = _inproj(x1, mod3, mod_row(tiles.inproj_rows), row(norm_mix_g), w_in[0], _rope_tables(n), chan_dft,
                              tm=tiles.inproj_rows)

    lamvec = jnp.concatenate([lambda_q1, lambda_k1, lambda_q2, lambda_k2], axis=0)
    seq = lambda a, rows: a.reshape(b, rows, a.shape[-1])
    att = _attention(lamvec, subln_g.reshape(HEAD_W, 1), seq(q, n), seq(k, n), seq(kc, n_ctx), seq(v, n), seq(vc, n_ctx),
                     tq=tiles.attn_queries, tk=tiles.attn_keys)

    zf = _position_dft(yc, ys, w_fourier[0], tile_rows=tiles.ffn_rows)

    out = _ffn(x1, mod3, mod_row(tiles.ffn_rows), row(norm2_g), ffn2_w_gate[0], ffn2_w_up[0], ffn2_w_down[0],
               tm=tiles.ffn_rows, mod_base=6, mix=(att.reshape(b * n, D_DIFF), zf, w_out[0]),
               tiles_per_seq=n // tiles.ffn_rows, final_g=row(final_norm_g))
    return out.reshape(b, n, d)
```

```python
import functools
import math
from typing import NamedTuple

import numpy as np
import jax
import jax.numpy as jnp
from jax import lax
from jax.experimental import pallas as pl
from jax.experimental.pallas import tpu as pltpu

D_MODEL = 1024
GRID_W = 64
D_FOURIER = 256
D_DIFF = 768
HEAD_DIM = 64
HEAD_W = 2 * HEAD_DIM
N_HEADS = D_DIFF // HEAD_W
FOURIER_GROUP_DIM = 64
D_IN_PROJ = 3 * D_DIFF + D_FOURIER
D_FF = 2816
N_MOD = 9
ROPE_BASE = 10000.0
ROPE_PAIRS = HEAD_DIM // 4
RMS_EPS = 1e-6
ATTN_SCALE = HEAD_DIM ** -0.5
Q_SCALE = ATTN_SCALE * math.log2(math.e)
LAMBDA_INIT = 0.8 - 0.6 * math.exp(-0.3 * 0)
UNDERFLOW_GUARD = 2.0 ** -80

F32 = jnp.float32
BF16 = jnp.bfloat16

V7X_VMEM_LIMIT_BYTES = 56 * 1024 * 1024
F32_SUBLANES = 8
BF16_SUBLANES = 16
STAGE_CHUNK_BYTES = 3 * 1024 * 1024
MOD_ROWS = F32_SUBLANES
DFT_N1 = 16
DFT_UNITS_PER_STEP = 2


class _Tiles(NamedTuple):
    ffn_rows: int = 512
    inproj_rows: int = 1024
    attn_queries: int = 1024
    attn_keys: int = 1024


TILES = _Tiles()
FF_CHUNKS = ((0, 1536), (1536, 1280))


def _const_spec(shape):
    return pl.BlockSpec(shape, lambda *_: (0,) * len(shape), pipeline_mode=pl.Buffered(1))


def _params(semantics):
    return pltpu.CompilerParams(dimension_semantics=semantics, vmem_limit_bytes=V7X_VMEM_LIMIT_BYTES)


def _rmsnorm(x, g):
    return x * lax.rsqrt(jnp.mean(x * x, axis=-1, keepdims=True) + RMS_EPS) * g


def _mod_row(mod_ref, k):
    return mod_ref[0, :, k * D_MODEL:(k + 1) * D_MODEL]


def _ada_kernel(cc_ref, w_ref, b_ref, o_ref):
    cc = cc_ref[...]
    s = cc * jax.nn.sigmoid(cc)
    o_ref[...] = jnp.dot(s.astype(BF16), w_ref[...].astype(BF16), preferred_element_type=F32) + b_ref[...]


def _ada(cc, w_ada, b_ada):
    n_out = w_ada.shape[1]
    tn = n_out // 4
    return pl.pallas_call(
        _ada_kernel,
        grid=(n_out // tn,),
        in_specs=[_const_spec((MOD_ROWS, D_MODEL)),
                  pl.BlockSpec((D_MODEL, tn), lambda j: (0, j)),
                  pl.BlockSpec((1, tn), lambda j: (0, j))],
        out_specs=pl.BlockSpec((MOD_ROWS, tn), lambda j: (0, j)),
        out_shape=jax.ShapeDtypeStruct((MOD_ROWS, n_out), F32),
        compiler_params=_params(("arbitrary",)),
        name="ada",
    )(cc, w_ada, b_ada)


def _stage_rows(k, n):
    fits = [r for r in range(BF16_SUBLANES, k + 1, BF16_SUBLANES) if k % r == 0 and r * n * 4 <= STAGE_CHUNK_BYTES]
    return max(fits)


def _stage_weight(src_hbm, first_col, dst_ref, stage_ref, sem_ref):
    rows, width = stage_ref.shape[1], dst_ref.shape[1]
    n_chunks = src_hbm.shape[0] // rows

    def copy(c):
        return pltpu.make_async_copy(src_hbm.at[pl.ds(c * rows, rows), pl.ds(first_col, width)],
                                     stage_ref.at[c % 2, :, pl.ds(0, width)], sem_ref.at[c % 2])

    copy(0).start()
    for c in range(n_chunks):
        if c + 1 < n_chunks:
            copy(c + 1).start()
        copy(c).wait()
        dst_ref[c * rows:(c + 1) * rows, :] = stage_ref[c % 2, :, 0:width].astype(BF16)


class _StagedWeights:
    def __init__(self, weights):
        entries = [w if isinstance(w, tuple) else (w, (0, w.shape[1])) for w in weights]
        self.weights = [w for w, _ in entries]
        self.first_cols = [first for _, (first, _) in entries]
        shapes = [(w.shape[0], width) for w, (_, width) in entries]
        self.in_specs = [pl.BlockSpec(memory_space=pl.ANY)] * len(self.weights)
        stages, self.stage_of = [], [None] * len(shapes)
        for j in sorted(range(len(shapes)), key=lambda j: -shapes[j][1]):
            k, n = shapes[j]
            fits = [s for s, (rows, width) in enumerate(stages)
                    if width >= n and k % rows == 0 and 2 * rows >= _stage_rows(k, n)]
            if not fits:
                stages.append((_stage_rows(k, n), n))
                fits = [len(stages) - 1]
            self.stage_of[j] = fits[0]
        self.scratch = ([pltpu.VMEM(s, BF16) for s in shapes]
                        + [pltpu.VMEM((2,) + s, F32) for s in stages]
                        + [pltpu.SemaphoreType.DMA((2,))])

    def load(self, hbm_refs, scratch_refs):
        n = len(self.weights)
        dst, stages, sem = scratch_refs[:n], scratch_refs[n:-1], scratch_refs[-1]

        @pl.when(pl.program_id(0) == 0)
        def _():
            for src, first, d, s in zip(hbm_refs, self.first_cols, dst, self.stage_of):
                _stage_weight(src, first, d, stages[s], sem)

        return dst


def _ffn_kernel(*refs, staged, mod_base, has_mix, final_norm, n_main):
    n_w = len(staged.weights)
    n_scratch = len(staged.scratch)
    x_ref, mod_ref, g_ref = refs[:3]
    w_hbm = refs[3:3 + n_w]
    rest = list(refs[3 + n_w:len(refs) - n_scratch])
    weights = staged.load(w_hbm, refs[len(refs) - n_scratch:])
    wg_ref, wu_ref, wd_ref = weights[:3]
    has_tail = n_main is not None
    if has_tail:
        tail_ref, tail_g_ref = rest[:2]
        kc_ref, vc_ref = rest[-2:]
        rest = rest[2:-2]
        wkv_ref = weights[-1]
        in_tail = pl.program_id(0) >= n_main
    if has_mix:
        att_ref, zf_ref = rest[:2]
        rest = rest[2:]
        wo_ref = weights[3]
    if final_norm:
        gf_ref = rest[0]
        rest = rest[1:]
    (o_ref,) = rest

    x = x_ref[...]
    if has_tail:
        x = jnp.where(in_tail, tail_ref[...], x)
    if has_mix:
        mix = jnp.dot(att_ref[...], wo_ref[:D_DIFF, :], preferred_element_type=F32)
        mix += jnp.dot(zf_ref[...], wo_ref[D_DIFF:, :], preferred_element_type=F32)
        x = x + _mod_row(mod_ref, 5) * mix
    h = _rmsnorm(x, g_ref[...]) * (1.0 + _mod_row(mod_ref, mod_base + 1)) + _mod_row(mod_ref, mod_base)
    h = h.astype(BF16)
    acc = None
    for start, width in FF_CHUNKS:
        gate = jnp.dot(h, wg_ref[:, start:start + width], preferred_element_type=F32)
        up = jnp.dot(h, wu_ref[:, start:start + width], preferred_element_type=F32)
        a = (gate * jax.nn.sigmoid(gate) * up).astype(BF16)
        part = jnp.dot(a, wd_ref[start:start + width, :], preferred_element_type=F32)
        acc = part if acc is None else acc + part
    x = x + (0.5 * _mod_row(mod_ref, mod_base + 2)) * acc
    if final_norm:
        x = _rmsnorm(x, gf_ref[...])
    if has_tail:
        @pl.when(in_tail)
        def _():
            hc = _rmsnorm(x, tail_g_ref[...]) * (1.0 + _mod_row(mod_ref, 4)) + _mod_row(mod_ref, 3)
            kv = jnp.dot(hc.astype(BF16), wkv_ref[...], preferred_element_type=F32)
            kc_ref[...] = kv[:, :D_DIFF].astype(BF16)
            vc_ref[...] = kv[:, D_DIFF:].astype(BF16)

        @pl.when(jnp.logical_not(in_tail))
        def _():
            o_ref[...] = x
    else:
        o_ref[...] = x


def _ffn(x, mod3, mod_index, g, wg, wu, wd, *, tm, mod_base, tail=None, mix=None, tiles_per_seq=None, final_g=None):
    t = x.shape[0]
    n_main = t // tm
    n_tiles = n_main + (tail[0].shape[0] // tm if tail is not None else 0)
    tile = lambda w: pl.BlockSpec((tm, w), lambda i: (jnp.minimum(i, n_main - 1), 0))
    tail_tile = lambda w: pl.BlockSpec((tm, w), lambda i: (jnp.maximum(i - n_main, 0), 0))
    staged = _StagedWeights([wg, wu, wd] + ([mix[2]] if mix is not None else [])
                            + ([(tail[2], (D_DIFF, 2 * D_DIFF))] if tail is not None else []))
    in_specs = [tile(D_MODEL),
                pl.BlockSpec((1, 1, N_MOD * D_MODEL), lambda i: (mod_index(i), 0, 0)),
                _const_spec((1, D_MODEL))] + staged.in_specs
    args = [x, mod3, g] + staged.weights
    out_specs, out_shape = tile(D_MODEL), jax.ShapeDtypeStruct((t, D_MODEL), F32)
    if tail is not None:
        ctx, tail_g, _ = tail
        in_specs += [tail_tile(D_MODEL), _const_spec((1, D_MODEL))]
        args += [ctx, tail_g]
        out_specs = [out_specs, tail_tile(D_DIFF), tail_tile(D_DIFF)]
        out_shape = [out_shape] + [jax.ShapeDtypeStruct((ctx.shape[0], D_DIFF), BF16)] * 2
    if mix is not None:
        att, zf, _ = mix
        by_batch = pl.BlockSpec((tm, D_FOURIER), lambda i: (i % tiles_per_seq, i // tiles_per_seq))
        in_specs += [tile(D_DIFF), by_batch]
        args += [att, zf]
    if final_g is not None:
        in_specs.append(_const_spec((1, D_MODEL)))
        args.append(final_g)
    kern = functools.partial(_ffn_kernel, staged=staged, mod_base=mod_base, has_mix=mix is not None,
                             final_norm=final_g is not None, n_main=n_main if tail is not None else None)
    return pl.pallas_call(
        kern,
        grid=(n_tiles,),
        in_specs=in_specs,
        out_specs=out_specs,
        out_shape=out_shape,
        scratch_shapes=staged.scratch,
        compiler_params=_params(("arbitrary",)),
        name="ffn_mix" if mix is not None else "ffn",
    )(*args)


def _rope(x, cos, sin_lo, sin_hi):
    return (x * cos + pltpu.roll(x, HEAD_W - ROPE_PAIRS, 1) * sin_lo
            + pltpu.roll(x, ROPE_PAIRS, 1) * sin_hi)


def _inproj_kernel(x_ref, mod_ref, g_ref, w_hbm, by_row_ref, by_col_ref, dft_ref,
                   q_ref, k_ref, v_ref, yc_ref, ys_ref, *scratch, staged, tiles_per_seq):
    (w_ref,) = staged.load([w_hbm], scratch)
    h = _rmsnorm(x_ref[...], g_ref[...]) * (1.0 + _mod_row(mod_ref, 4)) + _mod_row(mod_ref, 3)
    p = jnp.dot(h.astype(BF16), w_ref[...], preferred_element_type=F32)
    tm, grid_w = x_ref.shape[0], by_col_ref.shape[1]
    first_row = pl.multiple_of((pl.program_id(0) % tiles_per_seq) * (tm // grid_w), tm // grid_w)
    cos, slo, shi = [(by_row_ref[j, pl.ds(first_row, tm // grid_w), :][:, None, :]
                      + by_col_ref[j][None, :, :]).reshape(tm, HEAD_W) for j in range(3)]
    for hd in range(N_HEADS):
        lo = hd * HEAD_W
        q_ref[:, lo:lo + HEAD_W] = (_rope(p[:, lo:lo + HEAD_W], cos, slo, shi) * Q_SCALE).astype(BF16)
        k_ref[:, lo:lo + HEAD_W] = _rope(p[:, D_DIFF + lo:D_DIFF + lo + HEAD_W], cos, slo, shi).astype(BF16)
    v_ref[...] = p[:, 2 * D_DIFF:3 * D_DIFF].astype(BF16)
    y = jnp.dot(p[:, 3 * D_DIFF:].astype(BF16), dft_ref[...], preferred_element_type=F32)
    yc_ref[...] = y[:, :D_FOURIER].astype(BF16)
    ys_ref[...] = y[:, D_FOURIER:].astype(BF16)


def _inproj(x, mod3, mod_index, g, w_in, rope, chan_dft, *, tm):
    t = x.shape[0]
    by_row, by_col = rope
    tiles_per_seq = by_row.shape[1] * by_col.shape[1] // tm
    tile = lambda w: pl.BlockSpec((tm, w), lambda i: (i, 0))
    by_batch = pl.BlockSpec((tm, D_FOURIER), lambda i: (i % tiles_per_seq, i // tiles_per_seq))
    staged = _StagedWeights([w_in])
    wide = jax.ShapeDtypeStruct((t, D_DIFF), BF16)
    narrow = jax.ShapeDtypeStruct((tiles_per_seq * tm, t // (tiles_per_seq * tm) * D_FOURIER), BF16)
    return pl.pallas_call(
        functools.partial(_inproj_kernel, staged=staged, tiles_per_seq=tiles_per_seq),
        grid=(t // tm,),
        in_specs=[tile(D_MODEL),
                  pl.BlockSpec((1, 1, N_MOD * D_MODEL), lambda i: (mod_index(i), 0, 0)),
                  _const_spec((1, D_MODEL))] + staged.in_specs
                 + [_const_spec(by_row.shape), _const_spec(by_col.shape), _const_spec((D_FOURIER, 2 * D_FOURIER))],
        out_specs=[tile(D_DIFF)] * 3 + [by_batch] * 2,
        out_shape=[wide] * 3 + [narrow] * 2,
        scratch_shapes=staged.scratch,
        compiler_params=_params(("arbitrary",)),
        name="inproj",
    )(x, mod3, g, w_in, by_row, by_col, chan_dft)


def _max_key_norm_sq(kk):
    sq = (kk.astype(F32) ** 2).astype(BF16)
    r = lax.broadcasted_iota(jnp.int32, (HEAD_W, HEAD_W), 0)
    c = lax.broadcasted_iota(jnp.int32, (HEAD_W, HEAD_W), 1)
    sel = jnp.where((c == 0) == (r < HEAD_DIM), 1.0, 0.0) * jnp.where(c < 2, 1.0, 0.0)
    sums = jnp.dot(sq, sel.astype(BF16), preferred_element_type=F32)
    biggest = jnp.max(sums, axis=0, keepdims=True) * (1.0 + 2.0 ** -7)
    return biggest[:, 0:1], biggest[:, 1:2]


def _attn_kernel(lam_ref, g_ref, q_ref, k_ref, kc_ref, v_ref, vc_ref, o_ref, vt_ref, vct_ref, kn_ref, *, tk):
    tq = q_ref.shape[1]

    @pl.when(pl.program_id(2) == 0)
    def _():
        vt_ref[...] = v_ref[0].T
        vct_ref[...] = vc_ref[0].T
        (a1, a2), (b1, b2) = _max_key_norm_sq(k_ref[0]), _max_key_norm_sq(kc_ref[0])
        kn_ref[...] = jnp.sqrt(jnp.concatenate([jnp.broadcast_to(jnp.maximum(a1, b1), (1, tq)),
                                                 jnp.broadcast_to(jnp.maximum(a2, b2), (1, tq))], axis=1))

    lv = lam_ref[...]
    lam = (jnp.exp(jnp.sum(lv[0:1] * lv[1:2], axis=1, keepdims=True))
           - jnp.exp(jnp.sum(lv[2:3] * lv[3:4], axis=1, keepdims=True)) + LAMBDA_INIT)

    qt = q_ref[0].T
    none = jnp.zeros((HEAD_DIM, tq), BF16)
    qcat = jnp.concatenate([jnp.concatenate([qt[:HEAD_DIM], none], axis=0),
                            jnp.concatenate([none, qt[HEAD_DIM:]], axis=0)], axis=1)
    chunks = [(k_ref[0, c * tk:(c + 1) * tk, :], vt_ref[:, c * tk:(c + 1) * tk]) for c in range(k_ref.shape[1] // tk)]
    chunks.append((kc_ref[0], vct_ref[...]))

    def scores(kk):
        return jnp.dot(kk, qcat, preferred_element_type=F32)

    def softmax_sums(shift):
        lsum, acc = 0.0, 0.0
        for kk, vt in chunks:
            e = jnp.exp2(scores(kk) - shift)
            lsum += jnp.sum(e.reshape(-1, F32_SUBLANES, 2 * tq), axis=0)
            acc += jnp.dot(vt, e.astype(BF16), preferred_element_type=F32)
        return jnp.sum(lsum, axis=0, keepdims=True), acc

    def finish(l, acc):
        ot = acc[:, :tq] * (1.0 / l[:, :tq]) - acc[:, tq:] * (lam / l[:, tq:])
        scale = lax.rsqrt(jnp.mean(ot * ot, axis=0, keepdims=True) + RMS_EPS)
        o_ref[0] = (ot * scale * (g_ref[...] * (1.0 - LAMBDA_INIT))).astype(BF16).T

    sq = qt.astype(F32) ** 2
    qn = jnp.sqrt(jnp.concatenate([jnp.sum(sq[:HEAD_DIM], axis=0, keepdims=True),
                                   jnp.sum(sq[HEAD_DIM:], axis=0, keepdims=True)], axis=1))
    l, acc = softmax_sums(qn * kn_ref[...])
    finish(l, acc)

    @pl.when(jnp.min(l) < UNDERFLOW_GUARD)
    def _():
        m = None
        for kk, _ in chunks:
            cm = jnp.max(scores(kk), axis=0, keepdims=True)
            m = cm if m is None else jnp.maximum(m, cm)
        finish(*softmax_sums(m))


def _attention(lamvec, subln_g, q, k, kc, v, vc, *, tq, tk):
    b, n, _ = q.shape
    c = kc.shape[1]
    head_block = lambda rows: pl.BlockSpec((1, rows, HEAD_W), lambda bi, hi, qi: (bi, 0, hi))
    return pl.pallas_call(
        functools.partial(_attn_kernel, tk=tk),
        grid=(b, N_HEADS, n // tq),
        in_specs=[_const_spec(lamvec.shape), _const_spec((HEAD_W, 1)),
                  pl.BlockSpec((1, tq, HEAD_W), lambda bi, hi, qi: (bi, qi, hi)),
                  head_block(n), head_block(c), head_block(n), head_block(c)],
        out_specs=pl.BlockSpec((1, tq, HEAD_W), lambda bi, hi, qi: (bi, qi, hi)),
        out_shape=jax.ShapeDtypeStruct((b, n, D_DIFF), BF16),
        scratch_shapes=[pltpu.VMEM((HEAD_W, n), BF16), pltpu.VMEM((HEAD_W, c), BF16),
                        pltpu.VMEM((1, 2 * tq), F32)],
        compiler_params=_params(("parallel", "parallel", "arbitrary")),
        name="diff_attn",
    )(lamvec, subln_g, q, k, kc, v, vc)


def _dft_kernel(yc_ref, ys_ref, m1_ref, twr_ref, twi_ref, w2_ref, wf_ref, perm_ref, o_ref, br_ref, bi_ref):
    n1, cols = yc_ref.shape[0], yc_ref.shape[2]
    g = m1_ref.shape[0] // (2 * n1)
    per_step = yc_ref.shape[1] // g
    n2 = br_ref.shape[0] // n1
    steps1, steps2 = n2 // g // per_step, n1 // per_step
    i = pl.program_id(0)

    @pl.when(i < steps1)
    def _():
        for s in range(per_step):
            group = i * per_step + s
            mine = slice(s * g, (s + 1) * g)
            u = jnp.concatenate([yc_ref[:, mine, :].reshape(n1 * g, cols),
                                 ys_ref[:, mine, :].reshape(n1 * g, cols)], axis=0)
            a = jnp.dot(m1_ref[...], u, preferred_element_type=F32)
            ar, ai = a[:n1 * g], a[n1 * g:]
            tr, ti = twr_ref[group], twi_ref[group]
            br = (ar * tr - ai * ti).astype(BF16)
            bi = (ar * ti + ai * tr).astype(BF16)
            for c in range(n1):
                dst = pl.ds(pl.multiple_of(c * n2 + group * g, g), g)
                br_ref[dst, :] = br[c * g:(c + 1) * g]
                bi_ref[dst, :] = bi[c * g:(c + 1) * g]

    @pl.when((i >= steps1) & (i < steps1 + steps2))
    def _():
        w = wf_ref.shape[0]
        wf = wf_ref[...].astype(BF16)
        for s in range(per_step):
            src = pl.ds(pl.multiple_of(((i - steps1) * per_step + s) * n2, n2), n2)
            b = jnp.concatenate([br_ref[src, :], bi_ref[src, :]], axis=0)
            x = jnp.dot(w2_ref[...], b, preferred_element_type=F32)
            br_ref[src, :] = jnp.concatenate(
                [jnp.dot(x[:, lo:lo + w].astype(BF16), wf, preferred_element_type=F32).astype(BF16)
                 for lo in range(0, cols, w)], axis=1)

    @pl.when(i >= steps1 + steps2)
    def _():
        run = o_ref.shape[0] // n1
        first = (i - steps1 - steps2) * run
        rows = jnp.concatenate([br_ref[pl.ds(pl.multiple_of(c * n2 + first, run), run), :] for c in range(n1)], axis=0)
        o_ref[...] = jnp.dot(perm_ref[...], rows, preferred_element_type=F32).astype(BF16)


def _position_dft(yc, ys, wf, *, tile_rows):
    n, cols = yc.shape
    n1, g, per_step = DFT_N1, BF16_SUBLANES, DFT_UNITS_PER_STEP
    n2 = n // n1
    steps1, steps2, steps3 = n2 // g // per_step, n1 // per_step, n // tile_rows
    m1, twr, twi, w2 = _position_dft_tables(n, n1, g)
    perm = _row_permutation(n1, tile_rows)
    group = pl.BlockSpec((n1, g * per_step, cols), lambda i: (0, jnp.minimum(i, steps1 - 1), 0))
    return pl.pallas_call(
        _dft_kernel,
        grid=(steps1 + steps2 + steps3,),
        in_specs=[group, group, _const_spec(m1.shape), _const_spec(twr.shape), _const_spec(twi.shape),
                  _const_spec(w2.shape), _const_spec(wf.shape), _const_spec(perm.shape)],
        out_specs=pl.BlockSpec((tile_rows, cols), lambda i: (jnp.maximum(i - steps1 - steps2, 0), 0)),
        out_shape=jax.ShapeDtypeStruct((n, cols), BF16),
        scratch_shapes=[pltpu.VMEM((n, cols), BF16), pltpu.VMEM((n, cols), BF16)],
        compiler_params=_params(("arbitrary",)),
        name="position_dft",
    )(yc.reshape(n1, n2, cols), ys.reshape(n1, n2, cols), m1, twr, twi, w2, wf, perm)


def _rope_tables(n):
    rows = n // GRID_W
    lane = np.arange(HEAD_W)
    sub = lane % HEAD_DIM
    on_row_axis = jnp.asarray(sub // (2 * ROPE_PAIRS) == 0)[None, :]
    second_half = jnp.asarray((sub % (2 * ROPE_PAIRS)) // ROPE_PAIRS == 1)[None, :]
    inv_freq = ROPE_BASE ** (-jnp.asarray(sub % ROPE_PAIRS, dtype=F32) / ROPE_PAIRS)
    kinds = ((jnp.cos, True), (lambda a: -jnp.sin(a), ~second_half), (jnp.sin, second_half))

    def side(size, mine):
        ang = jnp.arange(size, dtype=F32)[:, None] * inv_freq[None, :]
        return jnp.stack([jnp.where(mine & keep, fn(ang), 0.0) for fn, keep in kinds])

    return side(rows, on_row_axis), side(GRID_W, ~on_row_axis)


def _channel_dft_table():
    c = np.arange(FOURIER_GROUP_DIM)
    ang = 2.0 * np.pi * np.outer(c, c) / FOURIER_GROUP_DIM
    eye = np.eye(D_FOURIER // FOURIER_GROUP_DIM)
    scale = FOURIER_GROUP_DIM ** -0.5
    return np.concatenate([np.kron(eye, np.cos(ang)), np.kron(eye, np.sin(ang))], axis=1) * scale


def _position_dft_tables(n, n1, g):
    n2 = n // n1
    w1 = np.exp(-2j * np.pi * np.outer(np.arange(n1), np.arange(n1)) / n1)
    m = np.kron(w1, np.eye(g)) * n ** -0.5
    m1 = np.block([[m.real, m.imag], [m.imag, -m.real]])
    b = (g * np.arange(n2 // g)[:, None, None] + np.arange(g)[None, None, :])
    tw = np.exp(-2j * np.pi * b * np.arange(n1)[None, :, None] / n).reshape(n2 // g, n1 * g, 1)
    ang2 = 2.0 * np.pi * np.outer(np.arange(n2), np.arange(n2)) / n2
    w2 = np.concatenate([np.cos(ang2), np.sin(ang2)], axis=1)
    f32 = lambda t: jnp.asarray(t, dtype=F32)
    return f32(m1).astype(BF16), f32(tw.real), f32(tw.imag), f32(w2).astype(BF16)


def _row_permutation(n1, rows):
    d_per_tile = rows // n1
    p = np.zeros((rows, rows), np.float32)
    c, d = np.meshgrid(np.arange(n1), np.arange(d_per_tile), indexing="ij")
    p[(c + n1 * d).ravel(), (c * d_per_tile + d).ravel()] = 1.0
    return jnp.asarray(p).astype(BF16)


def kernel(x, c, ctx, c_ctx, w_ada, b_ada, norm1_g, ffn1_w_gate, ffn1_w_up, ffn1_w_down, norm_mix_g, w_in,
           lambda_q1, lambda_k1, lambda_q2, lambda_k2, subln_g, w_fourier, w_out, norm2_g, ffn2_w_gate,
           ffn2_w_up, ffn2_w_down, final_norm_g):
    b, n, d = x.shape
    n_ctx = ctx.shape[1]
    assert (d, w_ada.shape[0]) == (D_MODEL, 1) and b + 1 <= MOD_ROWS
    tiles = TILES
    mod_row = lambda rows: (lambda i: jnp.minimum(i // (n // rows), b))
    row = lambda g: g.reshape(1, -1)

    cc = jnp.zeros((MOD_ROWS, d), F32).at[:b].set(c).at[b].set(c_ctx)
    mod3 = _ada(cc, w_ada[0], b_ada).reshape(MOD_ROWS, 1, N_MOD * d)

    x1, kc, vc = _ffn(x.reshape(b * n, d), mod3, mod_row(tiles.ffn_rows), row(norm1_g), ffn1_w_gate[0], ffn1_w_up[0],
                      ffn1_w_down[0], tm=tiles.ffn_rows, mod_base=0,
                      tail=(ctx.reshape(b * n_ctx, d), row(norm_mix_g), w_in[0]))

    chan_dft = jnp.asarray(_channel_dft_table(), dtype=F32).astype(BF16)
    q, k, v, yc, ys = _inproj(x1, mod3, mod_row(tiles.inproj_rows), row(norm_mix_g), w_in[0], _rope_tables(n), chan_dft,
                              tm=tiles.inproj_rows)

    lamvec = jnp.concatenate([lambda_q1, lambda_k1, lambda_q2, lambda_k2], axis=0)
    seq = lambda a, rows: a.reshape(b, rows, a.shape[-1])
    att = _attention(lamvec, subln_g.reshape(HEAD_W, 1), seq(q, n), seq(k, n), seq(kc, n_ctx), seq(v, n), seq(vc, n_ctx),
                     tq=tiles.attn_queries, tk=tiles.attn_keys)

    zf = _position_dft(yc, ys, w_fourier[0], tile_rows=tiles.ffn_rows)

    out = _ffn(x1, mod3, mod_row(tiles.ffn_rows), row(norm2_g), ffn2_w_gate[0], ffn2_w_up[0], ffn2_w_down[0],
               tm=tiles.ffn_rows, mod_base=6, mix=(att.reshape(b * n, D_DIFF), zf, w_out[0]),
               tiles_per_seq=n // tiles.ffn_rows, final_g=row(final_norm_g))
    return out.reshape(b, n, d)
```

```python
import functools
import math
from typing import NamedTuple

import numpy as np
import jax
import jax.numpy as jnp
from jax import lax
from jax.experimental import pallas as pl
from jax.experimental.pallas import tpu as pltpu

D_MODEL = 1024
GRID_W = 64
D_FOURIER = 256
D_DIFF = 768
HEAD_DIM = 64
HEAD_W = 2 * HEAD_DIM
N_HEADS = D_DIFF // HEAD_W
FOURIER_GROUP_DIM = 64
D_IN_PROJ = 3 * D_DIFF + D_FOURIER
D_FF = 2816
N_MOD = 9
ROPE_BASE = 10000.0
ROPE_PAIRS = HEAD_DIM // 4
RMS_EPS = 1e-6
ATTN_SCALE = HEAD_DIM ** -0.5
Q_SCALE = ATTN_SCALE * math.log2(math.e)
LAMBDA_INIT = 0.8 - 0.6 * math.exp(-0.3 * 0)
UNDERFLOW_GUARD = 2.0 ** -80

F32 = jnp.float32
BF16 = jnp.bfloat16

V7X_VMEM_LIMIT_BYTES = 56 * 1024 * 1024
F32_SUBLANES = 8
BF16_SUBLANES = 16
STAGE_CHUNK_BYTES = 3 * 1024 * 1024
MOD_ROWS = F32_SUBLANES
ADA_ROWS_PER_STEP = 128
DFT_N1 = 16
DFT_UNITS_PER_STEP = 2


class _Tiles(NamedTuple):
    ffn_rows: int = 512
    inproj_rows: int = 1024
    attn_queries: int = 1024
    attn_keys: int = 1024


TILES = _Tiles()
FF_CHUNKS = ((0, 1536), (1536, 1280))


def _const_spec(shape):
    return pl.BlockSpec(shape, lambda *_: (0,) * len(shape), pipeline_mode=pl.Buffered(1))


def _params(semantics):
    return pltpu.CompilerParams(dimension_semantics=semantics, vmem_limit_bytes=V7X_VMEM_LIMIT_BYTES)


def _rmsnorm(x, g):
    return x * lax.rsqrt(jnp.mean(x * x, axis=-1, keepdims=True) + RMS_EPS) * g


def _mod_row(mod_ref, k):
    return mod_ref[0, :, k * D_MODEL:(k + 1) * D_MODEL]


def _ada_kernel(cc_ref, w_ref, b_ref, o_ref):
    @pl.when(pl.program_id(0) == 0)
    def _():
        o_ref[...] = jnp.broadcast_to(b_ref[...], o_ref.shape)

    cc = cc_ref[...]
    s = cc * jax.nn.sigmoid(cc)
    o_ref[...] += jnp.dot(s.astype(BF16), w_ref[...].astype(BF16), preferred_element_type=F32)


def _ada(cc, w_ada, b_ada):
    k, n_out = w_ada.shape
    tk = ADA_ROWS_PER_STEP
    return pl.pallas_call(
        _ada_kernel,
        grid=(k // tk,),
        in_specs=[pl.BlockSpec((MOD_ROWS, tk), lambda j: (0, j)),
                  pl.BlockSpec((tk, n_out), lambda j: (j, 0)),
                  _const_spec((1, n_out))],
        out_specs=pl.BlockSpec((MOD_ROWS, n_out), lambda j: (0, 0)),
        out_shape=jax.ShapeDtypeStruct((MOD_ROWS, n_out), F32),
        compiler_params=_params(("arbitrary",)),
        name="ada",
    )(cc, w_ada, b_ada)


def _stage_rows(k, n):
    fits = [r for r in range(BF16_SUBLANES, k + 1, BF16_SUBLANES) if k % r == 0 and r * n * 4 <= STAGE_CHUNK_BYTES]
    return max(fits)


def _stage_weight(src_hbm, first_col, dst_ref, stage_ref, sem_ref):
    rows, width = stage_ref.shape[1], dst_ref.shape[1]
    n_chunks = src_hbm.shape[0] // rows

    def copy(c):
        return pltpu.make_async_copy(src_hbm.at[pl.ds(c * rows, rows), pl.ds(first_col, width)],
                                     stage_ref.at[c % 2, :, pl.ds(0, width)], sem_ref.at[c % 2])

    copy(0).start()
    for c in range(n_chunks):
        if c + 1 < n_chunks:
            copy(c + 1).start()
        copy(c).wait()
        dst_ref[c * rows:(c + 1) * rows, :] = stage_ref[c % 2, :, 0:width].astype(BF16)


class _StagedWeights:
    def __init__(self, weights):
        entries = [w if isinstance(w, tuple) else (w, (0, w.shape[1])) for w in weights]
        self.weights = [w for w, _ in entries]
        self.first_cols = [first for _, (first, _) in entries]
        shapes = [(w.shape[0], width) for w, (_, width) in entries]
        self.in_specs = [pl.BlockSpec(memory_space=pl.ANY)] * len(self.weights)
        stages, self.stage_of = [], [None] * len(shapes)
        for j in sorted(range(len(shapes)), key=lambda j: -shapes[j][1]):
            k, n = shapes[j]
            fits = [s for s, (rows, width) in enumerate(stages)
                    if width >= n and k % rows == 0 and 2 * rows >= _stage_rows(k, n)]
            if not fits:
                stages.append((_stage_rows(k, n), n))
                fits = [len(stages) - 1]
            self.stage_of[j] = fits[0]
        self.scratch = ([pltpu.VMEM(s, BF16) for s in shapes]
                        + [pltpu.VMEM((2,) + s, F32) for s in stages]
                        + [pltpu.SemaphoreType.DMA((2,))])

    def load(self, hbm_refs, scratch_refs):
        n = len(self.weights)
        dst, stages, sem = scratch_refs[:n], scratch_refs[n:-1], scratch_refs[-1]

        @pl.when(pl.program_id(0) == 0)
        def _():
            for src, first, d, s in zip(hbm_refs, self.first_cols, dst, self.stage_of):
                _stage_weight(src, first, d, stages[s], sem)

        return dst


def _ffn_kernel(*refs, staged, mod_base, has_mix, final_norm, n_main):
    n_w = len(staged.weights)
    n_scratch = len(staged.scratch)
    x_ref, mod_ref, g_ref = refs[:3]
    w_hbm = refs[3:3 + n_w]
    rest = list(refs[3 + n_w:len(refs) - n_scratch])
    weights = staged.load(w_hbm, refs[len(refs) - n_scratch:])
    wg_ref, wu_ref, wd_ref = weights[:3]
    has_tail = n_main is not None
    if has_tail:
        tail_ref, tail_g_ref = rest[:2]
        kc_ref, vc_ref = rest[-2:]
        rest = rest[2:-2]
        wkv_ref = weights[-1]
        in_tail = pl.program_id(0) >= n_main
    if has_mix:
        att_ref, zf_ref = rest[:2]
        rest = rest[2:]
        wo_ref = weights[3]
    if final_norm:
        gf_ref = rest[0]
        rest = rest[1:]
    (o_ref,) = rest

    x = x_ref[...]
    if has_tail:
        x = jnp.where(in_tail, tail_ref[...], x)
    if has_mix:
        mix = jnp.dot(att_ref[...], wo_ref[:D_DIFF, :], preferred_element_type=F32)
        mix += jnp.dot(zf_ref[...], wo_ref[D_DIFF:, :], preferred_element_type=F32)
        x = x + _mod_row(mod_ref, 5) * mix
    h = _rmsnorm(x, g_ref[...]) * (1.0 + _mod_row(mod_ref, mod_base + 1)) + _mod_row(mod_ref, mod_base)
    h = h.astype(BF16)
    acc = None
    for start, width in FF_CHUNKS:
        gate = jnp.dot(h, wg_ref[:, start:start + width], preferred_element_type=F32)
        up = jnp.dot(h, wu_ref[:, start:start + width], preferred_element_type=F32)
        a = (gate * jax.nn.sigmoid(gate) * up).astype(BF16)
        part = jnp.dot(a, wd_ref[start:start + width, :], preferred_element_type=F32)
        acc = part if acc is None else acc + part
    x = x + (0.5 * _mod_row(mod_ref, mod_base + 2)) * acc
    if final_norm:
        x = _rmsnorm(x, gf_ref[...])
    if has_tail:
        @pl.when(in_tail)
        def _():
            hc = _rmsnorm(x, tail_g_ref[...]) * (1.0 + _mod_row(mod_ref, 4)) + _mod_row(mod_ref, 3)
            kv = jnp.dot(hc.astype(BF16), wkv_ref[...], preferred_element_type=F32)
            kc_ref[...] = kv[:, :D_DIFF].astype(BF16)
            vc_ref[...] = kv[:, D_DIFF:].astype(BF16)

        @pl.when(jnp.logical_not(in_tail))
        def _():
            o_ref[...] = x
    else:
        o_ref[...] = x


def _ffn(x, mod3, mod_index, g, wg, wu, wd, *, tm, mod_base, tail=None, mix=None, tiles_per_seq=None, final_g=None):
    t = x.shape[0]
    n_main = t // tm
    n_tiles = n_main + (tail[0].shape[0] // tm if tail is not None else 0)
    tile = lambda w: pl.BlockSpec((tm, w), lambda i: (jnp.minimum(i, n_main - 1), 0))
    tail_tile = lambda w: pl.BlockSpec((tm, w), lambda i: (jnp.maximum(i - n_main, 0), 0))
    staged = _StagedWeights([wg, wu, wd] + ([mix[2]] if mix is not None else [])
                            + ([(tail[2], (D_DIFF, 2 * D_DIFF))] if tail is not None else []))
    in_specs = [tile(D_MODEL),
                pl.BlockSpec((1, 1, N_MOD * D_MODEL), lambda i: (mod_index(i), 0, 0)),
                _const_spec((1, D_MODEL))] + staged.in_specs
    args = [x, mod3, g] + staged.weights
    out_specs, out_shape = tile(D_MODEL), jax.ShapeDtypeStruct((t, D_MODEL), F32)
    if tail is not None:
        ctx, tail_g, _ = tail
        in_specs += [tail_tile(D_MODEL), _const_spec((1, D_MODEL))]
        args += [ctx, tail_g]
        out_specs = [out_specs, tail_tile(D_DIFF), tail_tile(D_DIFF)]
        out_shape = [out_shape] + [jax.ShapeDtypeStruct((ctx.shape[0], D_DIFF), BF16)] * 2
    if mix is not None:
        att, zf, _ = mix
        by_batch = pl.BlockSpec((tm, D_FOURIER), lambda i: (i % tiles_per_seq, i // tiles_per_seq))
        in_specs += [tile(D_DIFF), by_batch]
        args += [att, zf]
    if final_g is not None:
        in_specs.append(_const_spec((1, D_MODEL)))
        args.append(final_g)
    kern = functools.partial(_ffn_kernel, staged=staged, mod_base=mod_base, has_mix=mix is not None,
                             final_norm=final_g is not None, n_main=n_main if tail is not None else None)
    return pl.pallas_call(
        kern,
        grid=(n_tiles,),
        in_specs=in_specs,
        out_specs=out_specs,
        out_shape=out_shape,
        scratch_shapes=staged.scratch,
        compiler_params=_params(("arbitrary",)),
        name="ffn_mix" if mix is not None else "ffn",
    )(*args)


def _rope(x, cos, sin_lo, sin_hi):
    return (x * cos + pltpu.roll(x, HEAD_W - ROPE_PAIRS, 1) * sin_lo
            + pltpu.roll(x, ROPE_PAIRS, 1) * sin_hi)


def _inproj_kernel(x_ref, mod_ref, g_ref, w_hbm, by_row_ref, by_col_ref, dft_ref,
                   q_ref, k_ref, v_ref, yc_ref, ys_ref, *scratch, staged, tiles_per_seq):
    (w_ref,) = staged.load([w_hbm], scratch)
    h = _rmsnorm(x_ref[...], g_ref[...]) * (1.0 + _mod_row(mod_ref, 4)) + _mod_row(mod_ref, 3)
    p = jnp.dot(h.astype(BF16), w_ref[...], preferred_element_type=F32)
    tm, grid_w = x_ref.shape[0], by_col_ref.shape[1]
    first_row = pl.multiple_of((pl.program_id(0) % tiles_per_seq) * (tm // grid_w), tm // grid_w)
    cos, slo, shi = [(by_row_ref[j, pl.ds(first_row, tm // grid_w), :][:, None, :]
                      + by_col_ref[j][None, :, :]).reshape(tm, HEAD_W) for j in range(3)]
    for hd in range(N_HEADS):
        lo = hd * HEAD_W
        q_ref[:, lo:lo + HEAD_W] = (_rope(p[:, lo:lo + HEAD_W], cos, slo, shi) * Q_SCALE).astype(BF16)
        k_ref[:, lo:lo + HEAD_W] = _rope(p[:, D_DIFF + lo:D_DIFF + lo + HEAD_W], cos, slo, shi).astype(BF16)
    v_ref[...] = p[:, 2 * D_DIFF:3 * D_DIFF].astype(BF16)
    y = jnp.dot(p[:, 3 * D_DIFF:].astype(BF16), dft_ref[...], preferred_element_type=F32)
    yc_ref[...] = y[:, :D_FOURIER].astype(BF16)
    ys_ref[...] = y[:, D_FOURIER:].astype(BF16)


def _inproj(x, mod3, mod_index, g, w_in, rope, chan_dft, *, tm):
    t = x.shape[0]
    by_row, by_col = rope
    tiles_per_seq = by_row.shape[1] * by_col.shape[1] // tm
    tile = lambda w: pl.BlockSpec((tm, w), lambda i: (i, 0))
    by_batch = pl.BlockSpec((tm, D_FOURIER), lambda i: (i % tiles_per_seq, i // tiles_per_seq))
    staged = _StagedWeights([w_in])
    wide = jax.ShapeDtypeStruct((t, D_DIFF), BF16)
    narrow = jax.ShapeDtypeStruct((tiles_per_seq * tm, t // (tiles_per_seq * tm) * D_FOURIER), BF16)
    return pl.pallas_call(
        functools.partial(_inproj_kernel, staged=staged, tiles_per_seq=tiles_per_seq),
        grid=(t // tm,),
        in_specs=[tile(D_MODEL),
                  pl.BlockSpec((1, 1, N_MOD * D_MODEL), lambda i: (mod_index(i), 0, 0)),
                  _const_spec((1, D_MODEL))] + staged.in_specs
                 + [_const_spec(by_row.shape), _const_spec(by_col.shape), _const_spec((D_FOURIER, 2 * D_FOURIER))],
        out_specs=[tile(D_DIFF)] * 3 + [by_batch] * 2,
        out_shape=[wide] * 3 + [narrow] * 2,
        scratch_shapes=staged.scratch,
        compiler_params=_params(("arbitrary",)),
        name="inproj",
    )(x, mod3, g, w_in, by_row, by_col, chan_dft)


def _max_key_norm_sq(kk):
    sq = (kk.astype(F32) ** 2).astype(BF16)
    r = lax.broadcasted_iota(jnp.int32, (HEAD_W, HEAD_W), 0)
    c = lax.broadcasted_iota(jnp.int32, (HEAD_W, HEAD_W), 1)
    sel = jnp.where((c == 0) == (r < HEAD_DIM), 1.0, 0.0) * jnp.where(c < 2, 1.0, 0.0)
    sums = jnp.dot(sq, sel.astype(BF16), preferred_element_type=F32)
    biggest = jnp.max(sums, axis=0, keepdims=True) * (1.0 + 2.0 ** -7)
    return biggest[:, 0:1], biggest[:, 1:2]


def _attn_kernel(lam_ref, g_ref, q_ref, k_ref, kc_ref, v_ref, vc_ref, o_ref, vt_ref, vct_ref, kn_ref, *, tk):
    tq = q_ref.shape[1]

    @pl.when(pl.program_id(2) == 0)
    def _():
        vt_ref[...] = v_ref[0].T
        vct_ref[...] = vc_ref[0].T
        (a1, a2), (b1, b2) = _max_key_norm_sq(k_ref[0]), _max_key_norm_sq(kc_ref[0])
        kn_ref[...] = jnp.sqrt(jnp.concatenate([jnp.broadcast_to(jnp.maximum(a1, b1), (1, tq)),
                                                 jnp.broadcast_to(jnp.maximum(a2, b2), (1, tq))], axis=1))

    lv = lam_ref[...]
    lam = (jnp.exp(jnp.sum(lv[0:1] * lv[1:2], axis=1, keepdims=True))
           - jnp.exp(jnp.sum(lv[2:3] * lv[3:4], axis=1, keepdims=True)) + LAMBDA_INIT)

    qt = q_ref[0].T
    none = jnp.zeros((HEAD_DIM, tq), BF16)
    qcat = jnp.concatenate([jnp.concatenate([qt[:HEAD_DIM], none], axis=0),
                            jnp.concatenate([none, qt[HEAD_DIM:]], axis=0)], axis=1)
    chunks = [(k_ref[0, c * tk:(c + 1) * tk, :], vt_ref[:, c * tk:(c + 1) * tk]) for c in range(k_ref.shape[1] // tk)]
    chunks.append((kc_ref[0], vct_ref[...]))

    def scores(kk):
        return jnp.dot(kk, qcat, preferred_element_type=F32)

    def softmax_sums(shift):
        lsum, acc = 0.0, 0.0
        for kk, vt in chunks:
            e = jnp.exp2(scores(kk) - shift)
            lsum += jnp.sum(e.reshape(-1, F32_SUBLANES, 2 * tq), axis=0)
            acc += jnp.dot(vt, e.astype(BF16), preferred_element_type=F32)
        return jnp.sum(lsum, axis=0, keepdims=True), acc

    def finish(l, acc):
        ot = acc[:, :tq] * (1.0 / l[:, :tq]) - acc[:, tq:] * (lam / l[:, tq:])
        scale = lax.rsqrt(jnp.mean(ot * ot, axis=0, keepdims=True) + RMS_EPS)
        o_ref[0] = (ot * scale * (g_ref[...] * (1.0 - LAMBDA_INIT))).astype(BF16).T

    sq = qt.astype(F32) ** 2
    qn = jnp.sqrt(jnp.concatenate([jnp.sum(sq[:HEAD_DIM], axis=0, keepdims=True),
                                   jnp.sum(sq[HEAD_DIM:], axis=0, keepdims=True)], axis=1))
    l, acc = softmax_sums(qn * kn_ref[...])
    finish(l, acc)

    @pl.when(jnp.min(l) < UNDERFLOW_GUARD)
    def _():
        m = None
        for kk, _ in chunks:
            cm = jnp.max(scores(kk), axis=0, keepdims=True)
            m = cm if m is None else jnp.maximum(m, cm)
        finish(*softmax_sums(m))


def _attention(lamvec, subln_g, q, k, kc, v, vc, *, tq, tk):
    b, n, _ = q.shape
    c = kc.shape[1]
    head_block = lambda rows: pl.BlockSpec((1, rows, HEAD_W), lambda bi, hi, qi: (bi, 0, hi))
    return pl.pallas_call(
        functools.partial(_attn_kernel, tk=tk),
        grid=(b, N_HEADS, n // tq),
        in_specs=[_const_spec(lamvec.shape), _const_spec((HEAD_W, 1)),
                  pl.BlockSpec((1, tq, HEAD_W), lambda bi, hi, qi: (bi, qi, hi)),
                  head_block(n), head_block(c), head_block(n), head_block(c)],
        out_specs=pl.BlockSpec((1, tq, HEAD_W), lambda bi, hi, qi: (bi, qi, hi)),
        out_shape=jax.ShapeDtypeStruct((b, n, D_DIFF), BF16),
        scratch_shapes=[pltpu.VMEM((HEAD_W, n), BF16), pltpu.VMEM((HEAD_W, c), BF16),
                        pltpu.VMEM((1, 2 * tq), F32)],
        compiler_params=_params(("parallel", "parallel", "arbitrary")),
        name="diff_attn",
    )(lamvec, subln_g, q, k, kc, v, vc)


def _dft_kernel(yc_ref, ys_ref, m1_ref, twr_ref, twi_ref, w2_ref, wf_ref, perm_ref, o_ref, br_ref, bi_ref):
    n1, cols = yc_ref.shape[0], yc_ref.shape[2]
    g = m1_ref.shape[0] // (2 * n1)
    per_step = yc_ref.shape[1] // g
    n2 = br_ref.shape[0] // n1
    steps1, steps2 = n2 // g // per_step, n1 // per_step
    i = pl.program_id(0)

    @pl.when(i < steps1)
    def _():
        for s in range(per_step):
            group = i * per_step + s
            mine = slice(s * g, (s + 1) * g)
            u = jnp.concatenate([yc_ref[:, mine, :].reshape(n1 * g, cols),
                                 ys_ref[:, mine, :].reshape(n1 * g, cols)], axis=0)
            a = jnp.dot(m1_ref[...], u, preferred_element_type=F32)
            ar, ai = a[:n1 * g], a[n1 * g:]
            tr, ti = twr_ref[group], twi_ref[group]
            br = (ar * tr - ai * ti).astype(BF16)
            bi = (ar * ti + ai * tr).astype(BF16)
            for c in range(n1):
                dst = pl.ds(pl.multiple_of(c * n2 + group * g, g), g)
                br_ref[dst, :] = br[c * g:(c + 1) * g]
                bi_ref[dst, :] = bi[c * g:(c + 1) * g]

    @pl.when((i >= steps1) & (i < steps1 + steps2))
    def _():
        w = wf_ref.shape[0]
        wf = wf_ref[...].astype(BF16)
        for s in range(per_step):
            src = pl.ds(pl.multiple_of(((i - steps1) * per_step + s) * n2, n2), n2)
            b = jnp.concatenate([br_ref[src, :], bi_ref[src, :]], axis=0)
            x = jnp.dot(w2_ref[...], b, preferred_element_type=F32)
            br_ref[src, :] = jnp.concatenate(
                [jnp.dot(x[:, lo:lo + w].astype(BF16), wf, preferred_element_type=F32).astype(BF16)
                 for lo in range(0, cols, w)], axis=1)

    @pl.when(i >= steps1 + steps2)
    def _():
        run = o_ref.shape[0] // n1
        first = (i - steps1 - steps2) * run
        rows = jnp.concatenate([br_ref[pl.ds(pl.multiple_of(c * n2 + first, run), run), :] for c in range(n1)], axis=0)
        o_ref[...] = jnp.dot(perm_ref[...], rows, preferred_element_type=F32).astype(BF16)


def _position_dft(yc, ys, wf, *, tile_rows):
    n, cols = yc.shape
    n1, g, per_step = DFT_N1, BF16_SUBLANES, DFT_UNITS_PER_STEP
    n2 = n // n1
    steps1, steps2, steps3 = n2 // g // per_step, n1 // per_step, n // tile_rows
    m1, twr, twi, w2 = _position_dft_tables(n, n1, g)
    perm = _row_permutation(n1, tile_rows)
    group = pl.BlockSpec((n1, g * per_step, cols), lambda i: (0, jnp.minimum(i, steps1 - 1), 0))
    return pl.pallas_call(
        _dft_kernel,
        grid=(steps1 + steps2 + steps3,),
        in_specs=[group, group, _const_spec(m1.shape), _const_spec(twr.shape), _const_spec(twi.shape),
                  _const_spec(w2.shape), _const_spec(wf.shape), _const_spec(perm.shape)],
        out_specs=pl.BlockSpec((tile_rows, cols), lambda i: (jnp.maximum(i - steps1 - steps2, 0), 0)),
        out_shape=jax.ShapeDtypeStruct((n, cols), BF16),
        scratch_shapes=[pltpu.VMEM((n, cols), BF16), pltpu.VMEM((n, cols), BF16)],
        compiler_params=_params(("arbitrary",)),
        name="position_dft",
    )(yc.reshape(n1, n2, cols), ys.reshape(n1, n2, cols), m1, twr, twi, w2, wf, perm)


def _rope_tables(n):
    rows = n // GRID_W
    lane = np.arange(HEAD_W)
    sub = lane % HEAD_DIM
    on_row_axis = jnp.asarray(sub // (2 * ROPE_PAIRS) == 0)[None, :]
    second_half = jnp.asarray((sub % (2 * ROPE_PAIRS)) // ROPE_PAIRS == 1)[None, :]
    inv_freq = ROPE_BASE ** (-jnp.asarray(sub % ROPE_PAIRS, dtype=F32) / ROPE_PAIRS)
    kinds = ((jnp.cos, True), (lambda a: -jnp.sin(a), ~second_half), (jnp.sin, second_half))

    def side(size, mine):
        ang = jnp.arange(size, dtype=F32)[:, None] * inv_freq[None, :]
        return jnp.stack([jnp.where(mine & keep, fn(ang), 0.0) for fn, keep in kinds])

    return side(rows, on_row_axis), side(GRID_W, ~on_row_axis)


def _channel_dft_table():
    c = np.arange(FOURIER_GROUP_DIM)
    ang = 2.0 * np.pi * np.outer(c, c) / FOURIER_GROUP_DIM
    eye = np.eye(D_FOURIER // FOURIER_GROUP_DIM)
    scale = FOURIER_GROUP_DIM ** -0.5
    return np.concatenate([np.kron(eye, np.cos(ang)), np.kron(eye, np.sin(ang))], axis=1) * scale


def _position_dft_tables(n, n1, g):
    n2 = n // n1
    w1 = np.exp(-2j * np.pi * np.outer(np.arange(n1), np.arange(n1)) / n1)
    m = np.kron(w1, np.eye(g)) * n ** -0.5
    m1 = np.block([[m.real, m.imag], [m.imag, -m.real]])
    b = (g * np.arange(n2 // g)[:, None, None] + np.arange(g)[None, None, :])
    tw = np.exp(-2j * np.pi * b * np.arange(n1)[None, :, None] / n).reshape(n2 // g, n1 * g, 1)
    ang2 = 2.0 * np.pi * np.outer(np.arange(n2), np.arange(n2)) / n2
    w2 = np.concatenate([np.cos(ang2), np.sin(ang2)], axis=1)
    f32 = lambda t: jnp.asarray(t, dtype=F32)
    return f32(m1).astype(BF16), f32(tw.real), f32(tw.imag), f32(w2).astype(BF16)


def _row_permutation(n1, rows):
    d_per_tile = rows // n1
    p = np.zeros((rows, rows), np.float32)
    c, d = np.meshgrid(np.arange(n1), np.arange(d_per_tile), indexing="ij")
    p[(c + n1 * d).ravel(), (c * d_per_tile + d).ravel()] = 1.0
    return jnp.asarray(p).astype(BF16)


def kernel(x, c, ctx, c_ctx, w_ada, b_ada, norm1_g, ffn1_w_gate, ffn1_w_up, ffn1_w_down, norm_mix_g, w_in,
           lambda_q1, lambda_k1, lambda_q2, lambda_k2, subln_g, w_fourier, w_out, norm2_g, ffn2_w_gate,
           ffn2_w_up, ffn2_w_down, final_norm_g):
    b, n, d = x.shape
    n_ctx = ctx.shape[1]
    assert (d, w_ada.shape[0]) == (D_MODEL, 1) and b + 1 <= MOD_ROWS
    tiles = TILES
    mod_row = lambda rows: (lambda i: jnp.minimum(i // (n // rows), b))
    row = lambda g: g.reshape(1, -1)

    cc = jnp.zeros((MOD_ROWS, d), F32).at[:b].set(c).at[b].set(c_ctx)
    mod3 = _ada(cc, w_ada[0], b_ada).reshape(MOD_ROWS, 1, N_MOD * d)

    x1, kc, vc = _ffn(x.reshape(b * n, d), mod3, mod_row(tiles.ffn_rows), row(norm1_g), ffn1_w_gate[0], ffn1_w_up[0],
                      ffn1_w_down[0], tm=tiles.ffn_rows, mod_base=0,
                      tail=(ctx.reshape(b * n_ctx, d), row(norm_mix_g), w_in[0]))

    chan_dft = jnp.asarray(_channel_dft_table(), dtype=F32).astype(BF16)
    q, k, v, yc, ys = _inproj(x1, mod3, mod_row(tiles.inproj_rows), row(norm_mix_g), w_in[0], _rope_tables(n), chan_dft,
                              tm=tiles.inproj_rows)

    lamvec = jnp.concatenate([lambda_q1, lambda_k1, lambda_q2, lambda_k2], axis=0)
    seq = lambda a, rows: a.reshape(b, rows, a.shape[-1])
    att = _attention(lamvec, subln_g.reshape(HEAD_W, 1), seq(q, n), seq(k, n), seq(kc, n_ctx), seq(v, n), seq(vc, n_ctx),
                     tq=tiles.attn_queries, tk=tiles.attn_keys)

    zf = _position_dft(yc, ys, w_fourier[0], tile_rows=tiles.ffn_rows)

    out = _ffn(x1, mod3, mod_row(tiles.ffn_rows), row(norm2_g), ffn2_w_gate[0], ffn2_w_up[0], ffn2_w_down[0],
               tm=tiles.ffn_rows, mod_base=6, mix=(att.reshape(b * n, D_DIFF), zf, w_out[0]),
               tiles_per_seq=n // tiles.ffn_rows, final_g=row(final_norm_g))
    return out.reshape(b, n, d)
```

```python
import functools
import math
from typing import NamedTuple

import numpy as np
import jax
import jax.numpy as jnp
from jax import lax
from jax.experimental import pallas as pl
from jax.experimental.pallas import tpu as pltpu

D_MODEL = 1024
GRID_W = 64
D_FOURIER = 256
D_DIFF = 768
HEAD_DIM = 64
HEAD_W = 2 * HEAD_DIM
N_HEADS = D_DIFF // HEAD_W
FOURIER_GROUP_DIM = 64
D_IN_PROJ = 3 * D_DIFF + D_FOURIER
D_FF = 2816
N_MOD = 9
ROPE_BASE = 10000.0
ROPE_PAIRS = HEAD_DIM // 4
RMS_EPS = 1e-6
ATTN_SCALE = HEAD_DIM ** -0.5
Q_SCALE = ATTN_SCALE * math.log2(math.e)
LAMBDA_INIT = 0.8 - 0.6 * math.exp(-0.3 * 0)
UNDERFLOW_GUARD = 2.0 ** -80

F32 = jnp.float32
BF16 = jnp.bfloat16

V7X_VMEM_LIMIT_BYTES = 56 * 1024 * 1024
F32_SUBLANES = 8
BF16_SUBLANES = 16
STAGE_CHUNK_BYTES = 3 * 1024 * 1024
MOD_ROWS = F32_SUBLANES
ADA_ROWS_PER_STEP = 128
DFT_N1 = 16
DFT_UNITS_PER_STEP = 2


class _Tiles(NamedTuple):
    ffn_rows: int = 512
    inproj_rows: int = 1024
    attn_queries: int = 1024
    attn_keys: int = 1024


TILES = _Tiles()
FF_CHUNKS = ((0, 1536), (1536, 1280))


def _const_spec(shape):
    return pl.BlockSpec(shape, lambda *_: (0,) * len(shape), pipeline_mode=pl.Buffered(1))


def _params(semantics):
    return pltpu.CompilerParams(dimension_semantics=semantics, vmem_limit_bytes=V7X_VMEM_LIMIT_BYTES)


def _rmsnorm(x, g):
    return x * lax.rsqrt(jnp.mean(x * x, axis=-1, keepdims=True) + RMS_EPS) * g


def _mod_row(mod_ref, k):
    return mod_ref[0, :, k * D_MODEL:(k + 1) * D_MODEL]


def _ada_kernel(cc_ref, w_ref, b_ref, o_ref):
    @pl.when(pl.program_id(0) == 0)
    def _():
        o_ref[...] = jnp.broadcast_to(b_ref[...], o_ref.shape)

    cc = cc_ref[...]
    s = cc * jax.nn.sigmoid(cc)
    o_ref[...] += jnp.dot(s.astype(BF16), w_ref[...].astype(BF16), preferred_element_type=F32)


def _ada(cc, w_ada, b_ada):
    k, n_out = w_ada.shape
    tk = ADA_ROWS_PER_STEP
    return pl.pallas_call(
        _ada_kernel,
        grid=(k // tk,),
        in_specs=[pl.BlockSpec((MOD_ROWS, tk), lambda j: (0, j)),
                  pl.BlockSpec((tk, n_out), lambda j: (j, 0)),
                  _const_spec((1, n_out))],
        out_specs=pl.BlockSpec((MOD_ROWS, n_out), lambda j: (0, 0)),
        out_shape=jax.ShapeDtypeStruct((MOD_ROWS, n_out), F32),
        compiler_params=_params(("arbitrary",)),
        name="ada",
    )(cc, w_ada, b_ada)


def _stage_rows(k, n):
    fits = [r for r in range(BF16_SUBLANES, k + 1, BF16_SUBLANES) if k % r == 0 and r * n * 4 <= STAGE_CHUNK_BYTES]
    return max(fits)


def _stage_weight(src_hbm, first_col, dst_ref, stage_ref, sem_ref):
    rows, width = stage_ref.shape[1], dst_ref.shape[1]
    n_chunks = src_hbm.shape[0] // rows

    def copy(c):
        return pltpu.make_async_copy(src_hbm.at[pl.ds(c * rows, rows), pl.ds(first_col, width)],
                                     stage_ref.at[c % 2, :, pl.ds(0, width)], sem_ref.at[c % 2])

    copy(0).start()
    for c in range(n_chunks):
        if c + 1 < n_chunks:
            copy(c + 1).start()
        copy(c).wait()
        dst_ref[c * rows:(c + 1) * rows, :] = stage_ref[c % 2, :, 0:width].astype(BF16)


class _StagedWeights:
    def __init__(self, weights):
        entries = [w if isinstance(w, tuple) else (w, (0, w.shape[1])) for w in weights]
        self.weights = [w for w, _ in entries]
        self.first_cols = [first for _, (first, _) in entries]
        shapes = [(w.shape[0], width) for w, (_, width) in entries]
        self.in_specs = [pl.BlockSpec(memory_space=pl.ANY)] * len(self.weights)
        stages, self.stage_of = [], [None] * len(shapes)
        for j in sorted(range(len(shapes)), key=lambda j: -shapes[j][1]):
            k, n = shapes[j]
            fits = [s for s, (rows, width) in enumerate(stages)
                    if width >= n and k % rows == 0 and 2 * rows >= _stage_rows(k, n)]
            if not fits:
                stages.append((_stage_rows(k, n), n))
                fits = [len(stages) - 1]
            self.stage_of[j] = fits[0]
        self.scratch = ([pltpu.VMEM(s, BF16) for s in shapes]
                        + [pltpu.VMEM((2,) + s, F32) for s in stages]
                        + [pltpu.SemaphoreType.DMA((2,))])

    def load(self, hbm_refs, scratch_refs):
        n = len(self.weights)
        dst, stages, sem = scratch_refs[:n], scratch_refs[n:-1], scratch_refs[-1]

        @pl.when(pl.program_id(0) == 0)
        def _():
            for src, first, d, s in zip(hbm_refs, self.first_cols, dst, self.stage_of):
                _stage_weight(src, first, d, stages[s], sem)

        return dst


def _ffn_kernel(*refs, staged, mod_base, has_mix, final_norm, n_main):
    n_w = len(staged.weights)
    n_scratch = len(staged.scratch)
    x_ref, mod_ref, g_ref = refs[:3]
    w_hbm = refs[3:3 + n_w]
    rest = list(refs[3 + n_w:len(refs) - n_scratch])
    weights = staged.load(w_hbm, refs[len(refs) - n_scratch:])
    wg_ref, wu_ref, wd_ref = weights[:3]
    has_tail = n_main is not None
    if has_tail:
        tail_ref, tail_g_ref = rest[:2]
        kc_ref, vc_ref = rest[-2:]
        rest = rest[2:-2]
        wkv_ref = weights[-1]
        in_tail = pl.program_id(0) >= n_main
    if has_mix:
        att_ref, zf_ref = rest[:2]
        rest = rest[2:]
        wo_ref = weights[3]
    if final_norm:
        gf_ref = rest[0]
        rest = rest[1:]
    (o_ref,) = rest

    x = x_ref[...]
    if has_tail:
        x = jnp.where(in_tail, tail_ref[...], x)
    if has_mix:
        mix = jnp.dot(att_ref[...], wo_ref[:D_DIFF, :], preferred_element_type=F32)
        mix += jnp.dot(zf_ref[...], wo_ref[D_DIFF:, :], preferred_element_type=F32)
        x = x + _mod_row(mod_ref, 5) * mix
    h = _rmsnorm(x, g_ref[...]) * (1.0 + _mod_row(mod_ref, mod_base + 1)) + _mod_row(mod_ref, mod_base)
    h = h.astype(BF16)
    acc = None
    for start, width in FF_CHUNKS:
        gate = jnp.dot(h, wg_ref[:, start:start + width], preferred_element_type=F32)
        up = jnp.dot(h, wu_ref[:, start:start + width], preferred_element_type=F32)
        a = (gate * jax.nn.sigmoid(gate) * up).astype(BF16)
        part = jnp.dot(a, wd_ref[start:start + width, :], preferred_element_type=F32)
        acc = part if acc is None else acc + part
    x = x + (0.5 * _mod_row(mod_ref, mod_base + 2)) * acc
    if final_norm:
        x = _rmsnorm(x, gf_ref[...])
    if has_tail:
        @pl.when(in_tail)
        def _():
            hc = _rmsnorm(x, tail_g_ref[...]) * (1.0 + _mod_row(mod_ref, 4)) + _mod_row(mod_ref, 3)
            kv = jnp.dot(hc.astype(BF16), wkv_ref[...], preferred_element_type=F32)
            kc_ref[...] = kv[:, :D_DIFF].astype(BF16)
            vc_ref[...] = kv[:, D_DIFF:].astype(BF16)

        @pl.when(jnp.logical_not(in_tail))
        def _():
            o_ref[...] = x
    else:
        o_ref[...] = x


def _ffn(x, mod3, mod_index, g, wg, wu, wd, *, tm, mod_base, tail=None, mix=None, tiles_per_seq=None, final_g=None):
    t = x.shape[0]
    n_main = t // tm
    n_tiles = n_main + (tail[0].shape[0] // tm if tail is not None else 0)
    tile = lambda w: pl.BlockSpec((tm, w), lambda i: (jnp.minimum(i, n_main - 1), 0))
    tail_tile = lambda w: pl.BlockSpec((tm, w), lambda i: (jnp.maximum(i - n_main, 0), 0))
    staged = _StagedWeights([wg, wu, wd] + ([mix[2]] if mix is not None else [])
                            + ([(tail[2], (D_DIFF, 2 * D_DIFF))] if tail is not None else []))
    in_specs = [tile(D_MODEL),
                pl.BlockSpec((1, 1, N_MOD * D_MODEL), lambda i: (mod_index(i), 0, 0)),
                _const_spec((1, D_MODEL))] + staged.in_specs
    args = [x, mod3, g] + staged.weights
    out_specs, out_shape = tile(D_MODEL), jax.ShapeDtypeStruct((t, D_MODEL), F32)
    if tail is not None:
        ctx, tail_g, _ = tail
        in_specs += [tail_tile(D_MODEL), _const_spec((1, D_MODEL))]
        args += [ctx, tail_g]
        out_specs = [out_specs, tail_tile(D_DIFF), tail_tile(D_DIFF)]
        out_shape = [out_shape] + [jax.ShapeDtypeStruct((ctx.shape[0], D_DIFF), BF16)] * 2
    if mix is not None:
        att, zf, _ = mix
        by_batch = pl.BlockSpec((tm, D_FOURIER), lambda i: (i % tiles_per_seq, i // tiles_per_seq))
        in_specs += [tile(D_DIFF), by_batch]
        args += [att, zf]
    if final_g is not None:
        in_specs.append(_const_spec((1, D_MODEL)))
        args.append(final_g)
    kern = functools.partial(_ffn_kernel, staged=staged, mod_base=mod_base, has_mix=mix is not None,
                             final_norm=final_g is not None, n_main=n_main if tail is not None else None)
    return pl.pallas_call(
        kern,
        grid=(n_tiles,),
        in_specs=in_specs,
        out_specs=out_specs,
        out_shape=out_shape,
        scratch_shapes=staged.scratch,
        compiler_params=_params(("arbitrary",)),
        name="ffn_mix" if mix is not None else "ffn",
    )(*args)


def _rope(x, cos, sin_lo, sin_hi):
    return (x * cos + pltpu.roll(x, HEAD_W - ROPE_PAIRS, 1) * sin_lo
            + pltpu.roll(x, ROPE_PAIRS, 1) * sin_hi)


def _inproj_kernel(x_ref, mod_ref, g_ref, w_hbm, by_row_ref, by_col_ref, dft_ref,
                   q_ref, k_ref, v_ref, yc_ref, ys_ref, *scratch, staged, tiles_per_seq):
    (w_ref,) = staged.load([w_hbm], scratch)
    h = _rmsnorm(x_ref[...], g_ref[...]) * (1.0 + _mod_row(mod_ref, 4)) + _mod_row(mod_ref, 3)
    p = jnp.dot(h.astype(BF16), w_ref[...], preferred_element_type=F32)
    tm, grid_w = x_ref.shape[0], by_col_ref.shape[1]
    first_row = pl.multiple_of((pl.program_id(0) % tiles_per_seq) * (tm // grid_w), tm // grid_w)
    cos, slo, shi = [(by_row_ref[j, pl.ds(first_row, tm // grid_w), :][:, None, :]
                      + by_col_ref[j][None, :, :]).reshape(tm, HEAD_W) for j in range(3)]
    for hd in range(N_HEADS):
        lo = hd * HEAD_W
        q_ref[:, lo:lo + HEAD_W] = (_rope(p[:, lo:lo + HEAD_W], cos, slo, shi) * Q_SCALE).astype(BF16)
        k_ref[:, lo:lo + HEAD_W] = _rope(p[:, D_DIFF + lo:D_DIFF + lo + HEAD_W], cos, slo, shi).astype(BF16)
    v_ref[...] = p[:, 2 * D_DIFF:3 * D_DIFF].astype(BF16)
    y = jnp.dot(p[:, 3 * D_DIFF:].astype(BF16), dft_ref[...], preferred_element_type=F32)
    yc_ref[...] = y[:, :D_FOURIER].astype(BF16)
    ys_ref[...] = y[:, D_FOURIER:].astype(BF16)


def _inproj(x, mod3, mod_index, g, w_in, rope, chan_dft, *, tm):
    t = x.shape[0]
    by_row, by_col = rope
    tiles_per_seq = by_row.shape[1] * by_col.shape[1] // tm
    tile = lambda w: pl.BlockSpec((tm, w), lambda i: (i, 0))
    by_batch = pl.BlockSpec((tm, D_FOURIER), lambda i: (i % tiles_per_seq, i // tiles_per_seq))
    staged = _StagedWeights([w_in])
    wide = jax.ShapeDtypeStruct((t, D_DIFF), BF16)
    narrow = jax.ShapeDtypeStruct((tiles_per_seq * tm, t // (tiles_per_seq * tm) * D_FOURIER), BF16)
    return pl.pallas_call(
        functools.partial(_inproj_kernel, staged=staged, tiles_per_seq=tiles_per_seq),
        grid=(t // tm,),
        in_specs=[tile(D_MODEL),
                  pl.BlockSpec((1, 1, N_MOD * D_MODEL), lambda i: (mod_index(i), 0, 0)),
                  _const_spec((1, D_MODEL))] + staged.in_specs
                 + [_const_spec(by_row.shape), _const_spec(by_col.shape), _const_spec((D_FOURIER, 2 * D_FOURIER))],
        out_specs=[tile(D_DIFF)] * 3 + [by_batch] * 2,
        out_shape=[wide] * 3 + [narrow] * 2,
        scratch_shapes=staged.scratch,
        compiler_params=_params(("arbitrary",)),
        name="inproj",
    )(x, mod3, g, w_in, by_row, by_col, chan_dft)


def _max_key_norm_sq(kk):
    sq = (kk.astype(F32) ** 2).astype(BF16)
    r = lax.broadcasted_iota(jnp.int32, (HEAD_W, HEAD_W), 0)
    c = lax.broadcasted_iota(jnp.int32, (HEAD_W, HEAD_W), 1)
    sel = jnp.where((c == 0) == (r < HEAD_DIM), 1.0, 0.0) * jnp.where(c < 2, 1.0, 0.0)
    sums = jnp.dot(sq, sel.astype(BF16), preferred_element_type=F32)
    biggest = jnp.max(sums, axis=0, keepdims=True) * (1.0 + 2.0 ** -7)
    return biggest[:, 0:1], biggest[:, 1:2]


def _attn_kernel(lam_ref, g_ref, q_ref, k_ref, kc_ref, v_ref, vc_ref, o_ref, *rest, tk, exact_shift):
    lmin_ref = None if exact_shift else rest[0]
    vt_ref, vct_ref, kn_ref = rest[-3:]
    tq = q_ref.shape[1]

    @pl.when(pl.program_id(2) == 0)
    def _():
        vt_ref[...] = v_ref[0].T
        vct_ref[...] = vc_ref[0].T
        (a1, a2), (b1, b2) = _max_key_norm_sq(k_ref[0]), _max_key_norm_sq(kc_ref[0])
        kn_ref[...] = jnp.sqrt(jnp.concatenate([jnp.broadcast_to(jnp.maximum(a1, b1), (1, tq)),
                                                 jnp.broadcast_to(jnp.maximum(a2, b2), (1, tq))], axis=1))

    lv = lam_ref[...]
    lam = (jnp.exp(jnp.sum(lv[0:1] * lv[1:2], axis=1, keepdims=True))
           - jnp.exp(jnp.sum(lv[2:3] * lv[3:4], axis=1, keepdims=True)) + LAMBDA_INIT)

    qt = q_ref[0].T
    none = jnp.zeros((HEAD_DIM, tq), BF16)
    qcat = jnp.concatenate([jnp.concatenate([qt[:HEAD_DIM], none], axis=0),
                            jnp.concatenate([none, qt[HEAD_DIM:]], axis=0)], axis=1)
    chunks = [(k_ref[0, c * tk:(c + 1) * tk, :], vt_ref[:, c * tk:(c + 1) * tk]) for c in range(k_ref.shape[1] // tk)]
    chunks.append((kc_ref[0], vct_ref[...]))

    def scores(kk):
        return jnp.dot(kk, qcat, preferred_element_type=F32)

    def softmax_sums(shift):
        lsum, acc = 0.0, 0.0
        for kk, vt in chunks:
            e = jnp.exp2(scores(kk) - shift)
            lsum += jnp.sum(e.reshape(-1, F32_SUBLANES, 2 * tq), axis=0)
            acc += jnp.dot(vt, e.astype(BF16), preferred_element_type=F32)
        return jnp.sum(lsum, axis=0, keepdims=True), acc

    def finish(l, acc):
        ot = acc[:, :tq] * (1.0 / l[:, :tq]) - acc[:, tq:] * (lam / l[:, tq:])
        scale = lax.rsqrt(jnp.mean(ot * ot, axis=0, keepdims=True) + RMS_EPS)
        o_ref[0] = (ot * scale * (g_ref[...] * (1.0 - LAMBDA_INIT))).astype(BF16).T

    if exact_shift:
        m = None
        for kk, _ in chunks:
            cm = jnp.max(scores(kk), axis=0, keepdims=True)
            m = cm if m is None else jnp.maximum(m, cm)
        finish(*softmax_sums(m))
    else:
        sq = qt.astype(F32) ** 2
        qn = jnp.sqrt(jnp.concatenate([jnp.sum(sq[:HEAD_DIM], axis=0, keepdims=True),
                                       jnp.sum(sq[HEAD_DIM:], axis=0, keepdims=True)], axis=1))
        l, acc = softmax_sums(qn * kn_ref[...])
        finish(l, acc)
        lmin_ref[...] = jnp.broadcast_to(jnp.min(l, axis=1, keepdims=True), lmin_ref.shape)


def _attention(lamvec, subln_g, q, k, kc, v, vc, *, tq, tk, exact_shift):
    b, n, _ = q.shape
    c = kc.shape[1]
    blocks = n // tq
    head_block = lambda rows: pl.BlockSpec((1, rows, HEAD_W), lambda bi, hi, qi: (bi, 0, hi))
    query_block = pl.BlockSpec((1, tq, HEAD_W), lambda bi, hi, qi: (bi, qi, hi))
    out_specs, out_shape = query_block, jax.ShapeDtypeStruct((b, n, D_DIFF), BF16)
    if not exact_shift:
        out_specs = [out_specs, pl.BlockSpec((F32_SUBLANES, HEAD_W),
                                             lambda bi, hi, qi: ((bi * N_HEADS + hi) * blocks + qi, 0))]
        out_shape = [out_shape, jax.ShapeDtypeStruct((b * N_HEADS * blocks * F32_SUBLANES, HEAD_W), F32)]
    return pl.pallas_call(
        functools.partial(_attn_kernel, tk=tk, exact_shift=exact_shift),
        grid=(b, N_HEADS, blocks),
        in_specs=[_const_spec(lamvec.shape), _const_spec((HEAD_W, 1)), query_block,
                  head_block(n), head_block(c), head_block(n), head_block(c)],
        out_specs=out_specs,
        out_shape=out_shape,
        scratch_shapes=[pltpu.VMEM((HEAD_W, n), BF16), pltpu.VMEM((HEAD_W, c), BF16),
                        pltpu.VMEM((1, 2 * tq), F32)],
        compiler_params=_params(("parallel", "parallel", "arbitrary")),
        name="diff_attn_exact" if exact_shift else "diff_attn",
    )(lamvec, subln_g, q, k, kc, v, vc)


def _dft_kernel(yc_ref, ys_ref, m1_ref, twr_ref, twi_ref, w2_ref, wf_ref, perm_ref, o_ref, br_ref, bi_ref):
    n1, cols = yc_ref.shape[0], yc_ref.shape[2]
    g = m1_ref.shape[0] // (2 * n1)
    per_step = yc_ref.shape[1] // g
    n2 = br_ref.shape[0] // n1
    steps1, steps2 = n2 // g // per_step, n1 // per_step
    i = pl.program_id(0)

    @pl.when(i < steps1)
    def _():
        for s in range(per_step):
            group = i * per_step + s
            mine = slice(s * g, (s + 1) * g)
            u = jnp.concatenate([yc_ref[:, mine, :].reshape(n1 * g, cols),
                                 ys_ref[:, mine, :].reshape(n1 * g, cols)], axis=0)
            a = jnp.dot(m1_ref[...], u, preferred_element_type=F32)
            ar, ai = a[:n1 * g], a[n1 * g:]
            tr, ti = twr_ref[group], twi_ref[group]
            br = (ar * tr - ai * ti).astype(BF16)
            bi = (ar * ti + ai * tr).astype(BF16)
            for c in range(n1):
                dst = pl.ds(pl.multiple_of(c * n2 + group * g, g), g)
                br_ref[dst, :] = br[c * g:(c + 1) * g]
                bi_ref[dst, :] = bi[c * g:(c + 1) * g]

    @pl.when((i >= steps1) & (i < steps1 + steps2))
    def _():
        w = wf_ref.shape[0]
        wf = wf_ref[...].astype(BF16)
        for s in range(per_step):
            src = pl.ds(pl.multiple_of(((i - steps1) * per_step + s) * n2, n2), n2)
            b = jnp.concatenate([br_ref[src, :], bi_ref[src, :]], axis=0)
            x = jnp.dot(w2_ref[...], b, preferred_element_type=F32)
            br_ref[src, :] = jnp.concatenate(
                [jnp.dot(x[:, lo:lo + w].astype(BF16), wf, preferred_element_type=F32).astype(BF16)
                 for lo in range(0, cols, w)], axis=1)

    @pl.when(i >= steps1 + steps2)
    def _():
        run = o_ref.shape[0] // n1
        first = (i - steps1 - steps2) * run
        rows = jnp.concatenate([br_ref[pl.ds(pl.multiple_of(c * n2 + first, run), run), :] for c in range(n1)], axis=0)
        o_ref[...] = jnp.dot(perm_ref[...], rows, preferred_element_type=F32).astype(BF16)


def _position_dft(yc, ys, wf, *, tile_rows):
    n, cols = yc.shape
    n1, g, per_step = DFT_N1, BF16_SUBLANES, DFT_UNITS_PER_STEP
    n2 = n // n1
    steps1, steps2, steps3 = n2 // g // per_step, n1 // per_step, n // tile_rows
    m1, twr, twi, w2 = _position_dft_tables(n, n1, g)
    perm = _row_permutation(n1, tile_rows)
    group = pl.BlockSpec((n1, g * per_step, cols), lambda i: (0, jnp.minimum(i, steps1 - 1), 0))
    return pl.pallas_call(
        _dft_kernel,
        grid=(steps1 + steps2 + steps3,),
        in_specs=[group, group, _const_spec(m1.shape), _const_spec(twr.shape), _const_spec(twi.shape),
                  _const_spec(w2.shape), _const_spec(wf.shape), _const_spec(perm.shape)],
        out_specs=pl.BlockSpec((tile_rows, cols), lambda i: (jnp.maximum(i - steps1 - steps2, 0), 0)),
        out_shape=jax.ShapeDtypeStruct((n, cols), BF16),
        scratch_shapes=[pltpu.VMEM((n, cols), BF16), pltpu.VMEM((n, cols), BF16)],
        compiler_params=_params(("arbitrary",)),
        name="position_dft",
    )(yc.reshape(n1, n2, cols), ys.reshape(n1, n2, cols), m1, twr, twi, w2, wf, perm)


def _rope_tables(n):
    rows = n // GRID_W
    lane = np.arange(HEAD_W)
    sub = lane % HEAD_DIM
    on_row_axis = jnp.asarray(sub // (2 * ROPE_PAIRS) == 0)[None, :]
    second_half = jnp.asarray((sub % (2 * ROPE_PAIRS)) // ROPE_PAIRS == 1)[None, :]
    inv_freq = ROPE_BASE ** (-jnp.asarray(sub % ROPE_PAIRS, dtype=F32) / ROPE_PAIRS)
    kinds = ((jnp.cos, True), (lambda a: -jnp.sin(a), ~second_half), (jnp.sin, second_half))

    def side(size, mine):
        ang = jnp.arange(size, dtype=F32)[:, None] * inv_freq[None, :]
        return jnp.stack([jnp.where(mine & keep, fn(ang), 0.0) for fn, keep in kinds])

    return side(rows, on_row_axis), side(GRID_W, ~on_row_axis)


def _channel_dft_table():
    c = np.arange(FOURIER_GROUP_DIM)
    ang = 2.0 * np.pi * np.outer(c, c) / FOURIER_GROUP_DIM
    eye = np.eye(D_FOURIER // FOURIER_GROUP_DIM)
    scale = FOURIER_GROUP_DIM ** -0.5
    return np.concatenate([np.kron(eye, np.cos(ang)), np.kron(eye, np.sin(ang))], axis=1) * scale


def _position_dft_tables(n, n1, g):
    n2 = n // n1
    w1 = np.exp(-2j * np.pi * np.outer(np.arange(n1), np.arange(n1)) / n1)
    m = np.kron(w1, np.eye(g)) * n ** -0.5
    m1 = np.block([[m.real, m.imag], [m.imag, -m.real]])
    b = (g * np.arange(n2 // g)[:, None, None] + np.arange(g)[None, None, :])
    tw = np.exp(-2j * np.pi * b * np.arange(n1)[None, :, None] / n).reshape(n2 // g, n1 * g, 1)
    ang2 = 2.0 * np.pi * np.outer(np.arange(n2), np.arange(n2)) / n2
    w2 = np.concatenate([np.cos(ang2), np.sin(ang2)], axis=1)
    f32 = lambda t: jnp.asarray(t, dtype=F32)
    return f32(m1).astype(BF16), f32(tw.real), f32(tw.imag), f32(w2).astype(BF16)


def _row_permutation(n1, rows):
    d_per_tile = rows // n1
    p = np.zeros((rows, rows), np.float32)
    c, d = np.meshgrid(np.arange(n1), np.arange(d_per_tile), indexing="ij")
    p[(c + n1 * d).ravel(), (c * d_per_tile + d).ravel()] = 1.0
    return jnp.asarray(p).astype(BF16)


def kernel(x, c, ctx, c_ctx, w_ada, b_ada, norm1_g, ffn1_w_gate, ffn1_w_up, ffn1_w_down, norm_mix_g, w_in,
           lambda_q1, lambda_k1, lambda_q2, lambda_k2, subln_g, w_fourier, w_out, norm2_g, ffn2_w_gate,
           ffn2_w_up, ffn2_w_down, final_norm_g):
    b, n, d = x.shape
    n_ctx = ctx.shape[1]
    assert (d, w_ada.shape[0]) == (D_MODEL, 1) and b + 1 <= MOD_ROWS
    tiles = TILES
    mod_row = lambda rows: (lambda i: jnp.minimum(i // (n // rows), b))
    row = lambda g: g.reshape(1, -1)

    cc = jnp.zeros((MOD_ROWS, d), F32).at[:b].set(c).at[b].set(c_ctx)
    mod3 = _ada(cc, w_ada[0], b_ada).reshape(MOD_ROWS, 1, N_MOD * d)

    x1, kc, vc = _ffn(x.reshape(b * n, d), mod3, mod_row(tiles.ffn_rows), row(norm1_g), ffn1_w_gate[0], ffn1_w_up[0],
                      ffn1_w_down[0], tm=tiles.ffn_rows, mod_base=0,
                      tail=(ctx.reshape(b * n_ctx, d), row(norm_mix_g), w_in[0]))

    chan_dft = jnp.asarray(_channel_dft_table(), dtype=F32).astype(BF16)
    q, k, v, yc, ys = _inproj(x1, mod3, mod_row(tiles.inproj_rows), row(norm_mix_g), w_in[0], _rope_tables(n), chan_dft,
                              tm=tiles.inproj_rows)

    lamvec = jnp.concatenate([lambda_q1, lambda_k1, lambda_q2, lambda_k2], axis=0)
    seq = lambda a, rows: a.reshape(b, rows, a.shape[-1])
    attend = functools.partial(_attention, lamvec, subln_g.reshape(HEAD_W, 1), seq(q, n), seq(k, n), seq(kc, n_ctx),
                               seq(v, n), seq(vc, n_ctx), tq=tiles.attn_queries, tk=tiles.attn_keys)
    att, smallest_sum = attend(exact_shift=False)
    att = lax.cond(jnp.min(smallest_sum) < UNDERFLOW_GUARD, lambda: attend(exact_shift=True), lambda: att)

    zf = _position_dft(yc, ys, w_fourier[0], tile_rows=tiles.ffn_rows)

    out = _ffn(x1, mod3, mod_row(tiles.ffn_rows), row(norm2_g), ffn2_w_gate[0], ffn2_w_up[0], ffn2_w_down[0],
               tm=tiles.ffn_rows, mod_base=6, mix=(att.reshape(b * n, D_DIFF), zf, w_out[0]),
               tiles_per_seq=n // tiles.ffn_rows, final_g=row(final_norm_g))
    return out.reshape(b, n, d)
```

```python
import functools
import math
from typing import NamedTuple

import numpy as np
import jax
import jax.numpy as jnp
from jax import lax
from jax.experimental import pallas as pl
from jax.experimental.pallas import tpu as pltpu

D_MODEL = 1024
GRID_W = 64
D_FOURIER = 256
D_DIFF = 768
HEAD_DIM = 64
HEAD_W = 2 * HEAD_DIM
N_HEADS = D_DIFF // HEAD_W
FOURIER_GROUP_DIM = 64
D_IN_PROJ = 3 * D_DIFF + D_FOURIER
D_FF = 2816
N_MOD = 9
ROPE_BASE = 10000.0
ROPE_PAIRS = HEAD_DIM // 4
RMS_EPS = 1e-6
ATTN_SCALE = HEAD_DIM ** -0.5
Q_SCALE = ATTN_SCALE * math.log2(math.e)
LAMBDA_INIT = 0.8 - 0.6 * math.exp(-0.3 * 0)
UNDERFLOW_GUARD = 2.0 ** -80

F32 = jnp.float32
BF16 = jnp.bfloat16

V7X_VMEM_LIMIT_BYTES = 56 * 1024 * 1024
F32_SUBLANES = 8
BF16_SUBLANES = 16
STAGE_CHUNK_BYTES = 3 * 1024 * 1024
MOD_ROWS = F32_SUBLANES
ADA_ROWS_PER_STEP = 128
DFT_N1 = 16
DFT_UNITS_PER_STEP = 2


class _Tiles(NamedTuple):
    ffn_rows: int = 512
    inproj_rows: int = 1024
    attn_queries: int = 1024
    attn_keys: int = 1024


TILES = _Tiles()
FF_CHUNKS = ((0, 1536), (1536, 1280))


def _const_spec(shape):
    return pl.BlockSpec(shape, lambda *_: (0,) * len(shape), pipeline_mode=pl.Buffered(1))


def _params(semantics):
    return pltpu.CompilerParams(dimension_semantics=semantics, vmem_limit_bytes=V7X_VMEM_LIMIT_BYTES)


def _rmsnorm(x, g):
    return x * lax.rsqrt(jnp.mean(x * x, axis=-1, keepdims=True) + RMS_EPS) * g


def _mod_row(mod_ref, k):
    return mod_ref[0, :, k * D_MODEL:(k + 1) * D_MODEL]


def _ada_kernel(cc_ref, w_ref, b_ref, o_ref):
    @pl.when(pl.program_id(0) == 0)
    def _():
        o_ref[...] = jnp.broadcast_to(b_ref[...], o_ref.shape)

    cc = cc_ref[...]
    s = cc * jax.nn.sigmoid(cc)
    o_ref[...] += jnp.dot(s.astype(BF16), w_ref[...].astype(BF16), preferred_element_type=F32)


def _ada(cc, w_ada, b_ada):
    k, n_out = w_ada.shape
    tk = ADA_ROWS_PER_STEP
    return pl.pallas_call(
        _ada_kernel,
        grid=(k // tk,),
        in_specs=[pl.BlockSpec((MOD_ROWS, tk), lambda j: (0, j)),
                  pl.BlockSpec((tk, n_out), lambda j: (j, 0)),
                  _const_spec((1, n_out))],
        out_specs=pl.BlockSpec((MOD_ROWS, n_out), lambda j: (0, 0)),
        out_shape=jax.ShapeDtypeStruct((MOD_ROWS, n_out), F32),
        compiler_params=_params(("arbitrary",)),
        name="ada",
    )(cc, w_ada, b_ada)


def _stage_rows(k, n):
    fits = [r for r in range(BF16_SUBLANES, k + 1, BF16_SUBLANES) if k % r == 0 and r * n * 4 <= STAGE_CHUNK_BYTES]
    return max(fits)


def _stage_weight(src_hbm, first_col, dst_ref, stage_ref, sem_ref):
    rows, width = stage_ref.shape[1], dst_ref.shape[1]
    n_chunks = src_hbm.shape[0] // rows

    def copy(c):
        return pltpu.make_async_copy(src_hbm.at[pl.ds(c * rows, rows), pl.ds(first_col, width)],
                                     stage_ref.at[c % 2, :, pl.ds(0, width)], sem_ref.at[c % 2])

    copy(0).start()
    for c in range(n_chunks):
        if c + 1 < n_chunks:
            copy(c + 1).start()
        copy(c).wait()
        dst_ref[c * rows:(c + 1) * rows, :] = stage_ref[c % 2, :, 0:width].astype(BF16)


class _StagedWeights:
    def __init__(self, weights):
        entries = [w if isinstance(w, tuple) else (w, (0, w.shape[1])) for w in weights]
        self.weights = [w for w, _ in entries]
        self.first_cols = [first for _, (first, _) in entries]
        shapes = [(w.shape[0], width) for w, (_, width) in entries]
        self.in_specs = [pl.BlockSpec(memory_space=pl.ANY)] * len(self.weights)
        stages, self.stage_of = [], [None] * len(shapes)
        for j in sorted(range(len(shapes)), key=lambda j: -shapes[j][1]):
            k, n = shapes[j]
            fits = [s for s, (rows, width) in enumerate(stages)
                    if width >= n and k % rows == 0 and 2 * rows >= _stage_rows(k, n)]
            if not fits:
                stages.append((_stage_rows(k, n), n))
                fits = [len(stages) - 1]
            self.stage_of[j] = fits[0]
        self.scratch = ([pltpu.VMEM(s, BF16) for s in shapes]
                        + [pltpu.VMEM((2,) + s, F32) for s in stages]
                        + [pltpu.SemaphoreType.DMA((2,))])

    def load(self, hbm_refs, scratch_refs):
        n = len(self.weights)
        dst, stages, sem = scratch_refs[:n], scratch_refs[n:-1], scratch_refs[-1]

        @pl.when(pl.program_id(0) == 0)
        def _():
            for src, first, d, s in zip(hbm_refs, self.first_cols, dst, self.stage_of):
                _stage_weight(src, first, d, stages[s], sem)

        return dst


def _ffn_kernel(*refs, staged, mod_base, has_mix, final_norm, n_main):
    n_w = len(staged.weights)
    n_scratch = len(staged.scratch)
    x_ref, mod_ref, g_ref = refs[:3]
    w_hbm = refs[3:3 + n_w]
    rest = list(refs[3 + n_w:len(refs) - n_scratch])
    weights = staged.load(w_hbm, refs[len(refs) - n_scratch:])
    wg_ref, wu_ref, wd_ref = weights[:3]
    has_tail = n_main is not None
    if has_tail:
        tail_ref, tail_g_ref = rest[:2]
        kc_ref, vc_ref = rest[-2:]
        rest = rest[2:-2]
        wkv_ref = weights[-1]
        in_tail = pl.program_id(0) >= n_main
    if has_mix:
        att_ref, zf_ref = rest[:2]
        rest = rest[2:]
        wo_ref = weights[3]
    if final_norm:
        gf_ref = rest[0]
        rest = rest[1:]
    (o_ref,) = rest

    x = x_ref[...]
    if has_tail:
        x = jnp.where(in_tail, tail_ref[...], x)
    if has_mix:
        mix = jnp.dot(att_ref[...], wo_ref[:D_DIFF, :], preferred_element_type=F32)
        mix += jnp.dot(zf_ref[...], wo_ref[D_DIFF:, :], preferred_element_type=F32)
        x = x + _mod_row(mod_ref, 5) * mix
    h = _rmsnorm(x, g_ref[...]) * (1.0 + _mod_row(mod_ref, mod_base + 1)) + _mod_row(mod_ref, mod_base)
    h = h.astype(BF16)
    acc = None
    for start, width in FF_CHUNKS:
        gate = jnp.dot(h, wg_ref[:, start:start + width], preferred_element_type=F32)
        up = jnp.dot(h, wu_ref[:, start:start + width], preferred_element_type=F32)
        a = (gate * jax.nn.sigmoid(gate) * up).astype(BF16)
        part = jnp.dot(a, wd_ref[start:start + width, :], preferred_element_type=F32)
        acc = part if acc is None else acc + part
    x = x + (0.5 * _mod_row(mod_ref, mod_base + 2)) * acc
    if final_norm:
        x = _rmsnorm(x, gf_ref[...])
    if has_tail:
        @pl.when(in_tail)
        def _():
            hc = _rmsnorm(x, tail_g_ref[...]) * (1.0 + _mod_row(mod_ref, 4)) + _mod_row(mod_ref, 3)
            kv = jnp.dot(hc.astype(BF16), wkv_ref[...], preferred_element_type=F32)
            kc_ref[...] = kv[:, :D_DIFF].astype(BF16)
            vc_ref[...] = kv[:, D_DIFF:].astype(BF16)

        @pl.when(jnp.logical_not(in_tail))
        def _():
            o_ref[...] = x
    else:
        o_ref[...] = x


def _ffn(x, mod3, mod_index, g, wg, wu, wd, *, tm, mod_base, tail=None, mix=None, tiles_per_seq=None, final_g=None):
    t = x.shape[0]
    n_main = t // tm
    n_tiles = n_main + (tail[0].shape[0] // tm if tail is not None else 0)
    tile = lambda w: pl.BlockSpec((tm, w), lambda i: (jnp.minimum(i, n_main - 1), 0))
    tail_tile = lambda w: pl.BlockSpec((tm, w), lambda i: (jnp.maximum(i - n_main, 0), 0))
    staged = _StagedWeights([wg, wu, wd] + ([mix[2]] if mix is not None else [])
                            + ([(tail[2], (D_DIFF, 2 * D_DIFF))] if tail is not None else []))
    in_specs = [tile(D_MODEL),
                pl.BlockSpec((1, 1, N_MOD * D_MODEL), lambda i: (mod_index(i), 0, 0)),
                _const_spec((1, D_MODEL))] + staged.in_specs
    args = [x, mod3, g] + staged.weights
    out_specs, out_shape = tile(D_MODEL), jax.ShapeDtypeStruct((t, D_MODEL), F32)
    if tail is not None:
        ctx, tail_g, _ = tail
        in_specs += [tail_tile(D_MODEL), _const_spec((1, D_MODEL))]
        args += [ctx, tail_g]
        out_specs = [out_specs, tail_tile(D_DIFF), tail_tile(D_DIFF)]
        out_shape = [out_shape] + [jax.ShapeDtypeStruct((ctx.shape[0], D_DIFF), BF16)] * 2
    if mix is not None:
        att, zf, _ = mix
        by_batch = pl.BlockSpec((tm, D_FOURIER), lambda i: (i % tiles_per_seq, i // tiles_per_seq))
        in_specs += [tile(D_DIFF), by_batch]
        args += [att, zf]
    if final_g is not None:
        in_specs.append(_const_spec((1, D_MODEL)))
        args.append(final_g)
    kern = functools.partial(_ffn_kernel, staged=staged, mod_base=mod_base, has_mix=mix is not None,
                             final_norm=final_g is not None, n_main=n_main if tail is not None else None)
    return pl.pallas_call(
        kern,
        grid=(n_tiles,),
        in_specs=in_specs,
        out_specs=out_specs,
        out_shape=out_shape,
        scratch_shapes=staged.scratch,
        compiler_params=_params(("arbitrary",)),
        name="ffn_mix" if mix is not None else "ffn",
    )(*args)


def _rope(x, cos, sin_lo, sin_hi):
    return (x * cos + pltpu.roll(x, HEAD_W - ROPE_PAIRS, 1) * sin_lo
            + pltpu.roll(x, ROPE_PAIRS, 1) * sin_hi)


def _inproj_kernel(x_ref, mod_ref, g_ref, w_hbm, by_row_ref, by_col_ref, dft_ref,
                   q_ref, k_ref, v_ref, yc_ref, ys_ref, *scratch, staged, tiles_per_seq):
    (w_ref,) = staged.load([w_hbm], scratch)
    h = _rmsnorm(x_ref[...], g_ref[...]) * (1.0 + _mod_row(mod_ref, 4)) + _mod_row(mod_ref, 3)
    p = jnp.dot(h.astype(BF16), w_ref[...], preferred_element_type=F32)
    tm, grid_w = x_ref.shape[0], by_col_ref.shape[1]
    first_row = pl.multiple_of((pl.program_id(0) % tiles_per_seq) * (tm // grid_w), tm // grid_w)
    cos, slo, shi = [(by_row_ref[j, pl.ds(first_row, tm // grid_w), :][:, None, :]
                      + by_col_ref[j][None, :, :]).reshape(tm, HEAD_W) for j in range(3)]
    for hd in range(N_HEADS):
        lo = hd * HEAD_W
        q_ref[:, lo:lo + HEAD_W] = (_rope(p[:, lo:lo + HEAD_W], cos, slo, shi) * Q_SCALE).astype(BF16)
        k_ref[:, lo:lo + HEAD_W] = _rope(p[:, D_DIFF + lo:D_DIFF + lo + HEAD_W], cos, slo, shi).astype(BF16)
    v_ref[...] = p[:, 2 * D_DIFF:3 * D_DIFF].astype(BF16)
    y = jnp.dot(p[:, 3 * D_DIFF:].astype(BF16), dft_ref[...], preferred_element_type=F32)
    yc_ref[...] = y[:, :D_FOURIER].astype(BF16)
    ys_ref[...] = y[:, D_FOURIER:].astype(BF16)


def _inproj(x, mod3, mod_index, g, w_in, rope, chan_dft, *, tm):
    t = x.shape[0]
    by_row, by_col = rope
    tiles_per_seq = by_row.shape[1] * by_col.shape[1] // tm
    tile = lambda w: pl.BlockSpec((tm, w), lambda i: (i, 0))
    by_batch = pl.BlockSpec((tm, D_FOURIER), lambda i: (i % tiles_per_seq, i // tiles_per_seq))
    staged = _StagedWeights([w_in])
    wide = jax.ShapeDtypeStruct((t, D_DIFF), BF16)
    narrow = jax.ShapeDtypeStruct((tiles_per_seq * tm, t // (tiles_per_seq * tm) * D_FOURIER), BF16)
    return pl.pallas_call(
        functools.partial(_inproj_kernel, staged=staged, tiles_per_seq=tiles_per_seq),
        grid=(t // tm,),
        in_specs=[tile(D_MODEL),
                  pl.BlockSpec((1, 1, N_MOD * D_MODEL), lambda i: (mod_index(i), 0, 0)),
                  _const_spec((1, D_MODEL))] + staged.in_specs
                 + [_const_spec(by_row.shape), _const_spec(by_col.shape), _const_spec((D_FOURIER, 2 * D_FOURIER))],
        out_specs=[tile(D_DIFF)] * 3 + [by_batch] * 2,
        out_shape=[wide] * 3 + [narrow] * 2,
        scratch_shapes=staged.scratch,
        compiler_params=_params(("arbitrary",)),
        name="inproj",
    )(x, mod3, g, w_in, by_row, by_col, chan_dft)


def _max_key_norm_sq(kk):
    sq = (kk.astype(F32) ** 2).astype(BF16)
    r = lax.broadcasted_iota(jnp.int32, (HEAD_W, HEAD_W), 0)
    c = lax.broadcasted_iota(jnp.int32, (HEAD_W, HEAD_W), 1)
    sel = jnp.where((c == 0) == (r < HEAD_DIM), 1.0, 0.0) * jnp.where(c < 2, 1.0, 0.0)
    sums = jnp.dot(sq, sel.astype(BF16), preferred_element_type=F32)
    biggest = jnp.max(sums, axis=0, keepdims=True) * (1.0 + 2.0 ** -7)
    return biggest[:, 0:1], biggest[:, 1:2]


def _attn_kernel(lam_ref, g_ref, q_ref, k_ref, kc_ref, v_ref, vc_ref, o_ref, *rest, tk, exact_shift):
    lmin_ref = None if exact_shift else rest[0]
    vt_ref, vct_ref, kn_ref = rest[-3:]
    tq = q_ref.shape[1]

    if not exact_shift:
        @pl.when((pl.program_id(0) == 0) & (pl.program_id(1) == 0) & (pl.program_id(2) == 0))
        def _():
            lmin_ref[...] = jnp.full(lmin_ref.shape, jnp.finfo(F32).max, F32)

    @pl.when(pl.program_id(2) == 0)
    def _():
        vt_ref[...] = v_ref[0].T
        vct_ref[...] = vc_ref[0].T
        (a1, a2), (b1, b2) = _max_key_norm_sq(k_ref[0]), _max_key_norm_sq(kc_ref[0])
        kn_ref[...] = jnp.sqrt(jnp.concatenate([jnp.broadcast_to(jnp.maximum(a1, b1), (1, tq)),
                                                 jnp.broadcast_to(jnp.maximum(a2, b2), (1, tq))], axis=1))

    lv = lam_ref[...]
    lam = (jnp.exp(jnp.sum(lv[0:1] * lv[1:2], axis=1, keepdims=True))
           - jnp.exp(jnp.sum(lv[2:3] * lv[3:4], axis=1, keepdims=True)) + LAMBDA_INIT)

    qt = q_ref[0].T
    none = jnp.zeros((HEAD_DIM, tq), BF16)
    qcat = jnp.concatenate([jnp.concatenate([qt[:HEAD_DIM], none], axis=0),
                            jnp.concatenate([none, qt[HEAD_DIM:]], axis=0)], axis=1)
    chunks = [(k_ref[0, c * tk:(c + 1) * tk, :], vt_ref[:, c * tk:(c + 1) * tk]) for c in range(k_ref.shape[1] // tk)]
    chunks.append((kc_ref[0], vct_ref[...]))

    def scores(kk):
        return jnp.dot(kk, qcat, preferred_element_type=F32)

    def softmax_sums(shift):
        lsum, acc = 0.0, 0.0
        for kk, vt in chunks:
            e = jnp.exp2(scores(kk) - shift)
            lsum += jnp.sum(e.reshape(-1, F32_SUBLANES, 2 * tq), axis=0)
            acc += jnp.dot(vt, e.astype(BF16), preferred_element_type=F32)
        return jnp.sum(lsum, axis=0, keepdims=True), acc

    def finish(l, acc):
        ot = acc[:, :tq] * (1.0 / l[:, :tq]) - acc[:, tq:] * (lam / l[:, tq:])
        scale = lax.rsqrt(jnp.mean(ot * ot, axis=0, keepdims=True) + RMS_EPS)
        o_ref[0] = (ot * scale * (g_ref[...] * (1.0 - LAMBDA_INIT))).astype(BF16).T

    if exact_shift:
        m = None
        for kk, _ in chunks:
            cm = jnp.max(scores(kk), axis=0, keepdims=True)
            m = cm if m is None else jnp.maximum(m, cm)
        finish(*softmax_sums(m))
    else:
        sq = qt.astype(F32) ** 2
        qn = jnp.sqrt(jnp.concatenate([jnp.sum(sq[:HEAD_DIM], axis=0, keepdims=True),
                                       jnp.sum(sq[HEAD_DIM:], axis=0, keepdims=True)], axis=1))
        l, acc = softmax_sums(qn * kn_ref[...])
        finish(l, acc)
        lmin_ref[...] = jnp.minimum(lmin_ref[...], l)


def _attention(lamvec, subln_g, q, k, kc, v, vc, *, tq, tk, exact_shift):
    b, n, _ = q.shape
    c = kc.shape[1]
    blocks = n // tq
    head_block = lambda rows: pl.BlockSpec((1, rows, HEAD_W), lambda bi, hi, qi: (bi, 0, hi))
    query_block = pl.BlockSpec((1, tq, HEAD_W), lambda bi, hi, qi: (bi, qi, hi))
    out_specs, out_shape = query_block, jax.ShapeDtypeStruct((b, n, D_DIFF), BF16)
    if not exact_shift:
        out_specs = [out_specs, pl.BlockSpec((F32_SUBLANES, 2 * tq), lambda bi, hi, qi: (0, 0))]
        out_shape = [out_shape, jax.ShapeDtypeStruct((F32_SUBLANES, 2 * tq), F32)]
    return pl.pallas_call(
        functools.partial(_attn_kernel, tk=tk, exact_shift=exact_shift),
        grid=(b, N_HEADS, blocks),
        in_specs=[_const_spec(lamvec.shape), _const_spec((HEAD_W, 1)), query_block,
                  head_block(n), head_block(c), head_block(n), head_block(c)],
        out_specs=out_specs,
        out_shape=out_shape,
        scratch_shapes=[pltpu.VMEM((HEAD_W, n), BF16), pltpu.VMEM((HEAD_W, c), BF16),
                        pltpu.VMEM((1, 2 * tq), F32)],
        compiler_params=_params(("arbitrary", "arbitrary", "arbitrary")),
        name="diff_attn_exact" if exact_shift else "diff_attn",
    )(lamvec, subln_g, q, k, kc, v, vc)


def _dft_kernel(yc_ref, ys_ref, m1_ref, twr_ref, twi_ref, w2_ref, wf_ref, perm_ref, o_ref, br_ref, bi_ref):
    n1, cols = yc_ref.shape[0], yc_ref.shape[2]
    g = m1_ref.shape[0] // (2 * n1)
    per_step = yc_ref.shape[1] // g
    n2 = br_ref.shape[0] // n1
    steps1, steps2 = n2 // g // per_step, n1 // per_step
    i = pl.program_id(0)

    @pl.when(i < steps1)
    def _():
        for s in range(per_step):
            group = i * per_step + s
            mine = slice(s * g, (s + 1) * g)
            u = jnp.concatenate([yc_ref[:, mine, :].reshape(n1 * g, cols),
                                 ys_ref[:, mine, :].reshape(n1 * g, cols)], axis=0)
            a = jnp.dot(m1_ref[...], u, preferred_element_type=F32)
            ar, ai = a[:n1 * g], a[n1 * g:]
            tr, ti = twr_ref[group], twi_ref[group]
            br = (ar * tr - ai * ti).astype(BF16)
            bi = (ar * ti + ai * tr).astype(BF16)
            for c in range(n1):
                dst = pl.ds(pl.multiple_of(c * n2 + group * g, g), g)
                br_ref[dst, :] = br[c * g:(c + 1) * g]
                bi_ref[dst, :] = bi[c * g:(c + 1) * g]

    @pl.when((i >= steps1) & (i < steps1 + steps2))
    def _():
        w = wf_ref.shape[0]
        wf = wf_ref[...].astype(BF16)
        for s in range(per_step):
            src = pl.ds(pl.multiple_of(((i - steps1) * per_step + s) * n2, n2), n2)
            b = jnp.concatenate([br_ref[src, :], bi_ref[src, :]], axis=0)
            x = jnp.dot(w2_ref[...], b, preferred_element_type=F32)
            br_ref[src, :] = jnp.concatenate(
                [jnp.dot(x[:, lo:lo + w].astype(BF16), wf, preferred_element_type=F32).astype(BF16)
                 for lo in range(0, cols, w)], axis=1)

    @pl.when(i >= steps1 + steps2)
    def _():
        run = o_ref.shape[0] // n1
        first = (i - steps1 - steps2) * run
        rows = jnp.concatenate([br_ref[pl.ds(pl.multiple_of(c * n2 + first, run), run), :] for c in range(n1)], axis=0)
        o_ref[...] = jnp.dot(perm_ref[...], rows, preferred_element_type=F32).astype(BF16)


def _position_dft(yc, ys, wf, *, tile_rows):
    n, cols = yc.shape
    n1, g, per_step = DFT_N1, BF16_SUBLANES, DFT_UNITS_PER_STEP
    n2 = n // n1
    steps1, steps2, steps3 = n2 // g // per_step, n1 // per_step, n // tile_rows
    m1, twr, twi, w2 = _position_dft_tables(n, n1, g)
    perm = _row_permutation(n1, tile_rows)
    group = pl.BlockSpec((n1, g * per_step, cols), lambda i: (0, jnp.minimum(i, steps1 - 1), 0))
    return pl.pallas_call(
        _dft_kernel,
        grid=(steps1 + steps2 + steps3,),
        in_specs=[group, group, _const_spec(m1.shape), _const_spec(twr.shape), _const_spec(twi.shape),
                  _const_spec(w2.shape), _const_spec(wf.shape), _const_spec(perm.shape)],
        out_specs=pl.BlockSpec((tile_rows, cols), lambda i: (jnp.maximum(i - steps1 - steps2, 0), 0)),
        out_shape=jax.ShapeDtypeStruct((n, cols), BF16),
        scratch_shapes=[pltpu.VMEM((n, cols), BF16), pltpu.VMEM((n, cols), BF16)],
        compiler_params=_params(("arbitrary",)),
        name="position_dft",
    )(yc.reshape(n1, n2, cols), ys.reshape(n1, n2, cols), m1, twr, twi, w2, wf, perm)


def _rope_tables(n):
    rows = n // GRID_W
    lane = np.arange(HEAD_W)
    sub = lane % HEAD_DIM
    on_row_axis = jnp.asarray(sub // (2 * ROPE_PAIRS) == 0)[None, :]
    second_half = jnp.asarray((sub % (2 * ROPE_PAIRS)) // ROPE_PAIRS == 1)[None, :]
    inv_freq = ROPE_BASE ** (-jnp.asarray(sub % ROPE_PAIRS, dtype=F32) / ROPE_PAIRS)
    kinds = ((jnp.cos, True), (lambda a: -jnp.sin(a), ~second_half), (jnp.sin, second_half))

    def side(size, mine):
        ang = jnp.arange(size, dtype=F32)[:, None] * inv_freq[None, :]
        return jnp.stack([jnp.where(mine & keep, fn(ang), 0.0) for fn, keep in kinds])

    return side(rows, on_row_axis), side(GRID_W, ~on_row_axis)


def _channel_dft_table():
    c = np.arange(FOURIER_GROUP_DIM)
    ang = 2.0 * np.pi * np.outer(c, c) / FOURIER_GROUP_DIM
    eye = np.eye(D_FOURIER // FOURIER_GROUP_DIM)
    scale = FOURIER_GROUP_DIM ** -0.5
    return np.concatenate([np.kron(eye, np.cos(ang)), np.kron(eye, np.sin(ang))], axis=1) * scale


def _position_dft_tables(n, n1, g):
    n2 = n // n1
    w1 = np.exp(-2j * np.pi * np.outer(np.arange(n1), np.arange(n1)) / n1)
    m = np.kron(w1, np.eye(g)) * n ** -0.5
    m1 = np.block([[m.real, m.imag], [m.imag, -m.real]])
    b = (g * np.arange(n2 // g)[:, None, None] + np.arange(g)[None, None, :])
    tw = np.exp(-2j * np.pi * b * np.arange(n1)[None, :, None] / n).reshape(n2 // g, n1 * g, 1)
    ang2 = 2.0 * np.pi * np.outer(np.arange(n2), np.arange(n2)) / n2
    w2 = np.concatenate([np.cos(ang2), np.sin(ang2)], axis=1)
    f32 = lambda t: jnp.asarray(t, dtype=F32)
    return f32(m1).astype(BF16), f32(tw.real), f32(tw.imag), f32(w2).astype(BF16)


def _row_permutation(n1, rows):
    d_per_tile = rows // n1
    p = np.zeros((rows, rows), np.float32)
    c, d = np.meshgrid(np.arange(n1), np.arange(d_per_tile), indexing="ij")
    p[(c + n1 * d).ravel(), (c * d_per_tile + d).ravel()] = 1.0
    return jnp.asarray(p).astype(BF16)


def kernel(x, c, ctx, c_ctx, w_ada, b_ada, norm1_g, ffn1_w_gate, ffn1_w_up, ffn1_w_down, norm_mix_g, w_in,
           lambda_q1, lambda_k1, lambda_q2, lambda_k2, subln_g, w_fourier, w_out, norm2_g, ffn2_w_gate,
           ffn2_w_up, ffn2_w_down, final_norm_g):
    b, n, d = x.shape
    n_ctx = ctx.shape[1]
    assert (d, w_ada.shape[0]) == (D_MODEL, 1) and b + 1 <= MOD_ROWS
    tiles = TILES
    mod_row = lambda rows: (lambda i: jnp.minimum(i // (n // rows), b))
    row = lambda g: g.reshape(1, -1)

    cc = jnp.zeros((MOD_ROWS, d), F32).at[:b].set(c).at[b].set(c_ctx)
    mod3 = _ada(cc, w_ada[0], b_ada).reshape(MOD_ROWS, 1, N_MOD * d)

    x1, kc, vc = _ffn(x.reshape(b * n, d), mod3, mod_row(tiles.ffn_rows), row(norm1_g), ffn1_w_gate[0], ffn1_w_up[0],
                      ffn1_w_down[0], tm=tiles.ffn_rows, mod_base=0,
                      tail=(ctx.reshape(b * n_ctx, d), row(norm_mix_g), w_in[0]))

    chan_dft = jnp.asarray(_channel_dft_table(), dtype=F32).astype(BF16)
    q, k, v, yc, ys = _inproj(x1, mod3, mod_row(tiles.inproj_rows), row(norm_mix_g), w_in[0], _rope_tables(n), chan_dft,
                              tm=tiles.inproj_rows)

    lamvec = jnp.concatenate([lambda_q1, lambda_k1, lambda_q2, lambda_k2], axis=0)
    seq = lambda a, rows: a.reshape(b, rows, a.shape[-1])
    attend = functools.partial(_attention, lamvec, subln_g.reshape(HEAD_W, 1), seq(q, n), seq(k, n), seq(kc, n_ctx),
                               seq(v, n), seq(vc, n_ctx), tq=tiles.attn_queries, tk=tiles.attn_keys)
    att, smallest_sum = attend(exact_shift=False)
    att = lax.cond(jnp.min(smallest_sum) < UNDERFLOW_GUARD, lambda: attend(exact_shift=True), lambda: att)

    zf = _position_dft(yc, ys, w_fourier[0], tile_rows=tiles.ffn_rows)

    out = _ffn(x1, mod3, mod_row(tiles.ffn_rows), row(norm2_g), ffn2_w_gate[0], ffn2_w_up[0], ffn2_w_down[0],
               tm=tiles.ffn_rows, mod_base=6, mix=(att.reshape(b * n, D_DIFF), zf, w_out[0]),
               tiles_per_seq=n // tiles.ffn_rows, final_g=row(final_norm_g))
    return out.reshape(b, n, d)
```

```python
import functools
import math
from typing import NamedTuple

import numpy as np
import jax
import jax.numpy as jnp
from jax import lax
from jax.experimental import pallas as pl
from jax.experimental.pallas import tpu as pltpu

D_MODEL = 1024
GRID_W = 64
D_FOURIER = 256
D_DIFF = 768
HEAD_DIM = 64
HEAD_W = 2 * HEAD_DIM
N_HEADS = D_DIFF // HEAD_W
FOURIER_GROUP_DIM = 64
D_IN_PROJ = 3 * D_DIFF + D_FOURIER
D_FF = 2816
N_MOD = 9
ROPE_BASE = 10000.0
ROPE_PAIRS = HEAD_DIM // 4
RMS_EPS = 1e-6
ATTN_SCALE = HEAD_DIM ** -0.5
Q_SCALE = ATTN_SCALE * math.log2(math.e)
LAMBDA_INIT = 0.8 - 0.6 * math.exp(-0.3 * 0)
UNDERFLOW_GUARD = 2.0 ** -80

F32 = jnp.float32
BF16 = jnp.bfloat16

V7X_VMEM_LIMIT_BYTES = 56 * 1024 * 1024
F32_SUBLANES = 8
BF16_SUBLANES = 16
STAGE_CHUNK_BYTES = 2 * 1024 * 1024
STAGE_SLOTS = 3
MOD_ROWS = F32_SUBLANES
ADA_ROWS_PER_STEP = 128
DFT_N1 = 16
DFT_UNITS_PER_STEP = 2


class _Tiles(NamedTuple):
    ffn_rows: int = 512
    inproj_rows: int = 1024
    attn_queries: int = 1024
    attn_keys: int = 2048


TILES = _Tiles()
FF_CHUNKS = ((0, 1536), (1536, 1280))


def _const_spec(shape):
    return pl.BlockSpec(shape, lambda *_: (0,) * len(shape), pipeline_mode=pl.Buffered(1))


def _params(semantics):
    return pltpu.CompilerParams(dimension_semantics=semantics, vmem_limit_bytes=V7X_VMEM_LIMIT_BYTES)


def _rmsnorm(x, g):
    return x * lax.rsqrt(jnp.mean(x * x, axis=-1, keepdims=True) + RMS_EPS) * g


def _mod_row(mod_ref, k):
    return mod_ref[0, :, k * D_MODEL:(k + 1) * D_MODEL]


def _ada_kernel(cc_ref, w_ref, b_ref, o_ref):
    @pl.when(pl.program_id(0) == 0)
    def _():
        o_ref[...] = jnp.broadcast_to(b_ref[...], o_ref.shape)

    cc = cc_ref[...]
    s = cc * jax.nn.sigmoid(cc)
    o_ref[...] += jnp.dot(s.astype(BF16), w_ref[...].astype(BF16), preferred_element_type=F32)


def _ada(cc, w_ada, b_ada):
    k, n_out = w_ada.shape
    tk = ADA_ROWS_PER_STEP
    return pl.pallas_call(
        _ada_kernel,
        grid=(k // tk,),
        in_specs=[pl.BlockSpec((MOD_ROWS, tk), lambda j: (0, j)),
                  pl.BlockSpec((tk, n_out), lambda j: (j, 0)),
                  _const_spec((1, n_out))],
        out_specs=pl.BlockSpec((MOD_ROWS, n_out), lambda j: (0, 0)),
        out_shape=jax.ShapeDtypeStruct((MOD_ROWS, n_out), F32),
        compiler_params=_params(("arbitrary",)),
        name="ada",
    )(cc, w_ada, b_ada)


def _stage_rows(k, n):
    fits = [r for r in range(BF16_SUBLANES, k + 1, BF16_SUBLANES) if k % r == 0 and r * n * 4 <= STAGE_CHUNK_BYTES]
    return max(fits)


def _stage_weight(src_hbm, first_col, dst_ref, stage_ref, sem_ref):
    slots, rows, width = stage_ref.shape[0], stage_ref.shape[1], dst_ref.shape[1]
    n_chunks = src_hbm.shape[0] // rows

    def copy(c):
        return pltpu.make_async_copy(src_hbm.at[pl.ds(c * rows, rows), pl.ds(first_col, width)],
                                     stage_ref.at[c % slots, :, pl.ds(0, width)], sem_ref.at[c % slots])

    for c in range(min(slots - 1, n_chunks)):
        copy(c).start()
    for c in range(n_chunks):
        if c + slots - 1 < n_chunks:
            copy(c + slots - 1).start()
        copy(c).wait()
        dst_ref[c * rows:(c + 1) * rows, :] = stage_ref[c % slots, :, 0:width].astype(BF16)


class _StagedWeights:
    def __init__(self, weights):
        entries = [w if isinstance(w, tuple) else (w, (0, w.shape[1])) for w in weights]
        self.weights = [w for w, _ in entries]
        self.first_cols = [first for _, (first, _) in entries]
        shapes = [(w.shape[0], width) for w, (_, width) in entries]
        self.in_specs = [pl.BlockSpec(memory_space=pl.ANY)] * len(self.weights)
        stages, self.stage_of = [], [None] * len(shapes)
        for j in sorted(range(len(shapes)), key=lambda j: -shapes[j][1]):
            k, n = shapes[j]
            fits = [s for s, (rows, width) in enumerate(stages)
                    if width >= n and k % rows == 0 and 2 * rows >= _stage_rows(k, n)]
            if not fits:
                stages.append((_stage_rows(k, n), n))
                fits = [len(stages) - 1]
            self.stage_of[j] = fits[0]
        self.scratch = ([pltpu.VMEM(s, BF16) for s in shapes]
                        + [pltpu.VMEM((STAGE_SLOTS,) + s, F32) for s in stages]
                        + [pltpu.SemaphoreType.DMA((STAGE_SLOTS,))])

    def load(self, hbm_refs, scratch_refs):
        n = len(self.weights)
        dst, stages, sem = scratch_refs[:n], scratch_refs[n:-1], scratch_refs[-1]

        @pl.when(pl.program_id(0) == 0)
        def _():
            for src, first, d, s in zip(hbm_refs, self.first_cols, dst, self.stage_of):
                _stage_weight(src, first, d, stages[s], sem)

        return dst


def _ffn_kernel(*refs, staged, mod_base, has_mix, final_norm, n_main):
    n_w = len(staged.weights)
    n_scratch = len(staged.scratch)
    x_ref, mod_ref, g_ref = refs[:3]
    w_hbm = refs[3:3 + n_w]
    rest = list(refs[3 + n_w:len(refs) - n_scratch])
    weights = staged.load(w_hbm, refs[len(refs) - n_scratch:])
    wg_ref, wu_ref, wd_ref = weights[:3]
    has_tail = n_main is not None
    if has_tail:
        tail_ref, tail_g_ref = rest[:2]
        kc_ref, vc_ref = rest[-2:]
        rest = rest[2:-2]
        wkv_ref = weights[-1]
        in_tail = pl.program_id(0) >= n_main
    if has_mix:
        att_ref, zf_ref = rest[:2]
        rest = rest[2:]
        wo_ref = weights[3]
    if final_norm:
        gf_ref = rest[0]
        rest = rest[1:]
    (o_ref,) = rest

    x = x_ref[...]
    if has_tail:
        x = jnp.where(in_tail, tail_ref[...], x)
    if has_mix:
        mix = jnp.dot(att_ref[...], wo_ref[:D_DIFF, :], preferred_element_type=F32)
        mix += jnp.dot(zf_ref[...], wo_ref[D_DIFF:, :], preferred_element_type=F32)
        x = x + _mod_row(mod_ref, 5) * mix
    h = _rmsnorm(x, g_ref[...]) * (1.0 + _mod_row(mod_ref, mod_base + 1)) + _mod_row(mod_ref, mod_base)
    h = h.astype(BF16)
    acc = None
    for start, width in FF_CHUNKS:
        gate = jnp.dot(h, wg_ref[:, start:start + width], preferred_element_type=F32)
        up = jnp.dot(h, wu_ref[:, start:start + width], preferred_element_type=F32)
        a = (gate * jax.nn.sigmoid(gate) * up).astype(BF16)
        part = jnp.dot(a, wd_ref[start:start + width, :], preferred_element_type=F32)
        acc = part if acc is None else acc + part
    x = x + (0.5 * _mod_row(mod_ref, mod_base + 2)) * acc
    if final_norm:
        x = _rmsnorm(x, gf_ref[...])
    if has_tail:
        @pl.when(in_tail)
        def _():
            hc = _rmsnorm(x, tail_g_ref[...]) * (1.0 + _mod_row(mod_ref, 4)) + _mod_row(mod_ref, 3)
            kv = jnp.dot(hc.astype(BF16), wkv_ref[...], preferred_element_type=F32)
            kc_ref[...] = kv[:, :D_DIFF].astype(BF16)
            vc_ref[...] = kv[:, D_DIFF:].astype(BF16)

        @pl.when(jnp.logical_not(in_tail))
        def _():
            o_ref[...] = x
    else:
        o_ref[...] = x


def _ffn(x, mod3, mod_index, g, wg, wu, wd, *, tm, mod_base, tail=None, mix=None, tiles_per_seq=None, final_g=None):
    t = x.shape[0]
    n_main = t // tm
    n_tiles = n_main + (tail[0].shape[0] // tm if tail is not None else 0)
    tile = lambda w: pl.BlockSpec((tm, w), lambda i: (jnp.minimum(i, n_main - 1), 0))
    tail_tile = lambda w: pl.BlockSpec((tm, w), lambda i: (jnp.maximum(i - n_main, 0), 0))
    staged = _StagedWeights([wg, wu, wd] + ([mix[2]] if mix is not None else [])
                            + ([(tail[2], (D_DIFF, 2 * D_DIFF))] if tail is not None else []))
    in_specs = [tile(D_MODEL),
                pl.BlockSpec((1, 1, N_MOD * D_MODEL), lambda i: (mod_index(i), 0, 0)),
                _const_spec((1, D_MODEL))] + staged.in_specs
    args = [x, mod3, g] + staged.weights
    out_specs, out_shape = tile(D_MODEL), jax.ShapeDtypeStruct((t, D_MODEL), F32)
    if tail is not None:
        ctx, tail_g, _ = tail
        in_specs += [tail_tile(D_MODEL), _const_spec((1, D_MODEL))]
        args += [ctx, tail_g]
        out_specs = [out_specs, tail_tile(D_DIFF), tail_tile(D_DIFF)]
        out_shape = [out_shape] + [jax.ShapeDtypeStruct((ctx.shape[0], D_DIFF), BF16)] * 2
    if mix is not None:
        att, zf, _ = mix
        by_batch = pl.BlockSpec((tm, D_FOURIER), lambda i: (i % tiles_per_seq, i // tiles_per_seq))
        in_specs += [tile(D_DIFF), by_batch]
        args += [att, zf]
    if final_g is not None:
        in_specs.append(_const_spec((1, D_MODEL)))
        args.append(final_g)
    kern = functools.partial(_ffn_kernel, staged=staged, mod_base=mod_base, has_mix=mix is not None,
                             final_norm=final_g is not None, n_main=n_main if tail is not None else None)
    return pl.pallas_call(
        kern,
        grid=(n_tiles,),
        in_specs=in_specs,
        out_specs=out_specs,
        out_shape=out_shape,
        scratch_shapes=staged.scratch,
        compiler_params=_params(("arbitrary",)),
        name="ffn_mix" if mix is not None else "ffn",
    )(*args)


def _rope(x, cos, sin_lo, sin_hi):
    return (x * cos + pltpu.roll(x, HEAD_W - ROPE_PAIRS, 1) * sin_lo
            + pltpu.roll(x, ROPE_PAIRS, 1) * sin_hi)


def _inproj_kernel(x_ref, mod_ref, g_ref, w_hbm, by_row_ref, by_col_ref, dft_ref,
                   q_ref, k_ref, v_ref, yc_ref, ys_ref, *scratch, staged, tiles_per_seq):
    (w_ref,) = staged.load([w_hbm], scratch)
    h = _rmsnorm(x_ref[...], g_ref[...]) * (1.0 + _mod_row(mod_ref, 4)) + _mod_row(mod_ref, 3)
    p = jnp.dot(h.astype(BF16), w_ref[...], preferred_element_type=F32)
    tm, grid_w = x_ref.shape[0], by_col_ref.shape[1]
    first_row = pl.multiple_of((pl.program_id(0) % tiles_per_seq) * (tm // grid_w), tm // grid_w)
    cos, slo, shi = [(by_row_ref[j, pl.ds(first_row, tm // grid_w), :][:, None, :]
                      + by_col_ref[j][None, :, :]).reshape(tm, HEAD_W) for j in range(3)]
    for hd in range(N_HEADS):
        lo = hd * HEAD_W
        q_ref[:, lo:lo + HEAD_W] = (_rope(p[:, lo:lo + HEAD_W], cos, slo, shi) * Q_SCALE).astype(BF16)
        k_ref[:, lo:lo + HEAD_W] = _rope(p[:, D_DIFF + lo:D_DIFF + lo + HEAD_W], cos, slo, shi).astype(BF16)
    v_ref[...] = p[:, 2 * D_DIFF:3 * D_DIFF].astype(BF16)
    y = jnp.dot(p[:, 3 * D_DIFF:].astype(BF16), dft_ref[...], preferred_element_type=F32)
    yc_ref[...] = y[:, :D_FOURIER].astype(BF16)
    ys_ref[...] = y[:, D_FOURIER:].astype(BF16)


def _inproj(x, mod3, mod_index, g, w_in, rope, chan_dft, *, tm):
    t = x.shape[0]
    by_row, by_col = rope
    tiles_per_seq = by_row.shape[1] * by_col.shape[1] // tm
    tile = lambda w: pl.BlockSpec((tm, w), lambda i: (i, 0))
    by_batch = pl.BlockSpec((tm, D_FOURIER), lambda i: (i % tiles_per_seq, i // tiles_per_seq))
    staged = _StagedWeights([w_in])
    wide = jax.ShapeDtypeStruct((t, D_DIFF), BF16)
    narrow = jax.ShapeDtypeStruct((tiles_per_seq * tm, t // (tiles_per_seq * tm) * D_FOURIER), BF16)
    return pl.pallas_call(
        functools.partial(_inproj_kernel, staged=staged, tiles_per_seq=tiles_per_seq),
        grid=(t // tm,),
        in_specs=[tile(D_MODEL),
                  pl.BlockSpec((1, 1, N_MOD * D_MODEL), lambda i: (mod_index(i), 0, 0)),
                  _const_spec((1, D_MODEL))] + staged.in_specs
                 + [_const_spec(by_row.shape), _const_spec(by_col.shape), _const_spec((D_FOURIER, 2 * D_FOURIER))],
        out_specs=[tile(D_DIFF)] * 3 + [by_batch] * 2,
        out_shape=[wide] * 3 + [narrow] * 2,
        scratch_shapes=staged.scratch,
        compiler_params=_params(("arbitrary",)),
        name="inproj",
    )(x, mod3, g, w_in, by_row, by_col, chan_dft)


def _max_key_norm_sq(kk):
    sq = (kk.astype(F32) ** 2).astype(BF16)
    r = lax.broadcasted_iota(jnp.int32, (HEAD_W, HEAD_W), 0)
    c = lax.broadcasted_iota(jnp.int32, (HEAD_W, HEAD_W), 1)
    sel = jnp.where((c == 0) == (r < HEAD_DIM), 1.0, 0.0) * jnp.where(c < 2, 1.0, 0.0)
    sums = jnp.dot(sq, sel.astype(BF16), preferred_element_type=F32)
    biggest = jnp.max(sums, axis=0, keepdims=True) * (1.0 + 2.0 ** -7)
    return biggest[:, 0:1], biggest[:, 1:2]


def _attn_kernel(lam_ref, g_ref, q_ref, k_ref, kc_ref, v_ref, vc_ref, o_ref, *rest, tk, exact_shift):
    lmin_ref = None if exact_shift else rest[0]
    vt_ref, keys_ref, kn_ref = rest[-3:]
    tq = q_ref.shape[1]
    n_latent = k_ref.shape[1]

    @pl.when(pl.program_id(2) == 0)
    def _():
        if not exact_shift:
            lmin_ref[...] = jnp.full(lmin_ref.shape, jnp.finfo(F32).max, F32)
        keys_ref[0:n_latent] = k_ref[0]
        keys_ref[n_latent:] = kc_ref[0]
        vt_ref[:, 0:n_latent] = v_ref[0].T
        vt_ref[:, n_latent:] = vc_ref[0].T
        (a1, a2), (b1, b2) = _max_key_norm_sq(k_ref[0]), _max_key_norm_sq(kc_ref[0])
        kn_ref[...] = jnp.sqrt(jnp.concatenate([jnp.broadcast_to(jnp.maximum(a1, b1), (1, tq)),
                                                 jnp.broadcast_to(jnp.maximum(a2, b2), (1, tq))], axis=1))

    lv = lam_ref[...]
    lam = (jnp.exp(jnp.sum(lv[0:1] * lv[1:2], axis=1, keepdims=True))
           - jnp.exp(jnp.sum(lv[2:3] * lv[3:4], axis=1, keepdims=True)) + LAMBDA_INIT)

    qt = q_ref[0].T
    none = jnp.zeros((HEAD_DIM, tq), BF16)
    qcat = jnp.concatenate([jnp.concatenate([qt[:HEAD_DIM], none], axis=0),
                            jnp.concatenate([none, qt[HEAD_DIM:]], axis=0)], axis=1)
    n_keys = keys_ref.shape[0]
    bounds = [0] + [n_keys - (n_keys // tk - 1 - c) * tk for c in range(n_keys // tk)]
    chunks = [(keys_ref[lo:hi, :], vt_ref[:, lo:hi]) for lo, hi in zip(bounds[:-1], bounds[1:])]

    def scores(kk):
        return jnp.dot(kk, qcat, preferred_element_type=F32)

    def softmax_sums(shift):
        lsum, acc = 0.0, 0.0
        for kk, vt in chunks:
            e = jnp.exp2(scores(kk) - shift)
            lsum += jnp.sum(e.reshape(-1, F32_SUBLANES, 2 * tq), axis=0)
            acc += jnp.dot(vt, e.astype(BF16), preferred_element_type=F32)
        return jnp.sum(lsum, axis=0, keepdims=True), acc

    def finish(l, acc):
        ot = acc[:, :tq] * (1.0 / l[:, :tq]) - acc[:, tq:] * (lam / l[:, tq:])
        scale = lax.rsqrt(jnp.mean(ot * ot, axis=0, keepdims=True) + RMS_EPS)
        o_ref[0] = (ot * scale * (g_ref[...] * (1.0 - LAMBDA_INIT))).astype(BF16).T

    if exact_shift:
        m = None
        for kk, _ in chunks:
            cm = jnp.max(scores(kk), axis=0, keepdims=True)
            m = cm if m is None else jnp.maximum(m, cm)
        finish(*softmax_sums(m))
    else:
        sq = qt.astype(F32) ** 2
        qn = jnp.sqrt(jnp.concatenate([jnp.sum(sq[:HEAD_DIM], axis=0, keepdims=True),
                                       jnp.sum(sq[HEAD_DIM:], axis=0, keepdims=True)], axis=1))
        l, acc = softmax_sums(qn * kn_ref[...])
        finish(l, acc)
        smallest = l[:, 0:HEAD_W]
        for lo in range(HEAD_W, 2 * tq, HEAD_W):
            smallest = jnp.minimum(smallest, l[:, lo:lo + HEAD_W])
        lmin_ref[...] = jnp.minimum(lmin_ref[...], smallest)


def _attention(lamvec, subln_g, q, k, kc, v, vc, *, tq, tk, exact_shift):
    b, n, _ = q.shape
    c = kc.shape[1]
    blocks = n // tq
    head_block = lambda rows: pl.BlockSpec((1, rows, HEAD_W), lambda bi, hi, qi: (bi, 0, hi))
    query_block = pl.BlockSpec((1, tq, HEAD_W), lambda bi, hi, qi: (bi, qi, hi))
    out_specs, out_shape = query_block, jax.ShapeDtypeStruct((b, n, D_DIFF), BF16)
    if not exact_shift:
        out_specs = [out_specs, pl.BlockSpec((F32_SUBLANES, HEAD_W), lambda bi, hi, qi: (bi * N_HEADS + hi, 0))]
        out_shape = [out_shape, jax.ShapeDtypeStruct((b * N_HEADS * F32_SUBLANES, HEAD_W), F32)]
    return pl.pallas_call(
        functools.partial(_attn_kernel, tk=tk, exact_shift=exact_shift),
        grid=(b, N_HEADS, blocks),
        in_specs=[_const_spec(lamvec.shape), _const_spec((HEAD_W, 1)), query_block,
                  head_block(n), head_block(c), head_block(n), head_block(c)],
        out_specs=out_specs,
        out_shape=out_shape,
        scratch_shapes=[pltpu.VMEM((HEAD_W, n + c), BF16), pltpu.VMEM((n + c, HEAD_W), BF16),
                        pltpu.VMEM((1, 2 * tq), F32)],
        compiler_params=_params(("parallel", "parallel", "arbitrary")),
        name="diff_attn_exact" if exact_shift else "diff_attn",
    )(lamvec, subln_g, q, k, kc, v, vc)


def _dft_kernel(yc_ref, ys_ref, m1_ref, twr_ref, twi_ref, w2_ref, wf_ref, perm_ref, o_ref, br_ref, bi_ref):
    n1, cols = yc_ref.shape[0], yc_ref.shape[2]
    g = m1_ref.shape[0] // (2 * n1)
    per_step = yc_ref.shape[1] // g
    n2 = br_ref.shape[0] // n1
    steps1, steps2 = n2 // g // per_step, n1 // per_step
    i = pl.program_id(0)

    @pl.when(i < steps1)
    def _():
        for s in range(per_step):
            group = i * per_step + s
            mine = slice(s * g, (s + 1) * g)
            u = jnp.concatenate([yc_ref[:, mine, :].reshape(n1 * g, cols),
                                 ys_ref[:, mine, :].reshape(n1 * g, cols)], axis=0)
            a = jnp.dot(m1_ref[...], u, preferred_element_type=F32)
            ar, ai = a[:n1 * g], a[n1 * g:]
            tr, ti = twr_ref[group], twi_ref[group]
            br = (ar * tr - ai * ti).astype(BF16)
            bi = (ar * ti + ai * tr).astype(BF16)
            for c in range(n1):
                dst = pl.ds(pl.multiple_of(c * n2 + group * g, g), g)
                br_ref[dst, :] = br[c * g:(c + 1) * g]
                bi_ref[dst, :] = bi[c * g:(c + 1) * g]

    @pl.when((i >= steps1) & (i < steps1 + steps2))
    def _():
        w = wf_ref.shape[0]
        wf = wf_ref[...].astype(BF16)
        for s in range(per_step):
            src = pl.ds(pl.multiple_of(((i - steps1) * per_step + s) * n2, n2), n2)
            b = jnp.concatenate([br_ref[src, :], bi_ref[src, :]], axis=0)
            x = jnp.dot(w2_ref[...], b, preferred_element_type=F32)
            br_ref[src, :] = jnp.concatenate(
                [jnp.dot(x[:, lo:lo + w].astype(BF16), wf, preferred_element_type=F32).astype(BF16)
                 for lo in range(0, cols, w)], axis=1)

    @pl.when(i >= steps1 + steps2)
    def _():
        run = o_ref.shape[0] // n1
        first = (i - steps1 - steps2) * run
        rows = jnp.concatenate([br_ref[pl.ds(pl.multiple_of(c * n2 + first, run), run), :] for c in range(n1)], axis=0)
        o_ref[...] = jnp.dot(perm_ref[...], rows, preferred_element_type=F32).astype(BF16)


def _position_dft(yc, ys, wf, *, tile_rows):
    n, cols = yc.shape
    n1, g, per_step = DFT_N1, BF16_SUBLANES, DFT_UNITS_PER_STEP
    n2 = n // n1
    steps1, steps2, steps3 = n2 // g // per_step, n1 // per_step, n // tile_rows
    m1, twr, twi, w2 = _position_dft_tables(n, n1, g)
    perm = _row_permutation(n1, tile_rows)
    group = pl.BlockSpec((n1, g * per_step, cols), lambda i: (0, jnp.minimum(i, steps1 - 1), 0))
    return pl.pallas_call(
        _dft_kernel,
        grid=(steps1 + steps2 + steps3,),
        in_specs=[group, group, _const_spec(m1.shape), _const_spec(twr.shape), _const_spec(twi.shape),
                  _const_spec(w2.shape), _const_spec(wf.shape), _const_spec(perm.shape)],
        out_specs=pl.BlockSpec((tile_rows, cols), lambda i: (jnp.maximum(i - steps1 - steps2, 0), 0)),
        out_shape=jax.ShapeDtypeStruct((n, cols), BF16),
        scratch_shapes=[pltpu.VMEM((n, cols), BF16), pltpu.VMEM((n, cols), BF16)],
        compiler_params=_params(("arbitrary",)),
        name="position_dft",
    )(yc.reshape(n1, n2, cols), ys.reshape(n1, n2, cols), m1, twr, twi, w2, wf, perm)


def _rope_tables(n):
    rows = n // GRID_W
    lane = np.arange(HEAD_W)
    sub = lane % HEAD_DIM
    on_row_axis = jnp.asarray(sub // (2 * ROPE_PAIRS) == 0)[None, :]
    second_half = jnp.asarray((sub % (2 * ROPE_PAIRS)) // ROPE_PAIRS == 1)[None, :]
    inv_freq = ROPE_BASE ** (-jnp.asarray(sub % ROPE_PAIRS, dtype=F32) / ROPE_PAIRS)
    kinds = ((jnp.cos, True), (lambda a: -jnp.sin(a), ~second_half), (jnp.sin, second_half))

    def side(size, mine):
        ang = jnp.arange(size, dtype=F32)[:, None] * inv_freq[None, :]
        return jnp.stack([jnp.where(mine & keep, fn(ang), 0.0) for fn, keep in kinds])

    return side(rows, on_row_axis), side(GRID_W, ~on_row_axis)


def _channel_dft_table():
    c = np.arange(FOURIER_GROUP_DIM)
    ang = 2.0 * np.pi * np.outer(c, c) / FOURIER_GROUP_DIM
    eye = np.eye(D_FOURIER // FOURIER_GROUP_DIM)
    scale = FOURIER_GROUP_DIM ** -0.5
    return np.concatenate([np.kron(eye, np.cos(ang)), np.kron(eye, np.sin(ang))], axis=1) * scale


def _position_dft_tables(n, n1, g):
    n2 = n // n1
    w1 = np.exp(-2j * np.pi * np.outer(np.arange(n1), np.arange(n1)) / n1)
    m = np.kron(w1, np.eye(g)) * n ** -0.5
    m1 = np.block([[m.real, m.imag], [m.imag, -m.real]])
    b = (g * np.arange(n2 // g)[:, None, None] + np.arange(g)[None, None, :])
    tw = np.exp(-2j * np.pi * b * np.arange(n1)[None, :, None] / n).reshape(n2 // g, n1 * g, 1)
    ang2 = 2.0 * np.pi * np.outer(np.arange(n2), np.arange(n2)) / n2
    w2 = np.concatenate([np.cos(ang2), np.sin(ang2)], axis=1)
    f32 = lambda t: jnp.asarray(t, dtype=F32)
    return f32(m1).astype(BF16), f32(tw.real), f32(tw.imag), f32(w2).astype(BF16)


def _row_permutation(n1, rows):
    d_per_tile = rows // n1
    p = np.zeros((rows, rows), np.float32)
    c, d = np.meshgrid(np.arange(n1), np.arange(d_per_tile), indexing="ij")
    p[(c + n1 * d).ravel(), (c * d_per_tile + d).ravel()] = 1.0
    return jnp.asarray(p).astype(BF16)


def kernel(x, c, ctx, c_ctx, w_ada, b_ada, norm1_g, ffn1_w_gate, ffn1_w_up, ffn1_w_down, norm_mix_g, w_in,
           lambda_q1, lambda_k1, lambda_q2, lambda_k2, subln_g, w_fourier, w_out, norm2_g, ffn2_w_gate,
           ffn2_w_up, ffn2_w_down, final_norm_g):
    b, n, d = x.shape
    n_ctx = ctx.shape[1]
    assert (d, w_ada.shape[0]) == (D_MODEL, 1) and b + 1 <= MOD_ROWS
    tiles = TILES
    mod_row = lambda rows: (lambda i: jnp.minimum(i // (n // rows), b))
    row = lambda g: g.reshape(1, -1)

    cc = jnp.zeros((MOD_ROWS, d), F32).at[:b].set(c).at[b].set(c_ctx)
    mod3 = _ada(cc, w_ada[0], b_ada).reshape(MOD_ROWS, 1, N_MOD * d)

    x1, kc, vc = _ffn(x.reshape(b * n, d), mod3, mod_row(tiles.ffn_rows), row(norm1_g), ffn1_w_gate[0], ffn1_w_up[0],
                      ffn1_w_down[0], tm=tiles.ffn_rows, mod_base=0,
                      tail=(ctx.reshape(b * n_ctx, d), row(norm_mix_g), w_in[0]))

    chan_dft = jnp.asarray(_channel_dft_table(), dtype=F32).astype(BF16)
    q, k, v, yc, ys = _inproj(x1, mod3, mod_row(tiles.inproj_rows), row(norm_mix_g), w_in[0], _rope_tables(n), chan_dft,
                              tm=tiles.inproj_rows)

    lamvec = jnp.concatenate([lambda_q1, lambda_k1, lambda_q2, lambda_k2], axis=0)
    seq = lambda a, rows: a.reshape(b, rows, a.shape[-1])
    attend = functools.partial(_attention, lamvec, subln_g.reshape(HEAD_W, 1), seq(q, n), seq(k, n), seq(kc, n_ctx),
                               seq(v, n), seq(vc, n_ctx), tq=tiles.attn_queries, tk=tiles.attn_keys)
    att, smallest_sum = attend(exact_shift=False)
    att = lax.cond(jnp.min(smallest_sum) < UNDERFLOW_GUARD, lambda: attend(exact_shift=True), lambda: att)

    zf = _position_dft(yc, ys, w_fourier[0], tile_rows=tiles.ffn_rows)

    out = _ffn(x1, mod3, mod_row(tiles.ffn_rows), row(norm2_g), ffn2_w_gate[0], ffn2_w_up[0], ffn2_w_down[0],
               tm=tiles.ffn_rows, mod_base=6, mix=(att.reshape(b * n, D_DIFF), zf, w_out[0]),
               tiles_per_seq=n // tiles.ffn_rows, final_g=row(final_norm_g))
    return out.reshape(b, n, d)
```

```python
import functools
import math
from typing import NamedTuple

import numpy as np
import jax
import jax.numpy as jnp
from jax import lax
from jax.experimental import pallas as pl
from jax.experimental.pallas import tpu as pltpu

D_MODEL = 1024
GRID_W = 64
D_FOURIER = 256
D_DIFF = 768
HEAD_DIM = 64
HEAD_W = 2 * HEAD_DIM
N_HEADS = D_DIFF // HEAD_W
FOURIER_GROUP_DIM = 64
D_IN_PROJ = 3 * D_DIFF + D_FOURIER
D_FF = 2816
N_MOD = 9
ROPE_BASE = 10000.0
ROPE_PAIRS = HEAD_DIM // 4
RMS_EPS = 1e-6
ATTN_SCALE = HEAD_DIM ** -0.5
Q_SCALE = ATTN_SCALE * math.log2(math.e)
LAMBDA_INIT = 0.8 - 0.6 * math.exp(-0.3 * 0)
UNDERFLOW_GUARD = 2.0 ** -80

F32 = jnp.float32
BF16 = jnp.bfloat16

V7X_VMEM_LIMIT_BYTES = 56 * 1024 * 1024
F32_SUBLANES = 8
BF16_SUBLANES = 16
STAGE_CHUNK_BYTES = 2 * 1024 * 1024
STAGE_SLOTS = 3
MOD_ROWS = F32_SUBLANES
ADA_ROWS_PER_STEP = 128
DFT_N1 = 16
DFT_UNITS_PER_STEP = 2


class _Tiles(NamedTuple):
    ffn_rows: int = 512
    inproj_rows: int = 1024
    attn_queries: int = 1024
    attn_keys: int = 2048


TILES = _Tiles()
FF_CHUNKS = ((0, 1536), (1536, 1280))


def _const_spec(shape):
    return pl.BlockSpec(shape, lambda *_: (0,) * len(shape), pipeline_mode=pl.Buffered(1))


def _params(semantics):
    return pltpu.CompilerParams(dimension_semantics=semantics, vmem_limit_bytes=V7X_VMEM_LIMIT_BYTES)


def _rmsnorm(x, g):
    return x * lax.rsqrt(jnp.mean(x * x, axis=-1, keepdims=True) + RMS_EPS) * g


def _mod_row(mod_ref, k):
    return mod_ref[0, :, k * D_MODEL:(k + 1) * D_MODEL]


def _ada_kernel(cc_ref, w_ref, b_ref, o_ref):
    @pl.when(pl.program_id(0) == 0)
    def _():
        o_ref[...] = jnp.broadcast_to(b_ref[...], o_ref.shape)

    cc = cc_ref[...]
    s = cc * jax.nn.sigmoid(cc)
    o_ref[...] += jnp.dot(s.astype(BF16), w_ref[...].astype(BF16), preferred_element_type=F32)


def _ada(cc, w_ada, b_ada):
    k, n_out = w_ada.shape
    tk = ADA_ROWS_PER_STEP
    return pl.pallas_call(
        _ada_kernel,
        grid=(k // tk,),
        in_specs=[pl.BlockSpec((MOD_ROWS, tk), lambda j: (0, j)),
                  pl.BlockSpec((tk, n_out), lambda j: (j, 0)),
                  _const_spec((1, n_out))],
        out_specs=pl.BlockSpec((MOD_ROWS, n_out), lambda j: (0, 0)),
        out_shape=jax.ShapeDtypeStruct((MOD_ROWS, n_out), F32),
        compiler_params=_params(("arbitrary",)),
        name="ada",
    )(cc, w_ada, b_ada)


def _stage_rows(k, n):
    fits = [r for r in range(BF16_SUBLANES, k + 1, BF16_SUBLANES) if k % r == 0 and r * n * 4 <= STAGE_CHUNK_BYTES]
    return max(fits)


def _stage_weight(src_hbm, first_col, dst_ref, stage_ref, sem_ref):
    slots, rows, width = stage_ref.shape[0], stage_ref.shape[1], dst_ref.shape[1]
    n_chunks = src_hbm.shape[0] // rows

    def copy(c):
        return pltpu.make_async_copy(src_hbm.at[pl.ds(c * rows, rows), pl.ds(first_col, width)],
                                     stage_ref.at[c % slots, :, pl.ds(0, width)], sem_ref.at[c % slots])

    for c in range(min(slots - 1, n_chunks)):
        copy(c).start(priority=c % 2)
    for c in range(n_chunks):
        if c + slots - 1 < n_chunks:
            copy(c + slots - 1).start(priority=(c + slots - 1) % 2)
        copy(c).wait()
        dst_ref[c * rows:(c + 1) * rows, :] = stage_ref[c % slots, :, 0:width].astype(BF16)


class _StagedWeights:
    def __init__(self, weights):
        entries = [w if isinstance(w, tuple) else (w, (0, w.shape[1])) for w in weights]
        self.weights = [w for w, _ in entries]
        self.first_cols = [first for _, (first, _) in entries]
        shapes = [(w.shape[0], width) for w, (_, width) in entries]
        self.in_specs = [pl.BlockSpec(memory_space=pl.ANY)] * len(self.weights)
        stages, self.stage_of = [], [None] * len(shapes)
        for j in sorted(range(len(shapes)), key=lambda j: -shapes[j][1]):
            k, n = shapes[j]
            fits = [s for s, (rows, width) in enumerate(stages)
                    if width >= n and k % rows == 0 and 2 * rows >= _stage_rows(k, n)]
            if not fits:
                stages.append((_stage_rows(k, n), n))
                fits = [len(stages) - 1]
            self.stage_of[j] = fits[0]
        self.scratch = ([pltpu.VMEM(s, BF16) for s in shapes]
                        + [pltpu.VMEM((STAGE_SLOTS,) + s, F32) for s in stages]
                        + [pltpu.SemaphoreType.DMA((STAGE_SLOTS,))])

    def load(self, hbm_refs, scratch_refs):
        n = len(self.weights)
        dst, stages, sem = scratch_refs[:n], scratch_refs[n:-1], scratch_refs[-1]

        @pl.when(pl.program_id(0) == 0)
        def _():
            for src, first, d, s in zip(hbm_refs, self.first_cols, dst, self.stage_of):
                _stage_weight(src, first, d, stages[s], sem)

        return dst


def _ffn_kernel(*refs, staged, mod_base, has_mix, final_norm, n_main):
    n_w = len(staged.weights)
    n_scratch = len(staged.scratch)
    x_ref, mod_ref, g_ref = refs[:3]
    w_hbm = refs[3:3 + n_w]
    rest = list(refs[3 + n_w:len(refs) - n_scratch])
    weights = staged.load(w_hbm, refs[len(refs) - n_scratch:])
    wg_ref, wu_ref, wd_ref = weights[:3]
    has_tail = n_main is not None
    if has_tail:
        tail_ref, tail_g_ref = rest[:2]
        kc_ref, vc_ref = rest[-2:]
        rest = rest[2:-2]
        wkv_ref = weights[-1]
        in_tail = pl.program_id(0) >= n_main
    if has_mix:
        att_ref, zf_ref = rest[:2]
        rest = rest[2:]
        wo_ref = weights[3]
    if final_norm:
        gf_ref = rest[0]
        rest = rest[1:]
    (o_ref,) = rest

    x = x_ref[...]
    if has_tail:
        x = jnp.where(in_tail, tail_ref[...], x)
    if has_mix:
        mix = jnp.dot(att_ref[...], wo_ref[:D_DIFF, :], preferred_element_type=F32)
        mix += jnp.dot(zf_ref[...], wo_ref[D_DIFF:, :], preferred_element_type=F32)
        x = x + _mod_row(mod_ref, 5) * mix
    h = _rmsnorm(x, g_ref[...]) * (1.0 + _mod_row(mod_ref, mod_base + 1)) + _mod_row(mod_ref, mod_base)
    h = h.astype(BF16)
    acc = None
    for start, width in FF_CHUNKS:
        gate = jnp.dot(h, wg_ref[:, start:start + width], preferred_element_type=F32)
        up = jnp.dot(h, wu_ref[:, start:start + width], preferred_element_type=F32)
        a = (gate * jax.nn.sigmoid(gate) * up).astype(BF16)
        part = jnp.dot(a, wd_ref[start:start + width, :], preferred_element_type=F32)
        acc = part if acc is None else acc + part
    x = x + (0.5 * _mod_row(mod_ref, mod_base + 2)) * acc
    if final_norm:
        x = _rmsnorm(x, gf_ref[...])
    if has_tail:
        @pl.when(in_tail)
        def _():
            hc = _rmsnorm(x, tail_g_ref[...]) * (1.0 + _mod_row(mod_ref, 4)) + _mod_row(mod_ref, 3)
            kv = jnp.dot(hc.astype(BF16), wkv_ref[...], preferred_element_type=F32)
            kc_ref[...] = kv[:, :D_DIFF].astype(BF16)
            vc_ref[...] = kv[:, D_DIFF:].astype(BF16)

        @pl.when(jnp.logical_not(in_tail))
        def _():
            o_ref[...] = x
    else:
        o_ref[...] = x


def _ffn(x, mod3, mod_index, g, wg, wu, wd, *, tm, mod_base, tail=None, mix=None, tiles_per_seq=None, final_g=None):
    t = x.shape[0]
    n_main = t // tm
    n_tiles = n_main + (tail[0].shape[0] // tm if tail is not None else 0)
    tile = lambda w: pl.BlockSpec((tm, w), lambda i: (jnp.minimum(i, n_main - 1), 0))
    tail_tile = lambda w: pl.BlockSpec((tm, w), lambda i: (jnp.maximum(i - n_main, 0), 0))
    staged = _StagedWeights([wg, wu, wd] + ([mix[2]] if mix is not None else [])
                            + ([(tail[2], (D_DIFF, 2 * D_DIFF))] if tail is not None else []))
    in_specs = [tile(D_MODEL),
                pl.BlockSpec((1, 1, N_MOD * D_MODEL), lambda i: (mod_index(i), 0, 0)),
                _const_spec((1, D_MODEL))] + staged.in_specs
    args = [x, mod3, g] + staged.weights
    out_specs, out_shape = tile(D_MODEL), jax.ShapeDtypeStruct((t, D_MODEL), F32)
    if tail is not None:
        ctx, tail_g, _ = tail
        in_specs += [tail_tile(D_MODEL), _const_spec((1, D_MODEL))]
        args += [ctx, tail_g]
        out_specs = [out_specs, tail_tile(D_DIFF), tail_tile(D_DIFF)]
        out_shape = [out_shape] + [jax.ShapeDtypeStruct((ctx.shape[0], D_DIFF), BF16)] * 2
    if mix is not None:
        att, zf, _ = mix
        by_batch = pl.BlockSpec((tm, D_FOURIER), lambda i: (i % tiles_per_seq, i // tiles_per_seq))
        in_specs += [tile(D_DIFF), by_batch]
        args += [att, zf]
    if final_g is not None:
        in_specs.append(_const_spec((1, D_MODEL)))
        args.append(final_g)
    kern = functools.partial(_ffn_kernel, staged=staged, mod_base=mod_base, has_mix=mix is not None,
                             final_norm=final_g is not None, n_main=n_main if tail is not None else None)
    return pl.pallas_call(
        kern,
        grid=(n_tiles,),
        in_specs=in_specs,
        out_specs=out_specs,
        out_shape=out_shape,
        scratch_shapes=staged.scratch,
        compiler_params=_params(("arbitrary",)),
        name="ffn_mix" if mix is not None else "ffn",
    )(*args)


def _rope(x, cos, sin_lo, sin_hi):
    return (x * cos + pltpu.roll(x, HEAD_W - ROPE_PAIRS, 1) * sin_lo
            + pltpu.roll(x, ROPE_PAIRS, 1) * sin_hi)


def _inproj_kernel(x_ref, mod_ref, g_ref, w_hbm, by_row_ref, by_col_ref, dft_ref,
                   q_ref, k_ref, v_ref, yc_ref, ys_ref, *scratch, staged, tiles_per_seq):
    (w_ref,) = staged.load([w_hbm], scratch)
    h = _rmsnorm(x_ref[...], g_ref[...]) * (1.0 + _mod_row(mod_ref, 4)) + _mod_row(mod_ref, 3)
    p = jnp.dot(h.astype(BF16), w_ref[...], preferred_element_type=F32)
    tm, grid_w = x_ref.shape[0], by_col_ref.shape[1]
    first_row = pl.multiple_of((pl.program_id(0) % tiles_per_seq) * (tm // grid_w), tm // grid_w)
    cos, slo, shi = [(by_row_ref[j, pl.ds(first_row, tm // grid_w), :][:, None, :]
                      + by_col_ref[j][None, :, :]).reshape(tm, HEAD_W) for j in range(3)]
    for hd in range(N_HEADS):
        lo = hd * HEAD_W
        q_ref[:, lo:lo + HEAD_W] = (_rope(p[:, lo:lo + HEAD_W], cos, slo, shi) * Q_SCALE).astype(BF16)
        k_ref[:, lo:lo + HEAD_W] = _rope(p[:, D_DIFF + lo:D_DIFF + lo + HEAD_W], cos, slo, shi).astype(BF16)
    v_ref[...] = p[:, 2 * D_DIFF:3 * D_DIFF].astype(BF16)
    y = jnp.dot(p[:, 3 * D_DIFF:].astype(BF16), dft_ref[...], preferred_element_type=F32)
    yc_ref[...] = y[:, :D_FOURIER].astype(BF16)
    ys_ref[...] = y[:, D_FOURIER:].astype(BF16)


def _inproj(x, mod3, mod_index, g, w_in, rope, chan_dft, *, tm):
    t = x.shape[0]
    by_row, by_col = rope
    tiles_per_seq = by_row.shape[1] * by_col.shape[1] // tm
    tile = lambda w: pl.BlockSpec((tm, w), lambda i: (i, 0))
    by_batch = pl.BlockSpec((tm, D_FOURIER), lambda i: (i % tiles_per_seq, i // tiles_per_seq))
    staged = _StagedWeights([w_in])
    wide = jax.ShapeDtypeStruct((t, D_DIFF), BF16)
    narrow = jax.ShapeDtypeStruct((tiles_per_seq * tm, t // (tiles_per_seq * tm) * D_FOURIER), BF16)
    return pl.pallas_call(
        functools.partial(_inproj_kernel, staged=staged, tiles_per_seq=tiles_per_seq),
        grid=(t // tm,),
        in_specs=[tile(D_MODEL),
                  pl.BlockSpec((1, 1, N_MOD * D_MODEL), lambda i: (mod_index(i), 0, 0)),
                  _const_spec((1, D_MODEL))] + staged.in_specs
                 + [_const_spec(by_row.shape), _const_spec(by_col.shape), _const_spec((D_FOURIER, 2 * D_FOURIER))],
        out_specs=[tile(D_DIFF)] * 3 + [by_batch] * 2,
        out_shape=[wide] * 3 + [narrow] * 2,
        scratch_shapes=staged.scratch,
        compiler_params=_params(("arbitrary",)),
        name="inproj",
    )(x, mod3, g, w_in, by_row, by_col, chan_dft)


def _max_key_norm_sq(kk):
    sq = (kk.astype(F32) ** 2).astype(BF16)
    r = lax.broadcasted_iota(jnp.int32, (HEAD_W, HEAD_W), 0)
    c = lax.broadcasted_iota(jnp.int32, (HEAD_W, HEAD_W), 1)
    sel = jnp.where((c == 0) == (r < HEAD_DIM), 1.0, 0.0) * jnp.where(c < 2, 1.0, 0.0)
    sums = jnp.dot(sq, sel.astype(BF16), preferred_element_type=F32)
    biggest = jnp.max(sums, axis=0, keepdims=True) * (1.0 + 2.0 ** -7)
    return biggest[:, 0:1], biggest[:, 1:2]


def _attn_kernel(lam_ref, g_ref, q_ref, k_ref, kc_ref, v_ref, vc_ref, o_ref, *rest, tk, exact_shift):
    lmin_ref = None if exact_shift else rest[0]
    vt_ref, keys_ref, kn_ref = rest[-3:]
    tq = q_ref.shape[1]
    n_latent = k_ref.shape[1]

    @pl.when(pl.program_id(2) == 0)
    def _():
        if not exact_shift:
            lmin_ref[...] = jnp.full(lmin_ref.shape, jnp.finfo(F32).max, F32)
        keys_ref[0:n_latent] = k_ref[0]
        keys_ref[n_latent:] = kc_ref[0]
        vt_ref[:, 0:n_latent] = v_ref[0].T
        vt_ref[:, n_latent:] = vc_ref[0].T
        (a1, a2), (b1, b2) = _max_key_norm_sq(k_ref[0]), _max_key_norm_sq(kc_ref[0])
        kn_ref[...] = jnp.sqrt(jnp.concatenate([jnp.broadcast_to(jnp.maximum(a1, b1), (1, tq)),
                                                 jnp.broadcast_to(jnp.maximum(a2, b2), (1, tq))], axis=1))

    lv = lam_ref[...]
    lam = (jnp.exp(jnp.sum(lv[0:1] * lv[1:2], axis=1, keepdims=True))
           - jnp.exp(jnp.sum(lv[2:3] * lv[3:4], axis=1, keepdims=True)) + LAMBDA_INIT)

    qt = q_ref[0].T
    none = jnp.zeros((HEAD_DIM, tq), BF16)
    qcat = jnp.concatenate([jnp.concatenate([qt[:HEAD_DIM], none], axis=0),
                            jnp.concatenate([none, qt[HEAD_DIM:]], axis=0)], axis=1)
    n_keys = keys_ref.shape[0]
    bounds = [0] + [n_keys - (n_keys // tk - 1 - c) * tk for c in range(n_keys // tk)]
    chunks = [(keys_ref[lo:hi, :], vt_ref[:, lo:hi]) for lo, hi in zip(bounds[:-1], bounds[1:])]

    def scores(kk):
        return jnp.dot(kk, qcat, preferred_element_type=F32)

    def softmax_sums(shift):
        lsum, acc = 0.0, 0.0
        for kk, vt in chunks:
            e = jnp.exp2(scores(kk) - shift)
            lsum += jnp.sum(e.reshape(-1, F32_SUBLANES, 2 * tq), axis=0)
            acc += jnp.dot(vt, e.astype(BF16), preferred_element_type=F32)
        return jnp.sum(lsum, axis=0, keepdims=True), acc

    def finish(l, acc):
        ot = acc[:, :tq] * (1.0 / l[:, :tq]) - acc[:, tq:] * (lam / l[:, tq:])
        scale = lax.rsqrt(jnp.mean(ot * ot, axis=0, keepdims=True) + RMS_EPS)
        o_ref[0] = (ot * scale * (g_ref[...] * (1.0 - LAMBDA_INIT))).astype(BF16).T

    if exact_shift:
        m = None
        for kk, _ in chunks:
            cm = jnp.max(scores(kk), axis=0, keepdims=True)
            m = cm if m is None else jnp.maximum(m, cm)
        finish(*softmax_sums(m))
    else:
        sq = qt.astype(F32) ** 2
        qn = jnp.sqrt(jnp.concatenate([jnp.sum(sq[:HEAD_DIM], axis=0, keepdims=True),
                                       jnp.sum(sq[HEAD_DIM:], axis=0, keepdims=True)], axis=1))
        l, acc = softmax_sums(qn * kn_ref[...])
        finish(l, acc)
        smallest = l[:, 0:HEAD_W]
        for lo in range(HEAD_W, 2 * tq, HEAD_W):
            smallest = jnp.minimum(smallest, l[:, lo:lo + HEAD_W])
        lmin_ref[...] = jnp.minimum(lmin_ref[...], smallest)


def _attention(lamvec, subln_g, q, k, kc, v, vc, *, tq, tk, exact_shift):
    b, n, _ = q.shape
    c = kc.shape[1]
    blocks = n // tq
    head_block = lambda rows: pl.BlockSpec((1, rows, HEAD_W), lambda bi, hi, qi: (bi, 0, hi))
    query_block = pl.BlockSpec((1, tq, HEAD_W), lambda bi, hi, qi: (bi, qi, hi))
    out_specs, out_shape = query_block, jax.ShapeDtypeStruct((b, n, D_DIFF), BF16)
    if not exact_shift:
        out_specs = [out_specs, pl.BlockSpec((F32_SUBLANES, HEAD_W), lambda bi, hi, qi: (bi * N_HEADS + hi, 0))]
        out_shape = [out_shape, jax.ShapeDtypeStruct((b * N_HEADS * F32_SUBLANES, HEAD_W), F32)]
    return pl.pallas_call(
        functools.partial(_attn_kernel, tk=tk, exact_shift=exact_shift),
        grid=(b, N_HEADS, blocks),
        in_specs=[_const_spec(lamvec.shape), _const_spec((HEAD_W, 1)), query_block,
                  head_block(n), head_block(c), head_block(n), head_block(c)],
        out_specs=out_specs,
        out_shape=out_shape,
        scratch_shapes=[pltpu.VMEM((HEAD_W, n + c), BF16), pltpu.VMEM((n + c, HEAD_W), BF16),
                        pltpu.VMEM((1, 2 * tq), F32)],
        compiler_params=_params(("parallel", "parallel", "arbitrary")),
        name="diff_attn_exact" if exact_shift else "diff_attn",
    )(lamvec, subln_g, q, k, kc, v, vc)


def _dft_kernel(yc_ref, ys_ref, m1_ref, twr_ref, twi_ref, w2_ref, wf_ref, perm_ref, o_ref, br_ref, bi_ref):
    n1, cols = yc_ref.shape[0], yc_ref.shape[2]
    g = m1_ref.shape[0] // (2 * n1)
    per_step = yc_ref.shape[1] // g
    n2 = br_ref.shape[0] // n1
    steps1, steps2 = n2 // g // per_step, n1 // per_step
    i = pl.program_id(0)

    @pl.when(i < steps1)
    def _():
        for s in range(per_step):
            group = i * per_step + s
            mine = slice(s * g, (s + 1) * g)
            u = jnp.concatenate([yc_ref[:, mine, :].reshape(n1 * g, cols),
                                 ys_ref[:, mine, :].reshape(n1 * g, cols)], axis=0)
            a = jnp.dot(m1_ref[...], u, preferred_element_type=F32)
            ar, ai = a[:n1 * g], a[n1 * g:]
            tr, ti = twr_ref[group], twi_ref[group]
            br = (ar * tr - ai * ti).astype(BF16)
            bi = (ar * ti + ai * tr).astype(BF16)
            for c in range(n1):
                dst = pl.ds(pl.multiple_of(c * n2 + group * g, g), g)
                br_ref[dst, :] = br[c * g:(c + 1) * g]
                bi_ref[dst, :] = bi[c * g:(c + 1) * g]

    @pl.when((i >= steps1) & (i < steps1 + steps2))
    def _():
        w = wf_ref.shape[0]
        wf = wf_ref[...].astype(BF16)
        for s in range(per_step):
            src = pl.ds(pl.multiple_of(((i - steps1) * per_step + s) * n2, n2), n2)
            b = jnp.concatenate([br_ref[src, :], bi_ref[src, :]], axis=0)
            x = jnp.dot(w2_ref[...], b, preferred_element_type=F32)
            br_ref[src, :] = jnp.concatenate(
                [jnp.dot(x[:, lo:lo + w].astype(BF16), wf, preferred_element_type=F32).astype(BF16)
                 for lo in range(0, cols, w)], axis=1)

    @pl.when(i >= steps1 + steps2)
    def _():
        run = o_ref.shape[0] // n1
        first = (i - steps1 - steps2) * run
        rows = jnp.concatenate([br_ref[pl.ds(pl.multiple_of(c * n2 + first, run), run), :] for c in range(n1)], axis=0)
        o_ref[...] = jnp.dot(perm_ref[...], rows, preferred_element_type=F32).astype(BF16)


def _position_dft(yc, ys, wf, *, tile_rows):
    n, cols = yc.shape
    n1, g, per_step = DFT_N1, BF16_SUBLANES, DFT_UNITS_PER_STEP
    n2 = n // n1
    steps1, steps2, steps3 = n2 // g // per_step, n1 // per_step, n // tile_rows
    m1, twr, twi, w2 = _position_dft_tables(n, n1, g)
    perm = _row_permutation(n1, tile_rows)
    group = pl.BlockSpec((n1, g * per_step, cols), lambda i: (0, jnp.minimum(i, steps1 - 1), 0))
    return pl.pallas_call(
        _dft_kernel,
        grid=(steps1 + steps2 + steps3,),
        in_specs=[group, group, _const_spec(m1.shape), _const_spec(twr.shape), _const_spec(twi.shape),
                  _const_spec(w2.shape), _const_spec(wf.shape), _const_spec(perm.shape)],
        out_specs=pl.BlockSpec((tile_rows, cols), lambda i: (jnp.maximum(i - steps1 - steps2, 0), 0)),
        out_shape=jax.ShapeDtypeStruct((n, cols), BF16),
        scratch_shapes=[pltpu.VMEM((n, cols), BF16), pltpu.VMEM((n, cols), BF16)],
        compiler_params=_params(("arbitrary",)),
        name="position_dft",
    )(yc.reshape(n1, n2, cols), ys.reshape(n1, n2, cols), m1, twr, twi, w2, wf, perm)


def _rope_tables(n):
    rows = n // GRID_W
    lane = np.arange(HEAD_W)
    sub = lane % HEAD_DIM
    on_row_axis = jnp.asarray(sub // (2 * ROPE_PAIRS) == 0)[None, :]
    second_half = jnp.asarray((sub % (2 * ROPE_PAIRS)) // ROPE_PAIRS == 1)[None, :]
    inv_freq = ROPE_BASE ** (-jnp.asarray(sub % ROPE_PAIRS, dtype=F32) / ROPE_PAIRS)
    kinds = ((jnp.cos, True), (lambda a: -jnp.sin(a), ~second_half), (jnp.sin, second_half))

    def side(size, mine):
        ang = jnp.arange(size, dtype=F32)[:, None] * inv_freq[None, :]
        return jnp.stack([jnp.where(mine & keep, fn(ang), 0.0) for fn, keep in kinds])

    return side(rows, on_row_axis), side(GRID_W, ~on_row_axis)


def _channel_dft_table():
    c = np.arange(FOURIER_GROUP_DIM)
    ang = 2.0 * np.pi * np.outer(c, c) / FOURIER_GROUP_DIM
    eye = np.eye(D_FOURIER // FOURIER_GROUP_DIM)
    scale = FOURIER_GROUP_DIM ** -0.5
    return np.concatenate([np.kron(eye, np.cos(ang)), np.kron(eye, np.sin(ang))], axis=1) * scale


def _position_dft_tables(n, n1, g):
    n2 = n // n1
    w1 = np.exp(-2j * np.pi * np.outer(np.arange(n1), np.arange(n1)) / n1)
    m = np.kron(w1, np.eye(g)) * n ** -0.5
    m1 = np.block([[m.real, m.imag], [m.imag, -m.real]])
    b = (g * np.arange(n2 // g)[:, None, None] + np.arange(g)[None, None, :])
    tw = np.exp(-2j * np.pi * b * np.arange(n1)[None, :, None] / n).reshape(n2 // g, n1 * g, 1)
    ang2 = 2.0 * np.pi * np.outer(np.arange(n2), np.arange(n2)) / n2
    w2 = np.concatenate([np.cos(ang2), np.sin(ang2)], axis=1)
    f32 = lambda t: jnp.asarray(t, dtype=F32)
    return f32(m1).astype(BF16), f32(tw.real), f32(tw.imag), f32(w2).astype(BF16)


def _row_permutation(n1, rows):
    d_per_tile = rows // n1
    p = np.zeros((rows, rows), np.float32)
    c, d = np.meshgrid(np.arange(n1), np.arange(d_per_tile), indexing="ij")
    p[(c + n1 * d).ravel(), (c * d_per_tile + d).ravel()] = 1.0
    return jnp.asarray(p).astype(BF16)


def kernel(x, c, ctx, c_ctx, w_ada, b_ada, norm1_g, ffn1_w_gate, ffn1_w_up, ffn1_w_down, norm_mix_g, w_in,
           lambda_q1, lambda_k1, lambda_q2, lambda_k2, subln_g, w_fourier, w_out, norm2_g, ffn2_w_gate,
           ffn2_w_up, ffn2_w_down, final_norm_g):
    b, n, d = x.shape
    n_ctx = ctx.shape[1]
    assert (d, w_ada.shape[0]) == (D_MODEL, 1) and b + 1 <= MOD_ROWS
    tiles = TILES
    mod_row = lambda rows: (lambda i: jnp.minimum(i // (n // rows), b))
    row = lambda g: g.reshape(1, -1)

    cc = jnp.zeros((MOD_ROWS, d), F32).at[:b].set(c).at[b].set(c_ctx)
    mod3 = _ada(cc, w_ada[0], b_ada).reshape(MOD_ROWS, 1, N_MOD * d)

    x1, kc, vc = _ffn(x.reshape(b * n, d), mod3, mod_row(tiles.ffn_rows), row(norm1_g), ffn1_w_gate[0], ffn1_w_up[0],
                      ffn1_w_down[0], tm=tiles.ffn_rows, mod_base=0,
                      tail=(ctx.reshape(b * n_ctx, d), row(norm_mix_g), w_in[0]))

    chan_dft = jnp.asarray(_channel_dft_table(), dtype=F32).astype(BF16)
    q, k, v, yc, ys = _inproj(x1, mod3, mod_row(tiles.inproj_rows), row(norm_mix_g), w_in[0], _rope_tables(n), chan_dft,
                              tm=tiles.inproj_rows)

    lamvec = jnp.concatenate([lambda_q1, lambda_k1, lambda_q2, lambda_k2], axis=0)
    seq = lambda a, rows: a.reshape(b, rows, a.shape[-1])
    attend = functools.partial(_attention, lamvec, subln_g.reshape(HEAD_W, 1), seq(q, n), seq(k, n), seq(kc, n_ctx),
                               seq(v, n), seq(vc, n_ctx), tq=tiles.attn_queries, tk=tiles.attn_keys)
    att, smallest_sum = attend(exact_shift=False)
    att = lax.cond(jnp.min(smallest_sum) < UNDERFLOW_GUARD, lambda: attend(exact_shift=True), lambda: att)

    zf = _position_dft(yc, ys, w_fourier[0], tile_rows=tiles.ffn_rows)

    out = _ffn(x1, mod3, mod_row(tiles.ffn_rows), row(norm2_g), ffn2_w_gate[0], ffn2_w_up[0], ffn2_w_down[0],
               tm=tiles.ffn_rows, mod_base=6, mix=(att.reshape(b * n, D_DIFF), zf, w_out[0]),
               tiles_per_seq=n // tiles.ffn_rows, final_g=row(final_norm_g))
    return out.reshape(b, n, d)
```
